```python
import jax, jax.numpy as jnp
from jax import lax
import numpy as np

D_MODEL = 4096
BATCH = 4
SEQ = 2048
DEPTH = 1

GRID_W = 64
CTX_LEN = 256
N_MOD = 6
EPS = 1e-6
HEAD_DIM = 128
ATTN_HEADS = D_MODEL // (2 * HEAD_DIM)
ATTN_KV_HEADS = ATTN_HEADS // 4
GQA_GROUP = ATTN_HEADS // ATTN_KV_HEADS
ATTN_WIDTH = ATTN_HEADS * HEAD_DIM
KV_WIDTH = ATTN_KV_HEADS * HEAD_DIM
AXIS_DIM = HEAD_DIM // 2
ROPE_THETA = 10000.0
Q_BLOCK = 128
ATTN_SCALE = HEAD_DIM ** -0.5
HGRN_EXPAND = 128
HGRN_HEADS = D_MODEL // (2 * HGRN_EXPAND)
HGRN_WIDTH = HGRN_HEADS * HGRN_EXPAND
HGRN_HEAD_V = HGRN_WIDTH // HGRN_HEADS
CHUNK = 64
MIX_WIDTH = ATTN_WIDTH + HGRN_WIDTH
HEAD_COLS = ATTN_WIDTH + 2 * HGRN_WIDTH
TAIL_COLS = 2 * KV_WIDTH + 3 * HGRN_WIDTH
IN_WIDTH = HEAD_COLS + TAIL_COLS
HEAD_SPLITS = [ATTN_WIDTH, ATTN_WIDTH + HGRN_WIDTH]
TAIL_SPLITS = [KV_WIDTH, 2 * KV_WIDTH, 2 * KV_WIDTH + HGRN_WIDTH, 2 * KV_WIDTH + 2 * HGRN_WIDTH]
COL_SPLITS = HEAD_SPLITS + [HEAD_COLS + s for s in [0] + TAIL_SPLITS]
N_EXPERTS = 32
TOP_K = 4
D_EXPERT = 3 * D_MODEL // 8
SWIGLU_LIMIT = 7.0
SWIGLU_ALPHA = 1.702

kernel_name = "hybrid_attn_hgrn2_moe_dit_layer"


def rms_norm(x, gain):
    xf = x.astype(jnp.float32)
    y = xf * lax.rsqrt(jnp.mean(xf * xf, axis=-1, keepdims=True) + EPS)
    return (y * gain.astype(jnp.float32)).astype(x.dtype)


def modulate(x, shift, scale):
    return x * (1 + scale) + shift


def ada_modulation(cond, w_ada, b_ada):
    m = jax.nn.silu(cond) @ w_ada + b_ada
    return m.reshape(*cond.shape[:-1], N_MOD, D_MODEL)


def heads(a, n):
    return a.reshape(*a.shape[:-1], n, a.shape[-1] // n)


def flip(a):
    return jnp.flip(a, axis=1)


def axial_rope_tables(rows):
    row_ids = jnp.repeat(jnp.arange(rows), GRID_W).astype(jnp.float32)
    col_ids = jnp.tile(jnp.arange(GRID_W), rows).astype(jnp.float32)
    inv_freq = ROPE_THETA ** (-jnp.arange(0, AXIS_DIM, 2, dtype=jnp.float32) / AXIS_DIM)
    ang_r = row_ids[:, None] * inv_freq[None, :]
    ang_c = col_ids[:, None] * inv_freq[None, :]
    return (jnp.cos(ang_r), jnp.sin(ang_r), jnp.cos(ang_c), jnp.sin(ang_c))


def rotate_half(x, cos, sin):
    x1, x2 = jnp.split(x, 2, axis=-1)
    cos = cos[None, :, None, :]
    sin = sin[None, :, None, :]
    return jnp.concatenate([x1 * cos - x2 * sin, x1 * sin + x2 * cos], axis=-1)


def apply_axial_rope(x, rope):
    cos_r, sin_r, cos_c, sin_c = rope
    x_row, x_col = jnp.split(x.astype(jnp.float32), 2, axis=-1)
    out = jnp.concatenate([rotate_half(x_row, cos_r, sin_r), rotate_half(x_col, cos_c, sin_c)], axis=-1)
    return out.astype(x.dtype)


def attend(q, keys, vals):
    s = jnp.einsum('bqhgd,bkhd->bhgqk', q, keys, preferred_element_type=jnp.float32) * ATTN_SCALE
    p = jax.nn.softmax(s, axis=-1).astype(vals.dtype)
    return jnp.einsum('bhgqk,bkhd->bqhgd', p, vals)


def hgrn2_forget(f_raw, lower_bound):
    f = lower_bound + (1 - lower_bound) * jax.nn.sigmoid(f_raw.astype(jnp.float32))
    return heads(1 - f, HGRN_HEADS), heads(jnp.log(f), HGRN_HEADS)


def hgrn2_chunk_scan(k, v, log_f, s0, q=None):
    B, L, H, _ = k.shape
    nc = L // CHUNK
    blk = lambda a: a.astype(jnp.float32).reshape(B, nc, CHUNK, H, a.shape[-1])
    k, v, log_f = blk(k), blk(v), blk(log_f)
    b = jnp.cumsum(log_f, axis=2)
    b_last = b[:, :, -1]
    u = jnp.einsum('bnshk,bnshv->bnhkv', k * jnp.exp(b_last[:, :, None] - b), v)
    decay = jnp.exp(b_last)
    with_out = q is not None

    def step(state, inp):
        a_c, u_c = inp
        return a_c[..., None] * state + u_c, (state if with_out else None)

    s_final, s_start = lax.scan(step, s0, (jnp.moveaxis(decay, 1, 0), jnp.moveaxis(u, 1, 0)))
    if not with_out:
        return None, s_final
    q = blk(q)
    b_mid = b[:, :, CHUNK // 2 - 1][:, :, None]
    a = jnp.einsum('bnthk,bnshk->bnhts', q * jnp.exp(b - b_mid), k * jnp.exp(b_mid - b))
    within_chunk = jnp.tril(jnp.ones((CHUNK, CHUNK), dtype=bool))
    a = jnp.where(within_chunk, a, 0.0)
    o = (jnp.einsum('bnhts,bnshv->bnthv', a, v)
         + jnp.einsum('bnthk,bnhkv->bnthv', q * jnp.exp(b), jnp.moveaxis(s_start, 0, 1)))
    return o.reshape(B, L, H, v.shape[-1]), s_final


def hgrn2_readout(o_sum, gate_raw, hg_gain, dtype):
    o = rms_norm(o_sum, hg_gain).astype(dtype) * jax.nn.silu(heads(gate_raw, HGRN_HEADS))
    return o.reshape(*o.shape[:-2], HGRN_WIDTH)


def token_mixers(h, hc, rope, lb, need_ctx_out, w_in, q_gain, k_gain, hg_gain, w_out):
    B, L, _ = h.shape
    C = hc.shape[1]
    proj = h @ w_in
    q_a, hq, hg, k_a, v_a, hf_fw, hf_bw, hi = jnp.split(proj, COL_SPLITS, axis=-1)
    proj_c = hc @ w_in[:, HEAD_COLS:]
    kc_a, vc_a, hcf_fw, hcf_bw, hci = jnp.split(proj_c, TAIL_SPLITS, axis=-1)
    if need_ctx_out:
        qc_a, hcq, hcg = jnp.split(hc @ w_in[:, :HEAD_COLS], HEAD_SPLITS, axis=-1)

    q = apply_axial_rope(rms_norm(heads(q_a, ATTN_HEADS), q_gain), rope)
    k = apply_axial_rope(rms_norm(heads(k_a, ATTN_KV_HEADS), k_gain), rope)
    v = heads(v_a, ATTN_KV_HEADS)
    kc = rms_norm(heads(kc_a, ATTN_KV_HEADS), k_gain)
    vc = heads(vc_a, ATTN_KV_HEADS)
    keys = jnp.concatenate([kc, k], axis=1)
    vals = jnp.concatenate([vc, v], axis=1)
    qb = jnp.moveaxis(q.reshape(B, L // Q_BLOCK, Q_BLOCK, ATTN_KV_HEADS, GQA_GROUP, HEAD_DIM), 1, 0)
    attn = lax.map(lambda q_blk: attend(q_blk, keys, vals), qb)
    attn = jnp.moveaxis(attn, 0, 1).reshape(B, L, ATTN_WIDTH)

    s0 = jnp.zeros((B, HGRN_HEADS, HGRN_EXPAND, HGRN_HEAD_V), jnp.float32)
    kcf, lcf = hgrn2_forget(hcf_fw, lb[0])
    kcb, lcb = hgrn2_forget(hcf_bw, lb[1])
    vch = heads(hci, HGRN_HEADS)
    qch = jax.nn.silu(heads(hcq, HGRN_HEADS)) if need_ctx_out else None
    oc_f, sc_f = hgrn2_chunk_scan(kcf, vch, lcf, s0, qch)
    oc_b, sc_b = hgrn2_chunk_scan(flip(kcb), flip(vch), flip(lcb), s0,
                                  flip(qch) if need_ctx_out else None)
    kf, lf = hgrn2_forget(hf_fw, lb[0])
    kb, lbk = hgrn2_forget(hf_bw, lb[1])
    qh = jax.nn.silu(heads(hq, HGRN_HEADS))
    vh = heads(hi, HGRN_HEADS)
    o_f, _ = hgrn2_chunk_scan(kf, vh, lf, sc_f, qh)
    o_b, _ = hgrn2_chunk_scan(flip(kb), flip(vh), flip(lbk), sc_b, flip(qh))
    hgrn = hgrn2_readout(o_f + flip(o_b), hg, hg_gain, h.dtype)

    y = jnp.concatenate([attn, hgrn], axis=-1) @ w_out
    y_c = None
    if need_ctx_out:
        qc = rms_norm(heads(qc_a, ATTN_HEADS), q_gain).reshape(B, C, ATTN_KV_HEADS, GQA_GROUP, HEAD_DIM)
        attn_c = attend(qc, kc, vc).reshape(B, C, ATTN_WIDTH)
        hgrn_c = hgrn2_readout(oc_f + flip(oc_b), hcg, hg_gain, hc.dtype)
        y_c = jnp.concatenate([attn_c, hgrn_c], axis=-1) @ w_out
    return y, y_c


def moe_ffn(x, w_router, b_router, w_gate, b_gate, w_up, b_up, w_down, b_down):
    shape = x.shape
    xt = x.reshape(-1, shape[-1])
    logits = (xt @ w_router).astype(jnp.float32) + b_router.astype(jnp.float32)
    top_vals, top_idx = lax.top_k(logits, TOP_K)
    top_w = jax.nn.softmax(top_vals, axis=-1)
    combine = jnp.sum(jax.nn.one_hot(top_idx, N_EXPERTS, dtype=jnp.float32) * top_w[..., None], axis=1)
    out = jnp.zeros(xt.shape, jnp.float32)
    for e in range(N_EXPERTS):
        g = jnp.minimum(xt @ w_gate[e] + b_gate[e], SWIGLU_LIMIT)
        u = jnp.clip(xt @ w_up[e] + b_up[e], -SWIGLU_LIMIT, SWIGLU_LIMIT)
        act = g * jax.nn.sigmoid(SWIGLU_ALPHA * g) * (u + 1)
        out = out + combine[:, e:e + 1] * (act @ w_down[e] + b_down[e])
    return out.astype(x.dtype).reshape(shape)


def hybrid_layer(x, xc, c, c_ctx, rope, lb, need_ctx_out, w_ada, b_ada, gains, w_in, q_gain, k_gain,
                 hg_gain, w_out, w_router, b_router, w_gate, b_gate, w_up, b_up, w_down, b_down):
    mod = ada_modulation(c, w_ada, b_ada)
    mod_c = ada_modulation(c_ctx, w_ada, b_ada)
    sh1, sc1, g1, sh2, sc2, g2 = [mod[:, i, None, :] for i in range(N_MOD)]
    h = modulate(rms_norm(x, gains[0]), sh1, sc1)
    hc = modulate(rms_norm(xc, gains[0]), mod_c[0], mod_c[1])
    y, y_c = token_mixers(h, hc, rope, lb, need_ctx_out, w_in, q_gain, k_gain, hg_gain, w_out)
    x = x + g1 * rms_norm(y, gains[1])
    h2 = modulate(rms_norm(x, gains[2]), sh2, sc2)
    x = x + g2 * rms_norm(moe_ffn(h2, w_router, b_router, w_gate, b_gate, w_up, b_up, w_down, b_down), gains[3])
    if need_ctx_out:
        xc = xc + mod_c[2] * rms_norm(y_c, gains[1])
        hc2 = modulate(rms_norm(xc, gains[2]), mod_c[3], mod_c[4])
        xc = xc + mod_c[5] * rms_norm(moe_ffn(hc2, w_router, b_router, w_gate, b_gate, w_up, b_up, w_down, b_down), gains[3])
    return x, xc


def setup_inputs(seed: int = 0) -> dict:
    key = jax.random.key(seed)
    ks = jax.random.split(key, 24)
    f32 = jnp.float32
    nrm = lambda k, shape, s: jax.random.normal(k, shape, f32) * s
    D, L = D_MODEL, DEPTH
    return {
        "x": nrm(ks[0], (BATCH, SEQ, D), 1.0),
        "c": nrm(ks[1], (BATCH, D), 1.0),
        "ctx": nrm(ks[2], (BATCH, CTX_LEN, D), 1.0),
        "c_ctx": nrm(ks[3], (D,), 1.0),
        "w_ada": nrm(ks[4], (L, D, N_MOD * D), 0.5 * D ** -0.5),
        "b_ada": nrm(ks[5], (L, N_MOD * D), 0.02),
        "norm_gains": 1.0 + nrm(ks[6], (L, 4, D), 0.05),
        "w_in": nrm(ks[7], (L, D, IN_WIDTH), D ** -0.5),
        "q_norm_gain": 1.0 + nrm(ks[8], (L, HEAD_DIM), 0.05),
        "k_norm_gain": 1.0 + nrm(ks[9], (L, HEAD_DIM), 0.05),
        "hgrn_lb_logits": nrm(ks[10], (DEPTH + 1, 2, HGRN_WIDTH), 0.5),
        "hgrn_norm_gain": 1.0 + nrm(ks[11], (L, HGRN_HEAD_V), 0.05),
        "w_out": nrm(ks[12], (L, MIX_WIDTH, D), MIX_WIDTH ** -0.5),
        "w_router": nrm(ks[13], (L, D, N_EXPERTS), D ** -0.5),
        "b_router": nrm(ks[14], (L, N_EXPERTS), 0.01),
        "w_gate": nrm(ks[15], (L, N_EXPERTS, D, D_EXPERT), D ** -0.5),
        "b_gate": nrm(ks[16], (L, N_EXPERTS, D_EXPERT), 0.02),
        "w_up": nrm(ks[17], (L, N_EXPERTS, D, D_EXPERT), D ** -0.5),
        "b_up": nrm(ks[18], (L, N_EXPERTS, D_EXPERT), 0.02),
        "w_down": nrm(ks[19], (L, N_EXPERTS, D_EXPERT, D), D_EXPERT ** -0.5),
        "b_down": nrm(ks[20], (L, N_EXPERTS, D), 0.02),
    }


def reference(x, c, ctx, c_ctx, w_ada, b_ada, norm_gains, w_in, q_norm_gain, k_norm_gain,
              hgrn_lb_logits, hgrn_norm_gain, w_out, w_router, b_router,
              w_gate, b_gate, w_up, b_up, w_down, b_down):
    rows = x.shape[1] // GRID_W
    rope = axial_rope_tables(rows)
    lower_bounds = jnp.cumsum(jax.nn.softmax(hgrn_lb_logits.astype(jnp.float32), axis=0), axis=0)
    xc = ctx
    for layer in range(DEPTH):
        need_ctx_out = layer + 1 < DEPTH
        x, xc = hybrid_layer(x, xc, c, c_ctx, rope, lower_bounds[layer], need_ctx_out,
                             w_ada[layer], b_ada[layer], norm_gains[layer], w_in[layer],
                             q_norm_gain[layer], k_norm_gain[layer], hgrn_norm_gain[layer], w_out[layer],
                             w_router[layer], b_router[layer], w_gate[layer], b_gate[layer],
                             w_up[layer], b_up[layer], w_down[layer], b_down[layer])
    return x
```

```python
import functools

import jax
import jax.numpy as jnp
from jax import lax
from jax.experimental import pallas as pl
from jax.experimental.pallas import tpu as pltpu

F32 = jnp.float32
BF16 = jnp.bfloat16
U32 = jnp.uint32
I32 = jnp.int32

GRID_W = 64
N_MOD = 6
EPS = 1e-6
HEAD_DIM = 128
GQA_GROUP = 4
ROPE_THETA = 10000.0
HGRN_EXPAND = 128
CHUNK = 64
TOP_K = 4
SWIGLU_LIMIT = 7.0
SWIGLU_ALPHA = 1.702
MOD_ROWS = 8

V7X_VMEM_LIMIT = 56 * 1024 * 1024


def _cp(*sem):
    return pltpu.CompilerParams(dimension_semantics=sem, vmem_limit_bytes=V7X_VMEM_LIMIT)


def _tile(n, pref, mult):
    if n <= pref:
        return n
    t = (pref // mult) * mult
    while t > mult and n % t:
        t -= mult
    assert n % t == 0, (n, pref, mult)
    return t


def _dot(a, b):
    return jnp.dot(a, b, preferred_element_type=F32)


def _dot_nt(a, b):
    return lax.dot_general(a, b, (((1,), (1,)), ((), ())), preferred_element_type=F32)


def _dot_tn(a, b):
    return lax.dot_general(a, b, (((0,), (0,)), ((), ())), preferred_element_type=F32)


def _silu(v):
    return v * jax.nn.sigmoid(v)


def _rms(v, gain):
    return v * lax.rsqrt(jnp.mean(v * v, axis=-1, keepdims=True) + EPS) * gain


def _pack_bf16_pair(lo, hi):
    lo_bits = lax.bitcast_convert_type(lo.astype(BF16).astype(F32), U32)
    hi_bits = lax.bitcast_convert_type(hi.astype(BF16).astype(F32), U32)
    return (hi_bits & jnp.uint32(0xFFFF0000)) | (lo_bits >> 16)


def _unpack_bf16_pair(w):
    lo = lax.bitcast_convert_type(w << 16, F32)
    hi = lax.bitcast_convert_type(w & jnp.uint32(0xFFFF0000), F32)
    return lo, hi


def _ada_kernel(c_ref, w_ref, b_ref, o_ref):
    s = _silu(c_ref[...]).astype(BF16)
    o_ref[...] = _dot(s, w_ref[...].astype(BF16)) + b_ref[...]


def _ada_modulation(cond, w_ada, b_ada):
    rows, d = cond.shape
    n = w_ada.shape[1]
    tn = _tile(n, 512, 128)
    return pl.pallas_call(
        _ada_kernel,
        out_shape=jax.ShapeDtypeStruct((rows, n), F32),
        grid=(n // tn,),
        in_specs=[pl.BlockSpec((rows, d), lambda j: (0, 0)),
                  pl.BlockSpec((d, tn), lambda j: (0, j)),
                  pl.BlockSpec((1, tn), lambda j: (0, j))],
        out_specs=pl.BlockSpec((rows, tn), lambda j: (0, j)),
        compiler_params=_cp("arbitrary"),
        name="ada_modulation",
    )(cond, w_ada, b_ada.reshape(1, n))


def _norm_mod_kernel(nx, tiles_per_batch, ctx_row, x_ref, c_ref, g_ref, sh_ref, sc_ref, o_ref):
    i = pl.program_id(0)
    is_x = i < nx
    v = jnp.where(is_x, x_ref[...], c_ref[...])
    r = jnp.where(is_x, i // tiles_per_batch, ctx_row)
    shift = sh_ref[pl.ds(r, 1), :]
    scale = sc_ref[pl.ds(r, 1), :]
    o_ref[...] = (_rms(v, g_ref[...]) * (1.0 + scale) + shift).astype(o_ref.dtype)


def _norm_modulate(x2, c2, gain, shift_tab, scale_tab, seq, ctx_row):
    m, d = x2.shape
    mc = c2.shape[0]
    tm = _tile(_gcd(seq, mc), 256, 16)
    nx, ncx = m // tm, mc // tm
    return pl.pallas_call(
        functools.partial(_norm_mod_kernel, nx, seq // tm, ctx_row),
        out_shape=jax.ShapeDtypeStruct((m + mc, d), BF16),
        grid=(nx + ncx,),
        in_specs=[pl.BlockSpec((tm, d), lambda i: (jnp.minimum(i, nx - 1), 0)),
                  pl.BlockSpec((tm, d), lambda i: (jnp.maximum(i - nx, 0), 0)),
                  pl.BlockSpec((1, d), lambda i: (0, 0)),
                  pl.BlockSpec((MOD_ROWS, d), lambda i: (0, 0)),
                  pl.BlockSpec((MOD_ROWS, d), lambda i: (0, 0))],
        out_specs=pl.BlockSpec((tm, d), lambda i: (i, 0)),
        compiler_params=_cp("arbitrary"),
        name="norm_modulate",
    )(x2, c2, gain.reshape(1, d), shift_tab, scale_tab)


def _gcd(a, b):
    while b:
        a, b = b, a % b
    return a


def _mm_kernel(a_ref, w_ref, o_ref):
    o_ref[...] = _dot(a_ref[...], w_ref[...].astype(BF16)).astype(o_ref.dtype)


def _project(a, w, col0, ncols, m_rows, out_dtype):
    k = a.shape[1]
    tn = _tile(_gcd(ncols, col0) if col0 else ncols, 512, 128)
    tm = _tile(m_rows, 1024, 16)
    jb = col0 // tn
    return pl.pallas_call(
        _mm_kernel,
        out_shape=jax.ShapeDtypeStruct((m_rows, ncols), out_dtype),
        grid=(ncols // tn, m_rows // tm),
        in_specs=[pl.BlockSpec((tm, k), lambda j, i: (i, 0)),
                  pl.BlockSpec((k, tn), lambda j, i: (0, jb + j))],
        out_specs=pl.BlockSpec((tm, tn), lambda j, i: (i, j)),
        compiler_params=_cp("arbitrary", "arbitrary"),
        name="in_projection",
    )(a, w)


def _mm2_kernel(ka, a1_ref, a2_ref, w_ref, o_ref):
    w = w_ref[...].astype(BF16)
    o_ref[...] = _dot(a1_ref[...], w[:ka]) + _dot(a2_ref[...], w[ka:])


def _out_projection(a1, a2, w):
    m, ka = a1.shape
    kb = a2.shape[1]
    n = w.shape[1]
    tn = _tile(n, 512, 128)
    tm = _tile(m, 1024, 16)
    return pl.pallas_call(
        functools.partial(_mm2_kernel, ka),
        out_shape=jax.ShapeDtypeStruct((m, n), F32),
        grid=(n // tn, m // tm),
        in_specs=[pl.BlockSpec((tm, ka), lambda j, i: (i, 0)),
                  pl.BlockSpec((tm, kb), lambda j, i: (i, 0)),
                  pl.BlockSpec((ka + kb, tn), lambda j, i: (0, j))],
        out_specs=pl.BlockSpec((tm, tn), lambda j, i: (i, j)),
        compiler_params=_cp("arbitrary", "arbitrary"),
        name="out_projection",
    )(a1, a2, w)


def _rope(v, cos, sin_signed):
    lane = lax.broadcasted_iota(I32, v.shape, 1)
    partner = jnp.where((lane & 32) == 0, pltpu.roll(v, HEAD_DIM - 32, 1), pltpu.roll(v, 32, 1))
    return v * cos + partner * sin_signed


def _attn_kernel(tq, seq, ctx, q_ref, kl_ref, kc_ref, vl_ref, vc_ref, qg_ref, kg_ref, cos_ref, sin_ref,
                 o_ref, k_sc, v_sc):
    qi = pl.program_id(2)

    @pl.when(qi == 0)
    def _():
        kg = kg_ref[...]
        k_sc[0:ctx, :] = _rms(kc_ref[...].astype(F32), kg).astype(BF16)
        kl = _rope(_rms(kl_ref[...].astype(F32), kg), cos_ref[...], sin_ref[...])
        k_sc[ctx:ctx + seq, :] = kl.astype(BF16)
        v_sc[0:ctx, :] = vc_ref[...]
        v_sc[ctx:ctx + seq, :] = vl_ref[...]

    r0 = pl.multiple_of(qi * tq, tq)
    cos = cos_ref[pl.ds(r0, tq), :]
    sin = sin_ref[pl.ds(r0, tq), :]
    qg = qg_ref[...] * (HEAD_DIM ** -0.5)
    keys = k_sc[...]
    vals = v_sc[...]
    for h in range(GQA_GROUP):
        sl = slice(h * HEAD_DIM, (h + 1) * HEAD_DIM)
        q = _rope(_rms(q_ref[:, sl].astype(F32), qg), cos, sin).astype(BF16)
        s = _dot_nt(q, keys)
        p = jnp.exp(s - jnp.max(s, axis=-1, keepdims=True))
        denom = jnp.sum(p, axis=-1, keepdims=True)
        o = _dot(p.astype(BF16), vals) / denom
        o_ref[:, sl] = o.astype(o_ref.dtype)


def _attention(p_head, p_kv, q_gain, k_gain, cos, sin_signed, batch, seq, ctx, kv_heads):
    tq = _tile(seq, 256, 16)
    nq = seq // tq
    gw = GQA_GROUP * HEAD_DIM
    ctx_blk0 = batch * seq // ctx
    return pl.pallas_call(
        functools.partial(_attn_kernel, tq, seq, ctx),
        out_shape=jax.ShapeDtypeStruct((batch * seq, kv_heads * gw), BF16),
        grid=(batch, kv_heads, nq),
        in_specs=[pl.BlockSpec((tq, gw), lambda b, h, q: (b * nq + q, h)),
                  pl.BlockSpec((seq, HEAD_DIM), lambda b, h, q: (b, h)),
                  pl.BlockSpec((ctx, HEAD_DIM), lambda b, h, q: (ctx_blk0 + b, h)),
                  pl.BlockSpec((seq, HEAD_DIM), lambda b, h, q: (b, kv_heads + h)),
                  pl.BlockSpec((ctx, HEAD_DIM), lambda b, h, q: (ctx_blk0 + b, kv_heads + h)),
                  pl.BlockSpec((1, HEAD_DIM), lambda b, h, q: (0, 0)),
                  pl.BlockSpec((1, HEAD_DIM), lambda b, h, q: (0, 0)),
                  pl.BlockSpec((seq, HEAD_DIM), lambda b, h, q: (0, 0)),
                  pl.BlockSpec((seq, HEAD_DIM), lambda b, h, q: (0, 0))],
        out_specs=pl.BlockSpec((tq, gw), lambda b, h, q: (b * nq + q, h)),
        scratch_shapes=[pltpu.VMEM((ctx + seq, HEAD_DIM), BF16),
                        pltpu.VMEM((ctx + seq, HEAD_DIM), BF16)],
        compiler_params=_cp("arbitrary", "arbitrary", "arbitrary"),
        name="gqa_attention",
    )(p_head, p_kv, p_kv, p_kv, p_kv, q_gain.reshape(1, HEAD_DIM), k_gain.reshape(1, HEAD_DIM), cos, sin_signed)


def _hgrn_kernel(hpb, seq, ctx, lb_ref, gain_ref, q_ref, g_ref, ff_ref, fb_ref, v_ref,
                 cff_ref, cfb_ref, cv_ref, o_ref, st_ref, of_ref, ob_ref):
    nc, ncc = seq // CHUNK, ctx // CHUNK
    row = lax.broadcasted_iota(I32, (CHUNK, CHUNK), 0)
    col = lax.broadcasted_iota(I32, (CHUNK, CHUNK), 1)
    masks = (col <= row, col >= row)
    tris = tuple(m.astype(BF16) for m in masks)

    st_ref[...] = jnp.zeros(st_ref.shape, F32)

    def chunk(state_idx, raw, v, q, lb, rev):
        f = lb + (1.0 - lb) * jax.nn.sigmoid(raw)
        k = 1.0 - f
        logf = jnp.log(f)
        hi = logf.astype(BF16)
        lo = (logf - hi.astype(F32)).astype(BF16)
        b = _dot(tris[rev], hi) + _dot(tris[rev], lo)
        last, mid = (0, CHUNK // 2) if rev else (CHUNK - 1, CHUNK // 2 - 1)
        b_last = b[last:last + 1, :]
        st = st_ref[state_idx]
        kd = (k * jnp.exp(b_last - b)).astype(BF16)
        st_ref[state_idx] = st * jnp.exp(b_last) + _dot_tn(v, kd)
        if q is None:
            return None
        b_mid = b[mid:mid + 1, :]
        qt = (q * jnp.exp(b - b_mid)).astype(BF16)
        kt = (k * jnp.exp(b_mid - b)).astype(BF16)
        a = jnp.where(masks[rev], _dot_nt(qt, kt), 0.0)
        qs = (q * jnp.exp(b)).astype(BF16)
        return _dot(a.astype(BF16), v) + _dot_nt(qs, st.astype(BF16))

    def ctx_body(c, carry):
        r_f = pl.multiple_of(c * CHUNK, CHUNK)
        r_b = pl.multiple_of((ncc - 1 - c) * CHUNK, CHUNK)
        for g in range(hpb):
            sl = slice(g * HGRN_EXPAND, (g + 1) * HGRN_EXPAND)
            chunk(2 * g, cff_ref[pl.ds(r_f, CHUNK), sl], cv_ref[pl.ds(r_f, CHUNK), sl], None, lb_ref[0:1, sl], 0)
            chunk(2 * g + 1, cfb_ref[pl.ds(r_b, CHUNK), sl], cv_ref[pl.ds(r_b, CHUNK), sl], None, lb_ref[1:2, sl], 1)
        return carry

    lax.fori_loop(0, ncc, ctx_body, 0)

    def seq_body(c, carry):
        r_f = pl.multiple_of(c * CHUNK, CHUNK)
        r_b = pl.multiple_of((nc - 1 - c) * CHUNK, CHUNK)
        for g in range(hpb):
            sl = slice(g * HGRN_EXPAND, (g + 1) * HGRN_EXPAND)
            for rev, r, f_ref, out_ref in ((0, r_f, ff_ref, of_ref), (1, r_b, fb_ref, ob_ref)):
                q = _silu(q_ref[pl.ds(r, CHUNK), sl].astype(F32))
                out_ref[pl.ds(r, CHUNK), sl] = chunk(2 * g + rev, f_ref[pl.ds(r, CHUNK), sl],
                                                     v_ref[pl.ds(r, CHUNK), sl], q, lb_ref[rev:rev + 1, sl], rev)
        return carry

    lax.fori_loop(0, nc, seq_body, 0)

    rb = _tile(seq, 256, 8)

    def read_body(i, carry):
        r = pl.multiple_of(i * rb, rb)
        for g in range(hpb):
            sl = slice(g * HGRN_EXPAND, (g + 1) * HGRN_EXPAND)
            o = of_ref[pl.ds(r, rb), sl] + ob_ref[pl.ds(r, rb), sl]
            y = _rms(o, gain_ref[...]) * _silu(g_ref[pl.ds(r, rb), sl].astype(F32))
            o_ref[pl.ds(r, rb), sl] = y.astype(o_ref.dtype)
        return carry

    lax.fori_loop(0, seq // rb, read_body, 0)


def _hgrn(p_head, p_f, p_v, lower_bounds, hg_gain, batch, seq, ctx, attn_width, hgrn_width):
    heads = hgrn_width // HGRN_EXPAND
    hpb = 2 if heads % 2 == 0 else 1
    bw = hpb * HGRN_EXPAND
    nh = heads // hpb
    q0, g0 = attn_width // bw, (attn_width + hgrn_width) // bw
    ctx_blk0 = batch * seq // ctx
    lat = lambda c0: pl.BlockSpec((seq, bw), lambda b, h: (b, c0 + h))
    cx = lambda c0: pl.BlockSpec((ctx, bw), lambda b, h: (ctx_blk0 + b, c0 + h))
    return pl.pallas_call(
        functools.partial(_hgrn_kernel, hpb, seq, ctx),
        out_shape=jax.ShapeDtypeStruct((batch * seq, hgrn_width), BF16),
        grid=(batch, nh),
        in_specs=[pl.BlockSpec((2, bw), lambda b, h: (0, h)),
                  pl.BlockSpec((1, HGRN_EXPAND), lambda b, h: (0, 0)),
                  lat(q0), lat(g0), lat(0), lat(nh), lat(0), cx(0), cx(nh), cx(0)],
        out_specs=pl.BlockSpec((seq, bw), lambda b, h: (b, h)),
        scratch_shapes=[pltpu.VMEM((2 * hpb, HGRN_EXPAND, HGRN_EXPAND), F32),
                        pltpu.VMEM((seq, bw), F32),
                        pltpu.VMEM((seq, bw), F32)],
        compiler_params=_cp("arbitrary", "arbitrary"),
        name="hgrn2_scan",
    )(lower_bounds, hg_gain.reshape(1, HGRN_EXPAND), p_head, p_head, p_f, p_f, p_v, p_f, p_f, p_v)


def _epilogue_kernel(tiles_per_batch, x_ref, y_ref, g1n_ref, g2n_ref, gate_ref, sh_ref, sc_ref, wr_ref, br_ref,
                     x1_ref, hp_ref, lg_ref):
    r = pl.program_id(0) // tiles_per_batch
    x1 = x_ref[...] + gate_ref[pl.ds(r, 1), :] * _rms(y_ref[...], g1n_ref[...])
    x1_ref[...] = x1
    h = _rms(x1, g2n_ref[...]) * (1.0 + sc_ref[pl.ds(r, 1), :]) + sh_ref[pl.ds(r, 1), :]
    half = h.shape[1] // 2
    hp_ref[...] = _pack_bf16_pair(h[:, :half], h[:, half:])
    w = wr_ref[...]
    w_hi = w.astype(BF16)
    w_lo = (w - w_hi.astype(F32)).astype(BF16)
    h_hi = h.astype(BF16)
    h_lo = (h - h_hi.astype(F32)).astype(BF16)
    lg_ref[...] = _dot_nt(w_hi, h_hi) + _dot_nt(w_hi, h_lo) + _dot_nt(w_lo, h_hi) + br_ref[...]


def _epilogue(x2, y, gain1, gain2, gate_tab, shift_tab, scale_tab, w_router_t, b_router, seq):
    m, d = x2.shape
    ne = w_router_t.shape[0]
    tm = _tile(seq, 256, 128)
    row = lambda: pl.BlockSpec((tm, d), lambda i: (i, 0))
    tab = lambda: pl.BlockSpec((MOD_ROWS, d), lambda i: (0, 0))
    vec = lambda: pl.BlockSpec((1, d), lambda i: (0, 0))
    return pl.pallas_call(
        functools.partial(_epilogue_kernel, seq // tm),
        out_shape=(jax.ShapeDtypeStruct((m, d), F32),
                   jax.ShapeDtypeStruct((m, d // 2), U32),
                   jax.ShapeDtypeStruct((ne, m), F32)),
        grid=(m // tm,),
        in_specs=[row(), row(), vec(), vec(), tab(), tab(), tab(),
                  pl.BlockSpec((ne, d), lambda i: (0, 0)),
                  pl.BlockSpec((ne, 1), lambda i: (0, 0))],
        out_specs=(row(),
                   pl.BlockSpec((tm, d // 2), lambda i: (i, 0)),
                   pl.BlockSpec((ne, tm), lambda i: (0, i))),
        compiler_params=_cp("arbitrary"),
        name="residual_norm_router",
    )(x2, y, gain1.reshape(1, d), gain2.reshape(1, d), gate_tab, shift_tab, scale_tab,
      w_router_t, b_router.reshape(ne, 1))


def _route_kernel(lg_ref, idx_ref, wt_ref, pos_ref, cnt_ref, carry_ref):
    i = pl.program_id(0)
    ne, tn = lg_ref.shape

    @pl.when(i == 0)
    def _():
        carry_ref[...] = jnp.zeros(carry_ref.shape, F32)

    lg = lg_ref[...]
    eidx = lax.broadcasted_iota(I32, (ne, tn), 0)
    vals, hots = [], []
    for k in range(TOP_K):
        m = jnp.max(lg, axis=0, keepdims=True)
        sel = jnp.min(jnp.where(lg == m, eidx, ne), axis=0, keepdims=True)
        hot = eidx == sel
        idx_ref[k:k + 1, :] = sel
        vals.append(m)
        hots.append(hot)
        lg = jnp.where(hot, -jnp.inf, lg)
    exps = [jnp.exp(v - vals[0]) for v in vals]
    denom = exps[0] + exps[1] + exps[2] + exps[3]
    for k in range(TOP_K):
        wt_ref[k:k + 1, :] = exps[k] / denom

    member = hots[0] | hots[1] | hots[2] | hots[3]
    t_row = lax.broadcasted_iota(I32, (tn, tn), 0)
    t_col = lax.broadcasted_iota(I32, (tn, tn), 1)
    before = (t_row < t_col).astype(BF16)
    rank = _dot(member.astype(BF16), before) + carry_ref[:, 0:1]
    for k in range(TOP_K):
        pos_ref[k:k + 1, :] = jnp.sum(jnp.where(hots[k], rank, 0.0), axis=0, keepdims=True).astype(I32)
    carry_ref[...] = carry_ref[...] + jnp.sum(member.astype(F32), axis=1, keepdims=True)
    cnt_ref[...] = carry_ref[...].astype(I32)


def _route(logits_t):
    ne, m = logits_t.shape
    tn = _tile(m, 512, 128)
    out4 = lambda: pl.BlockSpec((TOP_K, tn), lambda i: (0, i))
    return pl.pallas_call(
        _route_kernel,
        out_shape=(jax.ShapeDtypeStruct((TOP_K, m), I32),
                   jax.ShapeDtypeStruct((TOP_K, m), F32),
                   jax.ShapeDtypeStruct((TOP_K, m), I32),
                   jax.ShapeDtypeStruct((ne, 128), I32)),
        grid=(m // tn,),
        in_specs=[pl.BlockSpec((ne, tn), lambda i: (0, i))],
        out_specs=(out4(), out4(), out4(), pl.BlockSpec((ne, 128), lambda i: (0, 0))),
        scratch_shapes=[pltpu.VMEM((ne, 128), F32)],
        compiler_params=_cp("arbitrary"),
        name="top4_route",
    )(logits_t)


def _gather_rows_kernel(tm, tok_ref, src_hbm, o_ref, sem):
    def row_copy(r):
        return pltpu.make_async_copy(src_hbm.at[pl.ds(tok_ref[0, 0, r], 1)], o_ref.at[pl.ds(r, 1)], sem)

    def start(r, c):
        row_copy(r).start()
        return c

    def wait(r, c):
        row_copy(r).wait()
        return c

    lax.fori_loop(0, tm, start, 0)
    lax.fori_loop(0, tm, wait, 0)


def _gather_rows(src, tok_of_slot, tm):
    s_pad = tok_of_slot.shape[0]
    w = src.shape[1]
    nt = s_pad // tm
    return pl.pallas_call(
        functools.partial(_gather_rows_kernel, tm),
        out_shape=jax.ShapeDtypeStruct((s_pad, w), src.dtype),
        grid=(nt,),
        in_specs=[pl.BlockSpec((1, 1, tm), lambda i: (i, 0, 0), memory_space=pltpu.SMEM),
                  pl.BlockSpec(memory_space=pl.ANY)],
        out_specs=pl.BlockSpec((tm, w), lambda i: (i, 0)),
        scratch_shapes=[pltpu.SemaphoreType.DMA],
        compiler_params=_cp("arbitrary"),
        name="moe_dispatch",
    )(tok_of_slot.reshape(nt, 1, tm), src)


def _expert_up_kernel(te_ref, nu_ref, xp_ref, wg_ref, wu_ref, bg_ref, bu_ref, h_ref):
    i = pl.program_id(1)

    @pl.when(i < nu_ref[0])
    def _():
        x_lo, x_hi = _unpack_bf16_pair(xp_ref[...])
        x_lo = x_lo.astype(BF16)
        x_hi = x_hi.astype(BF16)
        half = x_lo.shape[1]

        def proj(w_ref, b_ref):
            w = w_ref[0].astype(BF16)
            return _dot(x_lo, w[:half]) + _dot(x_hi, w[half:]) + b_ref[0]

        g = jnp.minimum(proj(wg_ref, bg_ref), SWIGLU_LIMIT)
        u = jnp.clip(proj(wu_ref, bu_ref), -SWIGLU_LIMIT, SWIGLU_LIMIT)
        h_ref[...] = (g * jax.nn.sigmoid(SWIGLU_ALPHA * g) * (u + 1.0)).astype(h_ref.dtype)

    @pl.when(i >= nu_ref[0])
    def _():
        h_ref[...] = jnp.zeros(h_ref.shape, h_ref.dtype)


def _expert_up(tile_expert, n_used, xs_p, w_gate, w_up, b_gate, b_up, tm):
    s_pad, half = xs_p.shape
    ne, d, f = w_gate.shape
    fc = _tile(f, 512, 128)
    nt = s_pad // tm
    wspec = lambda: pl.BlockSpec((1, d, fc), lambda c, i, te, nu: (te[i], 0, c))
    bspec = lambda: pl.BlockSpec((1, 1, fc), lambda c, i, te, nu: (te[i], 0, c))
    return pl.pallas_call(
        _expert_up_kernel,
        out_shape=jax.ShapeDtypeStruct((s_pad, f), BF16),
        grid_spec=pltpu.PrefetchScalarGridSpec(
            num_scalar_prefetch=2,
            grid=(f // fc, nt),
            in_specs=[pl.BlockSpec((tm, half), lambda c, i, te, nu: (i, 0)),
                      wspec(), wspec(), bspec(), bspec()],
            out_specs=pl.BlockSpec((tm, fc), lambda c, i, te, nu: (i, c))),
        compiler_params=_cp("arbitrary", "arbitrary"),
        name="expert_gate_up",
    )(tile_expert, n_used, xs_p, w_gate, w_up, b_gate.reshape(ne, 1, f), b_up.reshape(ne, 1, f))


def _expert_down_kernel(te_ref, nu_ref, h_ref, wd_ref, bd_ref, y_ref):
    i = pl.program_id(1)

    @pl.when(i < nu_ref[0])
    def _():
        y = _dot(h_ref[...], wd_ref[0].astype(BF16)) + bd_ref[0]
        half = y.shape[1] // 2
        y_ref[...] = _pack_bf16_pair(y[:, :half], y[:, half:])

    @pl.when(i >= nu_ref[0])
    def _():
        y_ref[...] = jnp.zeros(y_ref.shape, y_ref.dtype)


def _expert_down(tile_expert, n_used, h, w_down, b_down, tm, tn):
    s_pad, f = h.shape
    ne, _, d = w_down.shape
    nt = s_pad // tm
    return pl.pallas_call(
        _expert_down_kernel,
        out_shape=jax.ShapeDtypeStruct((s_pad, d // 2), U32),
        grid_spec=pltpu.PrefetchScalarGridSpec(
            num_scalar_prefetch=2,
            grid=(d // tn, nt),
            in_specs=[pl.BlockSpec((tm, f), lambda c, i, te, nu: (i, 0)),
                      pl.BlockSpec((1, f, tn), lambda c, i, te, nu: (te[i], 0, c)),
                      pl.BlockSpec((1, 1, tn), lambda c, i, te, nu: (te[i], 0, c))],
            out_specs=pl.BlockSpec((tm, tn // 2), lambda c, i, te, nu: (i, c))),
        compiler_params=_cp("arbitrary", "arbitrary"),
        name="expert_down",
    )(tile_expert, n_used, h, w_down, b_down.reshape(ne, 1, d))


def _combine_kernel(tm, tn, tiles_per_batch, slot_ref, yp_hbm, wt_ref, x1_ref, gn_ref, gate_ref, o_ref, buf, sem):
    def row_copy(j):
        k = j // tm
        r = j - k * tm
        return pltpu.make_async_copy(yp_hbm.at[pl.ds(slot_ref[0, 0, j], 1)], buf.at[k, pl.ds(r, 1)], sem)

    def start(j, c):
        row_copy(j).start()
        return c

    def wait(j, c):
        row_copy(j).wait()
        return c

    lax.fori_loop(0, TOP_K * tm, start, 0)
    lax.fori_loop(0, TOP_K * tm, wait, 0)

    d = o_ref.shape[1]
    hw = tn // 2
    wt = wt_ref[...]
    parts = []
    for c in range(d // tn):
        lo_acc = jnp.zeros((tm, hw), F32)
        hi_acc = jnp.zeros((tm, hw), F32)
        for k in range(TOP_K):
            lo, hi = _unpack_bf16_pair(buf[k, :, c * hw:(c + 1) * hw])
            wk = wt[:, k:k + 1]
            lo_acc = lo_acc + wk * lo
            hi_acc = hi_acc + wk * hi
        parts += [lo_acc, hi_acc]
    moe = jnp.concatenate(parts, axis=1)
    r = pl.program_id(0) // tiles_per_batch
    o_ref[...] = x1_ref[...] + gate_ref[pl.ds(r, 1), :] * _rms(moe, gn_ref[...])


def _combine(slots, yp, wts, x1, gain, gate_tab, seq, tn):
    m, d = x1.shape
    tm = _tile(seq, 128, 8)
    nt = m // tm
    slot_tiles = slots.reshape(TOP_K, nt, tm).transpose(1, 0, 2).reshape(nt, 1, TOP_K * tm)
    return pl.pallas_call(
        functools.partial(_combine_kernel, tm, tn, seq // tm),
        out_shape=jax.ShapeDtypeStruct((m, d), F32),
        grid=(nt,),
        in_specs=[pl.BlockSpec((1, 1, TOP_K * tm), lambda i: (i, 0, 0), memory_space=pltpu.SMEM),
                  pl.BlockSpec(memory_space=pl.ANY),
                  pl.BlockSpec((tm, TOP_K), lambda i: (i, 0)),
                  pl.BlockSpec((tm, d), lambda i: (i, 0)),
                  pl.BlockSpec((1, d), lambda i: (0, 0)),
                  pl.BlockSpec((MOD_ROWS, d), lambda i: (0, 0))],
        out_specs=pl.BlockSpec((tm, d), lambda i: (i, 0)),
        scratch_shapes=[pltpu.VMEM((TOP_K, tm, d // 2), U32), pltpu.SemaphoreType.DMA],
        compiler_params=_cp("arbitrary"),
        name="moe_combine",
    )(slot_tiles, yp, wts, x1, gain.reshape(1, d), gate_tab)


def _rope_tables(seq):
    rows = seq // GRID_W
    axis_dim = HEAD_DIM // 2
    row_ids = jnp.repeat(jnp.arange(rows), GRID_W).astype(F32)
    col_ids = jnp.tile(jnp.arange(GRID_W), rows).astype(F32)
    inv_freq = ROPE_THETA ** (-jnp.arange(0, axis_dim, 2, dtype=F32) / axis_dim)
    ang_r = row_ids[:, None] * inv_freq[None, :]
    ang_c = col_ids[:, None] * inv_freq[None, :]
    cos = jnp.concatenate([jnp.cos(ang_r)] * 2 + [jnp.cos(ang_c)] * 2, axis=1)
    sin = jnp.concatenate([-jnp.sin(ang_r), jnp.sin(ang_r), -jnp.sin(ang_c), jnp.sin(ang_c)], axis=1)
    return cos, sin


def _routing_tables(idx, pos, counts, n_tokens, n_experts, tm, s_pad):
    padded = ((counts + tm - 1) // tm) * tm
    ends = jnp.cumsum(padded)
    starts = ends - padded
    slots = starts[idx] + pos
    tok = jnp.broadcast_to(jnp.arange(n_tokens, dtype=I32)[None, :], slots.shape)
    tok_of_slot = jnp.zeros((s_pad,), I32).at[slots.reshape(-1)].set(tok.reshape(-1))
    tile_start = jnp.arange(s_pad // tm, dtype=I32) * tm
    tile_expert = jnp.minimum(jnp.searchsorted(ends, tile_start, side="right"), n_experts - 1).astype(I32)
    n_used = (ends[-1] // tm).astype(I32).reshape(1)
    return slots, tok_of_slot, tile_expert, n_used


def _layer(x, xc, c, c_ctx, lower_bounds, w_ada, b_ada, gains, w_in, q_gain, k_gain, hg_gain, w_out,
           w_router, b_router, w_gate, b_gate, w_up, b_up, w_down, b_down):
    batch, seq, d = x.shape
    ctx = xc.shape[1]
    attn_width = d // 2
    hgrn_width = d // 2
    kv_heads = attn_width // HEAD_DIM // GQA_GROUP
    kv_width = kv_heads * HEAD_DIM
    head_cols = attn_width + 2 * hgrn_width
    n_experts = w_router.shape[1]
    m, mc = batch * seq, batch * ctx
    assert batch < MOD_ROWS and seq % ctx == 0

    cond = jnp.zeros((MOD_ROWS, d), F32).at[:batch].set(c).at[batch].set(c_ctx)
    mod = _ada_modulation(cond, w_ada, b_ada)
    sh1, sc1, g1, sh2, sc2, g2 = [mod[:, i * d:(i + 1) * d] for i in range(N_MOD)]

    x2 = x.reshape(m, d)
    hh = _norm_modulate(x2, xc.reshape(mc, d), gains[0], sh1, sc1, seq, batch)

    p_head = _project(hh, w_in, 0, head_cols, m, BF16)
    p_kv = _project(hh, w_in, head_cols, 2 * kv_width, m + mc, BF16)
    p_f = _project(hh, w_in, head_cols + 2 * kv_width, 2 * hgrn_width, m + mc, F32)
    p_v = _project(hh, w_in, head_cols + 2 * kv_width + 2 * hgrn_width, hgrn_width, m + mc, BF16)

    cos, sin = _rope_tables(seq)
    attn = _attention(p_head, p_kv, q_gain, k_gain, cos, sin, batch, seq, ctx, kv_heads)
    hgrn = _hgrn(p_head, p_f, p_v, lower_bounds, hg_gain, batch, seq, ctx, attn_width, hgrn_width)
    y = _out_projection(attn, hgrn, w_out)

    x1, h2p, logits_t = _epilogue(x2, y, gains[1], gains[2], g1, sh2, sc2, w_router.T, b_router, seq)
    idx, wts, pos, cnt = _route(logits_t)

    tm = 256
    s_pad = -(-(TOP_K * m + n_experts * (tm - 1)) // tm) * tm
    slots, tok_of_slot, tile_expert, n_used = _routing_tables(idx, pos, cnt[:, 0], m, n_experts, tm, s_pad)
    xs_p = _gather_rows(h2p, tok_of_slot, tm)
    hmid = _expert_up(tile_expert, n_used, xs_p, w_gate, w_up, b_gate, b_up, tm)
    tn = _tile(d, 1024, 256)
    yp = _expert_down(tile_expert, n_used, hmid, w_down, b_down, tm, tn)
    out = _combine(slots, yp, wts.T, x1, gains[3], g2, seq, tn)
    return out.reshape(batch, seq, d)


def kernel(x, c, ctx, c_ctx, w_ada, b_ada, norm_gains, w_in, q_norm_gain, k_norm_gain, hgrn_lb_logits,
           hgrn_norm_gain, w_out, w_router, b_router, w_gate, b_gate, w_up, b_up, w_down, b_down):
    assert w_ada.shape[0] == 1, "single-layer stack only"
    lower_bounds = jnp.cumsum(jax.nn.softmax(hgrn_lb_logits.astype(F32), axis=0), axis=0)
    return _layer(x, ctx, c, c_ctx, lower_bounds[0], w_ada[0], b_ada[0], norm_gains[0], w_in[0],
                  q_norm_gain[0], k_norm_gain[0], hgrn_norm_gain[0], w_out[0], w_router[0], b_router[0],
                  w_gate[0], b_gate[0], w_up[0], b_up[0], w_down[0], b_down[0])
```

```python
import functools

import jax
import jax.numpy as jnp
from jax import lax
from jax.experimental import pallas as pl
from jax.experimental.pallas import tpu as pltpu

F32 = jnp.float32
BF16 = jnp.bfloat16
U32 = jnp.uint32
I32 = jnp.int32

GRID_W = 64
N_MOD = 6
EPS = 1e-6
HEAD_DIM = 128
GQA_GROUP = 4
ROPE_THETA = 10000.0
HGRN_EXPAND = 128
CHUNK = 64
TOP_K = 4
SWIGLU_LIMIT = 7.0
SWIGLU_ALPHA = 1.702
MOD_ROWS = 8

V7X_VMEM_LIMIT = 56 * 1024 * 1024


def _cp(*sem):
    return pltpu.CompilerParams(dimension_semantics=sem, vmem_limit_bytes=V7X_VMEM_LIMIT)


def _tile(n, pref, mult):
    if n <= pref:
        return n
    t = (pref // mult) * mult
    while t > mult and n % t:
        t -= mult
    assert n % t == 0, (n, pref, mult)
    return t


def _dot(a, b):
    return jnp.dot(a, b, preferred_element_type=F32)


def _dot_nt(a, b):
    return lax.dot_general(a, b, (((1,), (1,)), ((), ())), preferred_element_type=F32)


def _dot_tn(a, b):
    return lax.dot_general(a, b, (((0,), (0,)), ((), ())), preferred_element_type=F32)


def _silu(v):
    return v * jax.nn.sigmoid(v)


def _rms(v, gain):
    return v * lax.rsqrt(jnp.mean(v * v, axis=-1, keepdims=True) + EPS) * gain


def _pack_bf16_pair(lo, hi):
    lo_bits = lax.bitcast_convert_type(lo.astype(BF16).astype(F32), U32)
    hi_bits = lax.bitcast_convert_type(hi.astype(BF16).astype(F32), U32)
    return (hi_bits & jnp.uint32(0xFFFF0000)) | (lo_bits >> 16)


def _unpack_bf16_pair(w):
    lo = lax.bitcast_convert_type(w << 16, F32)
    hi = lax.bitcast_convert_type(w & jnp.uint32(0xFFFF0000), F32)
    return lo, hi


def _ada_kernel(c_ref, w_ref, b_ref, o_ref):
    s = _silu(c_ref[...]).astype(BF16)
    o_ref[...] = _dot(s, w_ref[...].astype(BF16)) + b_ref[...]


def _ada_modulation(cond, w_ada, b_ada):
    rows, d = cond.shape
    n = w_ada.shape[1]
    tn = _tile(n, 512, 128)
    return pl.pallas_call(
        _ada_kernel,
        out_shape=jax.ShapeDtypeStruct((rows, n), F32),
        grid=(n // tn,),
        in_specs=[pl.BlockSpec((rows, d), lambda j: (0, 0)),
                  pl.BlockSpec((d, tn), lambda j: (0, j)),
                  pl.BlockSpec((1, tn), lambda j: (0, j))],
        out_specs=pl.BlockSpec((rows, tn), lambda j: (0, j)),
        compiler_params=_cp("arbitrary"),
        name="ada_modulation",
    )(cond, w_ada, b_ada.reshape(1, n))


def _norm_mod_kernel(nx, tiles_per_batch, ctx_row, x_ref, c_ref, g_ref, sh_ref, sc_ref, o_ref):
    i = pl.program_id(0)
    is_x = i < nx
    v = jnp.where(is_x, x_ref[...], c_ref[...])
    r = jnp.where(is_x, i // tiles_per_batch, ctx_row)
    shift = sh_ref[pl.ds(r, 1), :]
    scale = sc_ref[pl.ds(r, 1), :]
    o_ref[...] = (_rms(v, g_ref[...]) * (1.0 + scale) + shift).astype(o_ref.dtype)


def _norm_modulate(x2, c2, gain, shift_tab, scale_tab, seq, ctx_row):
    m, d = x2.shape
    mc = c2.shape[0]
    tm = _tile(_gcd(seq, mc), 256, 16)
    nx, ncx = m // tm, mc // tm
    return pl.pallas_call(
        functools.partial(_norm_mod_kernel, nx, seq // tm, ctx_row),
        out_shape=jax.ShapeDtypeStruct((m + mc, d), BF16),
        grid=(nx + ncx,),
        in_specs=[pl.BlockSpec((tm, d), lambda i: (jnp.minimum(i, nx - 1), 0)),
                  pl.BlockSpec((tm, d), lambda i: (jnp.maximum(i - nx, 0), 0)),
                  pl.BlockSpec((1, d), lambda i: (0, 0)),
                  pl.BlockSpec((MOD_ROWS, d), lambda i: (0, 0)),
                  pl.BlockSpec((MOD_ROWS, d), lambda i: (0, 0))],
        out_specs=pl.BlockSpec((tm, d), lambda i: (i, 0)),
        compiler_params=_cp("arbitrary"),
        name="norm_modulate",
    )(x2, c2, gain.reshape(1, d), shift_tab, scale_tab)


def _gcd(a, b):
    while b:
        a, b = b, a % b
    return a


def _mm_kernel(a_ref, w_ref, o_ref):
    o_ref[...] = _dot(a_ref[...], w_ref[...].astype(BF16)).astype(o_ref.dtype)


def _project(a, w, col0, ncols, m_rows, out_dtype):
    k = a.shape[1]
    tn = _tile(_gcd(ncols, col0) if col0 else ncols, 512, 128)
    tm = _tile(m_rows, 1024, 16)
    jb = col0 // tn
    return pl.pallas_call(
        _mm_kernel,
        out_shape=jax.ShapeDtypeStruct((m_rows, ncols), out_dtype),
        grid=(ncols // tn, m_rows // tm),
        in_specs=[pl.BlockSpec((tm, k), lambda j, i: (i, 0)),
                  pl.BlockSpec((k, tn), lambda j, i: (0, jb + j))],
        out_specs=pl.BlockSpec((tm, tn), lambda j, i: (i, j)),
        compiler_params=_cp("arbitrary", "arbitrary"),
        name="in_projection",
    )(a, w)


def _mm2_kernel(ka, a1_ref, a2_ref, w_ref, o_ref):
    w = w_ref[...].astype(BF16)
    o_ref[...] = _dot(a1_ref[...], w[:ka]) + _dot(a2_ref[...], w[ka:])


def _out_projection(a1, a2, w):
    m, ka = a1.shape
    kb = a2.shape[1]
    n = w.shape[1]
    tn = _tile(n, 512, 128)
    tm = _tile(m, 1024, 16)
    return pl.pallas_call(
        functools.partial(_mm2_kernel, ka),
        out_shape=jax.ShapeDtypeStruct((m, n), F32),
        grid=(n // tn, m // tm),
        in_specs=[pl.BlockSpec((tm, ka), lambda j, i: (i, 0)),
                  pl.BlockSpec((tm, kb), lambda j, i: (i, 0)),
                  pl.BlockSpec((ka + kb, tn), lambda j, i: (0, j))],
        out_specs=pl.BlockSpec((tm, tn), lambda j, i: (i, j)),
        compiler_params=_cp("arbitrary", "arbitrary"),
        name="out_projection",
    )(a1, a2, w)


def _rope(v, cos, sin_signed):
    lane = lax.broadcasted_iota(I32, v.shape, 1)
    partner = jnp.where((lane & 32) == 0, pltpu.roll(v, HEAD_DIM - 32, 1), pltpu.roll(v, 32, 1))
    return v * cos + partner * sin_signed


def _attn_kernel(tq, seq, ctx, q_ref, kl_ref, kc_ref, vl_ref, vc_ref, qg_ref, kg_ref, cos_ref, sin_ref,
                 o_ref, k_sc, v_sc):
    qi = pl.program_id(2)

    @pl.when(qi == 0)
    def _():
        kg = kg_ref[...]
        k_sc[0:ctx, :] = _rms(kc_ref[...].astype(F32), kg).astype(BF16)
        kl = _rope(_rms(kl_ref[...].astype(F32), kg), cos_ref[...], sin_ref[...])
        k_sc[ctx:ctx + seq, :] = kl.astype(BF16)
        v_sc[0:ctx, :] = vc_ref[...]
        v_sc[ctx:ctx + seq, :] = vl_ref[...]

    r0 = pl.multiple_of(qi * tq, tq)
    cos = cos_ref[pl.ds(r0, tq), :]
    sin = sin_ref[pl.ds(r0, tq), :]
    qg = qg_ref[...] * (HEAD_DIM ** -0.5)
    keys = k_sc[...]
    vals = v_sc[...]
    for h in range(GQA_GROUP):
        sl = slice(h * HEAD_DIM, (h + 1) * HEAD_DIM)
        q = _rope(_rms(q_ref[:, sl].astype(F32), qg), cos, sin).astype(BF16)
        s = _dot_nt(q, keys)
        p = jnp.exp(s - jnp.max(s, axis=-1, keepdims=True))
        denom = jnp.sum(p, axis=-1, keepdims=True)
        o = _dot(p.astype(BF16), vals) / denom
        o_ref[:, sl] = o.astype(o_ref.dtype)


def _attention(p_head, p_kv, q_gain, k_gain, cos, sin_signed, batch, seq, ctx, kv_heads):
    tq = _tile(seq, 256, 16)
    nq = seq // tq
    gw = GQA_GROUP * HEAD_DIM
    ctx_blk0 = batch * seq // ctx
    return pl.pallas_call(
        functools.partial(_attn_kernel, tq, seq, ctx),
        out_shape=jax.ShapeDtypeStruct((batch * seq, kv_heads * gw), BF16),
        grid=(batch, kv_heads, nq),
        in_specs=[pl.BlockSpec((tq, gw), lambda b, h, q: (b * nq + q, h)),
                  pl.BlockSpec((seq, HEAD_DIM), lambda b, h, q: (b, h)),
                  pl.BlockSpec((ctx, HEAD_DIM), lambda b, h, q: (ctx_blk0 + b, h)),
                  pl.BlockSpec((seq, HEAD_DIM), lambda b, h, q: (b, kv_heads + h)),
                  pl.BlockSpec((ctx, HEAD_DIM), lambda b, h, q: (ctx_blk0 + b, kv_heads + h)),
                  pl.BlockSpec((1, HEAD_DIM), lambda b, h, q: (0, 0)),
                  pl.BlockSpec((1, HEAD_DIM), lambda b, h, q: (0, 0)),
                  pl.BlockSpec((seq, HEAD_DIM), lambda b, h, q: (0, 0)),
                  pl.BlockSpec((seq, HEAD_DIM), lambda b, h, q: (0, 0))],
        out_specs=pl.BlockSpec((tq, gw), lambda b, h, q: (b * nq + q, h)),
        scratch_shapes=[pltpu.VMEM((ctx + seq, HEAD_DIM), BF16),
                        pltpu.VMEM((ctx + seq, HEAD_DIM), BF16)],
        compiler_params=_cp("arbitrary", "arbitrary", "arbitrary"),
        name="gqa_attention",
    )(p_head, p_kv, p_kv, p_kv, p_kv, q_gain.reshape(1, HEAD_DIM), k_gain.reshape(1, HEAD_DIM), cos, sin_signed)


def _hgrn_kernel(hpb, seq, ctx, lb_ref, gain_ref, q_ref, g_ref, ff_ref, fb_ref, v_ref,
                 cff_ref, cfb_ref, cv_ref, o_ref, st_ref, of_ref, ob_ref):
    nc, ncc = seq // CHUNK, ctx // CHUNK
    row = lax.broadcasted_iota(I32, (CHUNK, CHUNK), 0)
    col = lax.broadcasted_iota(I32, (CHUNK, CHUNK), 1)
    masks = (col <= row, col >= row)
    tris = tuple(m.astype(BF16) for m in masks)

    st_ref[...] = jnp.zeros(st_ref.shape, F32)

    def chunk(state_idx, raw, v, q, lb, rev):
        f = lb + (1.0 - lb) * jax.nn.sigmoid(raw)
        k = 1.0 - f
        logf = jnp.log(f)
        hi = logf.astype(BF16)
        lo = (logf - hi.astype(F32)).astype(BF16)
        b = _dot(tris[rev], hi) + _dot(tris[rev], lo)
        last, mid = (0, CHUNK // 2) if rev else (CHUNK - 1, CHUNK // 2 - 1)
        b_last = b[last:last + 1, :]
        st = st_ref[state_idx]
        kd = (k * jnp.exp(b_last - b)).astype(BF16)
        st_ref[state_idx] = st * jnp.exp(b_last) + _dot_tn(v, kd)
        if q is None:
            return None
        b_mid = b[mid:mid + 1, :]
        qt = (q * jnp.exp(b - b_mid)).astype(BF16)
        kt = (k * jnp.exp(b_mid - b)).astype(BF16)
        a = jnp.where(masks[rev], _dot_nt(qt, kt), 0.0)
        qs = (q * jnp.exp(b)).astype(BF16)
        return _dot(a.astype(BF16), v) + _dot_nt(qs, st.astype(BF16))

    def ctx_body(c, carry):
        r_f = pl.multiple_of(c * CHUNK, CHUNK)
        r_b = pl.multiple_of((ncc - 1 - c) * CHUNK, CHUNK)
        for g in range(hpb):
            sl = slice(g * HGRN_EXPAND, (g + 1) * HGRN_EXPAND)
            chunk(2 * g, cff_ref[pl.ds(r_f, CHUNK), sl], cv_ref[pl.ds(r_f, CHUNK), sl], None, lb_ref[0:1, sl], 0)
            chunk(2 * g + 1, cfb_ref[pl.ds(r_b, CHUNK), sl], cv_ref[pl.ds(r_b, CHUNK), sl], None, lb_ref[1:2, sl], 1)
        return carry

    lax.fori_loop(0, ncc, ctx_body, 0)

    def seq_body(c, carry):
        r_f = pl.multiple_of(c * CHUNK, CHUNK)
        r_b = pl.multiple_of((nc - 1 - c) * CHUNK, CHUNK)
        for g in range(hpb):
            sl = slice(g * HGRN_EXPAND, (g + 1) * HGRN_EXPAND)
            for rev, r, f_ref, out_ref in ((0, r_f, ff_ref, of_ref), (1, r_b, fb_ref, ob_ref)):
                q = _silu(q_ref[pl.ds(r, CHUNK), sl].astype(F32))
                out_ref[pl.ds(r, CHUNK), sl] = chunk(2 * g + rev, f_ref[pl.ds(r, CHUNK), sl],
                                                     v_ref[pl.ds(r, CHUNK), sl], q, lb_ref[rev:rev + 1, sl], rev)
        return carry

    lax.fori_loop(0, nc, seq_body, 0)

    rb = _tile(seq, 256, 8)

    def read_body(i, carry):
        r = pl.multiple_of(i * rb, rb)
        for g in range(hpb):
            sl = slice(g * HGRN_EXPAND, (g + 1) * HGRN_EXPAND)
            o = of_ref[pl.ds(r, rb), sl] + ob_ref[pl.ds(r, rb), sl]
            y = _rms(o, gain_ref[...]) * _silu(g_ref[pl.ds(r, rb), sl].astype(F32))
            o_ref[pl.ds(r, rb), sl] = y.astype(o_ref.dtype)
        return carry

    lax.fori_loop(0, seq // rb, read_body, 0)


def _hgrn(p_head, p_f, p_v, lower_bounds, hg_gain, batch, seq, ctx, attn_width, hgrn_width):
    heads = hgrn_width // HGRN_EXPAND
    hpb = 2 if heads % 2 == 0 else 1
    bw = hpb * HGRN_EXPAND
    nh = heads // hpb
    q0, g0 = attn_width // bw, (attn_width + hgrn_width) // bw
    ctx_blk0 = batch * seq // ctx
    lat = lambda c0: pl.BlockSpec((seq, bw), lambda b, h: (b, c0 + h))
    cx = lambda c0: pl.BlockSpec((ctx, bw), lambda b, h: (ctx_blk0 + b, c0 + h))
    return pl.pallas_call(
        functools.partial(_hgrn_kernel, hpb, seq, ctx),
        out_shape=jax.ShapeDtypeStruct((batch * seq, hgrn_width), BF16),
        grid=(batch, nh),
        in_specs=[pl.BlockSpec((2, bw), lambda b, h: (0, h)),
                  pl.BlockSpec((1, HGRN_EXPAND), lambda b, h: (0, 0)),
                  lat(q0), lat(g0), lat(0), lat(nh), lat(0), cx(0), cx(nh), cx(0)],
        out_specs=pl.BlockSpec((seq, bw), lambda b, h: (b, h)),
        scratch_shapes=[pltpu.VMEM((2 * hpb, HGRN_EXPAND, HGRN_EXPAND), F32),
                        pltpu.VMEM((seq, bw), F32),
                        pltpu.VMEM((seq, bw), F32)],
        compiler_params=_cp("arbitrary", "arbitrary"),
        name="hgrn2_scan",
    )(lower_bounds, hg_gain.reshape(1, HGRN_EXPAND), p_head, p_head, p_f, p_f, p_v, p_f, p_f, p_v)


def _epilogue_kernel(tiles_per_batch, x_ref, y_ref, g1n_ref, g2n_ref, gate_ref, sh_ref, sc_ref, wr_ref, br_ref,
                     x1_ref, hp_ref, lg_ref):
    r = pl.program_id(0) // tiles_per_batch
    x1 = x_ref[...] + gate_ref[pl.ds(r, 1), :] * _rms(y_ref[...], g1n_ref[...])
    x1_ref[...] = x1
    h = _rms(x1, g2n_ref[...]) * (1.0 + sc_ref[pl.ds(r, 1), :]) + sh_ref[pl.ds(r, 1), :]
    half = h.shape[1] // 2
    hp_ref[...] = _pack_bf16_pair(h[:, :half], h[:, half:])
    w = wr_ref[...]
    w_hi = w.astype(BF16)
    w_lo = (w - w_hi.astype(F32)).astype(BF16)
    h_hi = h.astype(BF16)
    h_lo = (h - h_hi.astype(F32)).astype(BF16)
    lg_ref[...] = _dot_nt(w_hi, h_hi) + _dot_nt(w_hi, h_lo) + _dot_nt(w_lo, h_hi) + br_ref[...]


def _epilogue(x2, y, gain1, gain2, gate_tab, shift_tab, scale_tab, w_router_t, b_router, seq):
    m, d = x2.shape
    ne = w_router_t.shape[0]
    tm = _tile(seq, 256, 128)
    row = lambda: pl.BlockSpec((tm, d), lambda i: (i, 0))
    tab = lambda: pl.BlockSpec((MOD_ROWS, d), lambda i: (0, 0))
    vec = lambda: pl.BlockSpec((1, d), lambda i: (0, 0))
    return pl.pallas_call(
        functools.partial(_epilogue_kernel, seq // tm),
        out_shape=(jax.ShapeDtypeStruct((m, d), F32),
                   jax.ShapeDtypeStruct((m, d // 2), U32),
                   jax.ShapeDtypeStruct((ne, m), F32)),
        grid=(m // tm,),
        in_specs=[row(), row(), vec(), vec(), tab(), tab(), tab(),
                  pl.BlockSpec((ne, d), lambda i: (0, 0)),
                  pl.BlockSpec((ne, 1), lambda i: (0, 0))],
        out_specs=(row(),
                   pl.BlockSpec((tm, d // 2), lambda i: (i, 0)),
                   pl.BlockSpec((ne, tm), lambda i: (0, i))),
        compiler_params=_cp("arbitrary"),
        name="residual_norm_router",
    )(x2, y, gain1.reshape(1, d), gain2.reshape(1, d), gate_tab, shift_tab, scale_tab,
      w_router_t, b_router.reshape(ne, 1))


def _route_kernel(lg_ref, idx_ref, wt_ref, pos_ref, cnt_ref, carry_ref):
    i = pl.program_id(0)
    ne, tn = lg_ref.shape

    @pl.when(i == 0)
    def _():
        carry_ref[...] = jnp.zeros(carry_ref.shape, F32)

    lg = lg_ref[...]
    eidx = lax.broadcasted_iota(I32, (ne, tn), 0)
    vals, hots = [], []
    for k in range(TOP_K):
        m = jnp.max(lg, axis=0, keepdims=True)
        sel = jnp.min(jnp.where(lg == m, eidx, ne), axis=0, keepdims=True)
        hot = eidx == sel
        idx_ref[k:k + 1, :] = sel
        vals.append(m)
        hots.append(hot)
        lg = jnp.where(hot, -jnp.inf, lg)
    exps = [jnp.exp(v - vals[0]) for v in vals]
    denom = exps[0] + exps[1] + exps[2] + exps[3]
    for k in range(TOP_K):
        wt_ref[k:k + 1, :] = exps[k] / denom

    member = hots[0] | hots[1] | hots[2] | hots[3]
    t_row = lax.broadcasted_iota(I32, (tn, tn), 0)
    t_col = lax.broadcasted_iota(I32, (tn, tn), 1)
    before = (t_row < t_col).astype(BF16)
    rank = _dot(member.astype(BF16), before) + carry_ref[:, 0:1]
    for k in range(TOP_K):
        pos_ref[k:k + 1, :] = jnp.sum(jnp.where(hots[k], rank, 0.0), axis=0, keepdims=True).astype(I32)
    carry_ref[...] = carry_ref[...] + jnp.sum(member.astype(F32), axis=1, keepdims=True)
    cnt_ref[...] = carry_ref[...].astype(I32)


def _route(logits_t):
    ne, m = logits_t.shape
    tn = _tile(m, 512, 128)
    out4 = lambda: pl.BlockSpec((TOP_K, tn), lambda i: (0, i))
    return pl.pallas_call(
        _route_kernel,
        out_shape=(jax.ShapeDtypeStruct((TOP_K, m), I32),
                   jax.ShapeDtypeStruct((TOP_K, m), F32),
                   jax.ShapeDtypeStruct((TOP_K, m), I32),
                   jax.ShapeDtypeStruct((ne, 128), I32)),
        grid=(m // tn,),
        in_specs=[pl.BlockSpec((ne, tn), lambda i: (0, i))],
        out_specs=(out4(), out4(), out4(), pl.BlockSpec((ne, 128), lambda i: (0, 0))),
        scratch_shapes=[pltpu.VMEM((ne, 128), F32)],
        compiler_params=_cp("arbitrary"),
        name="top4_route",
    )(logits_t)


def _slots_kernel(row_tile, lg_tile, idx_ref, pos_ref, cnt_ref, slot_ref, te_ref, start_ref, ends_ref):
    ne = cnt_ref.shape[0]
    tn = idx_ref.shape[1]
    tiles = (cnt_ref[...] + (row_tile - 1)) >> lg_tile
    e_row = lax.broadcasted_iota(I32, (ne, ne), 0)
    e_col = lax.broadcasted_iota(I32, (ne, ne), 1)
    upto = jnp.where(e_col <= e_row, 1.0, 0.0).astype(BF16)
    ends = _dot(upto, tiles.astype(F32).astype(BF16)).astype(I32)
    start = (ends - tiles) << lg_tile
    start_ref[...] = start
    ends_ref[...] = ends
    eidx = lax.broadcasted_iota(I32, (ne, tn), 0)
    start_col = start[:, 0:1]
    for k in range(TOP_K):
        hit = eidx == idx_ref[k:k + 1, :]
        slot_ref[k:k + 1, :] = jnp.sum(jnp.where(hit, start_col, 0), axis=0, keepdims=True) + pos_ref[k:k + 1, :]
    tile_i = lax.broadcasted_iota(I32, (ne, te_ref.shape[1]), 1)
    te = jnp.sum(jnp.where(tile_i >= ends[:, 0:1], 1, 0), axis=0, keepdims=True)
    te_ref[...] = jnp.minimum(te, ne - 1)


def _slots(idx, pos, cnt, row_tile, n_tiles):
    ne = cnt.shape[0]
    m = idx.shape[1]
    tn = _tile(m, 2048, 128)
    ntp = -(-n_tiles // 128) * 128
    lg_tile = row_tile.bit_length() - 1
    assert 1 << lg_tile == row_tile and n_tiles <= 256
    io4 = lambda: pl.BlockSpec((TOP_K, tn), lambda i: (0, i))
    small = lambda w: pl.BlockSpec((ne, w), lambda i: (0, 0))
    return pl.pallas_call(
        functools.partial(_slots_kernel, row_tile, lg_tile),
        out_shape=(jax.ShapeDtypeStruct((TOP_K, m), I32),
                   jax.ShapeDtypeStruct((1, ntp), I32),
                   jax.ShapeDtypeStruct((ne, 128), I32),
                   jax.ShapeDtypeStruct((ne, 128), I32)),
        grid=(m // tn,),
        in_specs=[io4(), io4(), small(128)],
        out_specs=(io4(), pl.BlockSpec((1, ntp), lambda i: (0, 0)), small(128), small(128)),
        compiler_params=_cp("arbitrary"),
        name="route_slots",
    )(idx, pos, cnt)


DMA_ISSUE_UNROLL = 8


def _dispatch_kernel(tm, row_tile, ne, n_tiles, start_ref, cnt_ref, nu_ref, slot_ref, x_ref, xs_hbm, zero_sc,
                     sem, zsem):
    i = pl.program_id(0)

    @pl.when(i == 0)
    def _():
        zero_sc[...] = jnp.zeros(zero_sc.shape, zero_sc.dtype)

        def zero_copy(row):
            return pltpu.make_async_copy(zero_sc.at[pl.ds(0, 1)], xs_hbm.at[pl.ds(row, 1)], zsem)

        def tile_copy(t):
            return pltpu.make_async_copy(zero_sc, xs_hbm.at[pl.ds(pl.multiple_of(t * row_tile, row_tile), row_tile)],
                                         zsem)

        def tile_start(t, c):
            tile_copy(t).start()
            return c

        def tile_wait(t, c):
            tile_copy(t).wait()
            return c

        lax.fori_loop(nu_ref[0], n_tiles, tile_start, 0)
        lax.fori_loop(nu_ref[0], n_tiles, tile_wait, 0)

        for e in range(ne):
            cnt = cnt_ref[e]
            first = start_ref[e] + cnt
            n_pad = (-cnt) & (row_tile - 1)

            def start(j, c, first=first):
                zero_copy(first + j).start()
                return c

            def wait(j, c, first=first):
                zero_copy(first + j).wait()
                return c

            lax.fori_loop(0, n_pad, start, 0)
            lax.fori_loop(0, n_pad, wait, 0)

    def issue(g, c):
        for u in range(DMA_ISSUE_UNROLL):
            r = g * DMA_ISSUE_UNROLL + u
            for k in range(TOP_K):
                s = slot_ref[0, 0, k * tm + r]
                pltpu.make_async_copy(x_ref.at[pl.ds(r, 1)], xs_hbm.at[pl.ds(s, 1)], sem).start()
        return c

    lax.fori_loop(0, tm // DMA_ISSUE_UNROLL, issue, 0)
    for k in range(TOP_K):
        pltpu.make_async_copy(x_ref, xs_hbm.at[pl.ds(0, tm)], sem).wait()


def _dispatch(starts, counts, n_used, slot_tiles, src, n_tiles, tm, row_tile):
    m, w = src.shape
    ne = starts.shape[0]
    return pl.pallas_call(
        functools.partial(_dispatch_kernel, tm, row_tile, ne, n_tiles),
        out_shape=jax.ShapeDtypeStruct((n_tiles * row_tile, w), src.dtype),
        grid_spec=pltpu.PrefetchScalarGridSpec(
            num_scalar_prefetch=3,
            grid=(m // tm,),
            in_specs=[pl.BlockSpec((1, 1, TOP_K * tm), lambda i, st, ct, nu: (i, 0, 0), memory_space=pltpu.SMEM),
                      pl.BlockSpec((tm, w), lambda i, st, ct, nu: (i, 0))],
            out_specs=pl.BlockSpec(memory_space=pl.ANY),
            scratch_shapes=[pltpu.VMEM((row_tile, w), src.dtype),
                            pltpu.SemaphoreType.DMA, pltpu.SemaphoreType.DMA]),
        compiler_params=_cp("arbitrary"),
        name="moe_dispatch",
    )(starts, counts, n_used, slot_tiles, src)


def _expert_up_kernel(te_ref, nu_ref, xp_ref, wg_ref, wu_ref, bg_ref, bu_ref, h_ref):
    @pl.when(pl.program_id(1) < nu_ref[0])
    def _():
        x_lo, x_hi = _unpack_bf16_pair(xp_ref[...])
        x_lo = x_lo.astype(BF16)
        x_hi = x_hi.astype(BF16)
        half = x_lo.shape[1]

        def proj(w_ref, b_ref):
            w = w_ref[0].astype(BF16)
            return _dot(x_lo, w[:half]) + _dot(x_hi, w[half:]) + b_ref[0]

        g = jnp.minimum(proj(wg_ref, bg_ref), SWIGLU_LIMIT)
        u = jnp.clip(proj(wu_ref, bu_ref), -SWIGLU_LIMIT, SWIGLU_LIMIT)
        h_ref[...] = (g * jax.nn.sigmoid(SWIGLU_ALPHA * g) * (u + 1.0)).astype(h_ref.dtype)

    @pl.when(pl.program_id(1) >= nu_ref[0])
    def _():
        h_ref[...] = jnp.zeros(h_ref.shape, h_ref.dtype)


def _used_tile(i, nu):
    return jnp.minimum(i, nu[0] - 1)


def _expert_up(tile_expert, n_used, xs_p, w_gate, w_up, b_gate, b_up, tm):
    s_pad, half = xs_p.shape
    ne, d, f = w_gate.shape
    fc = _tile(f, 512, 128)
    nt = s_pad // tm
    wspec = lambda: pl.BlockSpec((1, d, fc), lambda c, i, te, nu: (te[i], 0, c))
    bspec = lambda: pl.BlockSpec((1, 1, fc), lambda c, i, te, nu: (te[i], 0, c))
    return pl.pallas_call(
        _expert_up_kernel,
        out_shape=jax.ShapeDtypeStruct((s_pad, f), BF16),
        grid_spec=pltpu.PrefetchScalarGridSpec(
            num_scalar_prefetch=2,
            grid=(f // fc, nt),
            in_specs=[pl.BlockSpec((tm, half), lambda c, i, te, nu: (_used_tile(i, nu), 0)),
                      wspec(), wspec(), bspec(), bspec()],
            out_specs=pl.BlockSpec((tm, fc), lambda c, i, te, nu: (i, c))),
        compiler_params=_cp("arbitrary", "arbitrary"),
        name="expert_gate_up",
    )(tile_expert, n_used, xs_p, w_gate, w_up, b_gate.reshape(ne, 1, f), b_up.reshape(ne, 1, f))


def _expert_down_kernel(te_ref, nu_ref, h_ref, wd_ref, bd_ref, y_ref):
    @pl.when(pl.program_id(1) < nu_ref[0])
    def _():
        y = _dot(h_ref[...], wd_ref[0].astype(BF16)) + bd_ref[0]
        half = y.shape[1] // 2
        y_ref[...] = _pack_bf16_pair(y[:, :half], y[:, half:])

    @pl.when(pl.program_id(1) >= nu_ref[0])
    def _():
        y_ref[...] = jnp.zeros(y_ref.shape, y_ref.dtype)


def _expert_down(tile_expert, n_used, h, w_down, b_down, tm, tn):
    s_pad, f = h.shape
    ne, _, d = w_down.shape
    nt = s_pad // tm
    return pl.pallas_call(
        _expert_down_kernel,
        out_shape=jax.ShapeDtypeStruct((s_pad, d // 2), U32),
        grid_spec=pltpu.PrefetchScalarGridSpec(
            num_scalar_prefetch=2,
            grid=(d // tn, nt),
            in_specs=[pl.BlockSpec((tm, f), lambda c, i, te, nu: (_used_tile(i, nu), 0)),
                      pl.BlockSpec((1, f, tn), lambda c, i, te, nu: (te[i], 0, c)),
                      pl.BlockSpec((1, 1, tn), lambda c, i, te, nu: (te[i], 0, c))],
            out_specs=pl.BlockSpec((tm, tn // 2), lambda c, i, te, nu: (i, c))),
        compiler_params=_cp("arbitrary", "arbitrary"),
        name="expert_down",
    )(tile_expert, n_used, h, w_down, b_down.reshape(ne, 1, d))


def _combine_kernel(tm, tn, tiles_per_batch, nt, slot_ref, next_ref, yp_hbm, wt_ref, x1_ref, gn_ref, gate_ref,
                    o_ref, buf, sem):
    i = pl.program_id(0)

    def issue(s_ref, b):
        def body(g, c):
            for u in range(DMA_ISSUE_UNROLL):
                r = g * DMA_ISSUE_UNROLL + u
                for k in range(TOP_K):
                    pltpu.make_async_copy(yp_hbm.at[pl.ds(s_ref[0, 0, k * tm + r], 1)],
                                          buf.at[b, k, pl.ds(r, 1)], sem.at[b]).start()
            return c

        lax.fori_loop(0, tm // DMA_ISSUE_UNROLL, body, 0)

    @pl.when(i == 0)
    def _():
        issue(slot_ref, 0)

    @pl.when(i + 1 < nt)
    def _():
        issue(next_ref, (i + 1) % 2)

    b = i % 2
    for k in range(TOP_K):
        pltpu.make_async_copy(yp_hbm.at[pl.ds(0, tm)], buf.at[b, k], sem.at[b]).wait()

    d = o_ref.shape[1]
    hw = tn // 2
    wt = wt_ref[...]
    parts = []
    for c in range(d // tn):
        lo_acc = jnp.zeros((tm, hw), F32)
        hi_acc = jnp.zeros((tm, hw), F32)
        for k in range(TOP_K):
            lo, hi = _unpack_bf16_pair(buf[b, k, :, c * hw:(c + 1) * hw])
            wk = wt[:, k:k + 1]
            lo_acc = lo_acc + wk * lo
            hi_acc = hi_acc + wk * hi
        parts += [lo_acc, hi_acc]
    moe = jnp.concatenate(parts, axis=1)
    r = i // tiles_per_batch
    o_ref[...] = x1_ref[...] + gate_ref[pl.ds(r, 1), :] * _rms(moe, gn_ref[...])


def _combine(slot_tiles, yp, wts, x1, gain, gate_tab, seq, tm, tn):
    m, d = x1.shape
    nt = m // tm
    slot_spec = lambda f: pl.BlockSpec((1, 1, TOP_K * tm), f, memory_space=pltpu.SMEM)
    return pl.pallas_call(
        functools.partial(_combine_kernel, tm, tn, seq // tm, nt),
        out_shape=jax.ShapeDtypeStruct((m, d), F32),
        grid=(nt,),
        in_specs=[slot_spec(lambda i: (i, 0, 0)),
                  slot_spec(lambda i: (jnp.minimum(i + 1, nt - 1), 0, 0)),
                  pl.BlockSpec(memory_space=pl.ANY),
                  pl.BlockSpec((tm, TOP_K), lambda i: (i, 0)),
                  pl.BlockSpec((tm, d), lambda i: (i, 0)),
                  pl.BlockSpec((1, d), lambda i: (0, 0)),
                  pl.BlockSpec((MOD_ROWS, d), lambda i: (0, 0))],
        out_specs=pl.BlockSpec((tm, d), lambda i: (i, 0)),
        scratch_shapes=[pltpu.VMEM((2, TOP_K, tm, d // 2), U32), pltpu.SemaphoreType.DMA((2,))],
        compiler_params=_cp("arbitrary"),
        name="moe_combine",
    )(slot_tiles, slot_tiles, yp, wts, x1, gain.reshape(1, d), gate_tab)


def _rope_tables(seq):
    rows = seq // GRID_W
    axis_dim = HEAD_DIM // 2
    row_ids = jnp.repeat(jnp.arange(rows), GRID_W).astype(F32)
    col_ids = jnp.tile(jnp.arange(GRID_W), rows).astype(F32)
    inv_freq = ROPE_THETA ** (-jnp.arange(0, axis_dim, 2, dtype=F32) / axis_dim)
    ang_r = row_ids[:, None] * inv_freq[None, :]
    ang_c = col_ids[:, None] * inv_freq[None, :]
    cos = jnp.concatenate([jnp.cos(ang_r)] * 2 + [jnp.cos(ang_c)] * 2, axis=1)
    sin = jnp.concatenate([-jnp.sin(ang_r), jnp.sin(ang_r), -jnp.sin(ang_c), jnp.sin(ang_c)], axis=1)
    return cos, sin


def _layer(x, xc, c, c_ctx, lower_bounds, w_ada, b_ada, gains, w_in, q_gain, k_gain, hg_gain, w_out,
           w_router, b_router, w_gate, b_gate, w_up, b_up, w_down, b_down):
    batch, seq, d = x.shape
    ctx = xc.shape[1]
    attn_width = d // 2
    hgrn_width = d // 2
    kv_heads = attn_width // HEAD_DIM // GQA_GROUP
    kv_width = kv_heads * HEAD_DIM
    head_cols = attn_width + 2 * hgrn_width
    n_experts = w_router.shape[1]
    m, mc = batch * seq, batch * ctx
    assert batch < MOD_ROWS and seq % ctx == 0

    cond = jnp.concatenate([c, c_ctx[None, :], jnp.zeros((MOD_ROWS - batch - 1, d), F32)], axis=0)
    mod = _ada_modulation(cond, w_ada, b_ada)
    sh1, sc1, g1, sh2, sc2, g2 = [mod[:, i * d:(i + 1) * d] for i in range(N_MOD)]

    x2 = x.reshape(m, d)
    hh = _norm_modulate(x2, xc.reshape(mc, d), gains[0], sh1, sc1, seq, batch)

    p_head = _project(hh, w_in, 0, head_cols, m, BF16)
    p_kv = _project(hh, w_in, head_cols, 2 * kv_width, m + mc, BF16)
    p_f = _project(hh, w_in, head_cols + 2 * kv_width, 2 * hgrn_width, m + mc, F32)
    p_v = _project(hh, w_in, head_cols + 2 * kv_width + 2 * hgrn_width, hgrn_width, m + mc, BF16)

    cos, sin = _rope_tables(seq)
    attn = _attention(p_head, p_kv, q_gain, k_gain, cos, sin, batch, seq, ctx, kv_heads)
    hgrn = _hgrn(p_head, p_f, p_v, lower_bounds, hg_gain, batch, seq, ctx, attn_width, hgrn_width)
    y = _out_projection(attn, hgrn, w_out)

    x1, h2p, logits_t = _epilogue(x2, y, gains[1], gains[2], g1, sh2, sc2, w_router.T, b_router, seq)
    idx, wts, pos, cnt = _route(logits_t)

    row_tile = 256
    n_tiles = -(-(TOP_K * m + n_experts * (row_tile - 1)) // row_tile)
    s_pad = n_tiles * row_tile
    slots, tile_expert, starts, ends = _slots(idx, pos, cnt, row_tile, n_tiles)
    tile_expert = tile_expert[0, :n_tiles]
    n_used = ends[n_experts - 1, 0:1]

    tok_tile = _tile(seq, 128, 8)
    nt = m // tok_tile
    slot_tiles = slots.reshape(TOP_K, nt, tok_tile).transpose(1, 0, 2).reshape(nt, 1, TOP_K * tok_tile)
    xs_p = _dispatch(starts[:, 0], cnt[:, 0], n_used, slot_tiles, h2p, n_tiles, tok_tile, row_tile)
    hmid = _expert_up(tile_expert, n_used, xs_p, w_gate, w_up, b_gate, b_up, row_tile)
    tn = _tile(d, 2048, 256)
    yp = _expert_down(tile_expert, n_used, hmid, w_down, b_down, row_tile, tn)
    out = _combine(slot_tiles, yp, wts.T, x1, gains[3], g2, seq, tok_tile, tn)
    return out.reshape(batch, seq, d)


def kernel(x, c, ctx, c_ctx, w_ada, b_ada, norm_gains, w_in, q_norm_gain, k_norm_gain, hgrn_lb_logits,
           hgrn_norm_gain, w_out, w_router, b_router, w_gate, b_gate, w_up, b_up, w_down, b_down):
    assert w_ada.shape[0] == 1, "single-layer stack only"
    lower_bounds = jnp.cumsum(jax.nn.softmax(hgrn_lb_logits.astype(F32), axis=0), axis=0)
    return _layer(x, ctx, c, c_ctx, lower_bounds[0], w_ada[0], b_ada[0], norm_gains[0], w_in[0],
                  q_norm_gain[0], k_norm_gain[0], hgrn_norm_gain[0], w_out[0], w_router[0], b_router[0],
                  w_gate[0], b_gate[0], w_up[0], b_up[0], w_down[0], b_down[0])
```

```python
import functools

import jax
import jax.numpy as jnp
from jax import lax
from jax.experimental import pallas as pl
from jax.experimental.pallas import tpu as pltpu

F32 = jnp.float32
BF16 = jnp.bfloat16
U32 = jnp.uint32
I32 = jnp.int32

GRID_W = 64
N_MOD = 6
EPS = 1e-6
HEAD_DIM = 128
GQA_GROUP = 4
ROPE_THETA = 10000.0
HGRN_EXPAND = 128
CHUNK = 64
TOP_K = 4
SWIGLU_LIMIT = 7.0
SWIGLU_ALPHA = 1.702
MOD_ROWS = 8

V7X_VMEM_LIMIT = 56 * 1024 * 1024


def _cp(*sem):
    return pltpu.CompilerParams(dimension_semantics=sem, vmem_limit_bytes=V7X_VMEM_LIMIT)


def _tile(n, pref, mult):
    if n <= pref:
        return n
    t = (pref // mult) * mult
    while t > mult and n % t:
        t -= mult
    assert n % t == 0, (n, pref, mult)
    return t


def _dot(a, b):
    return jnp.dot(a, b, preferred_element_type=F32)


def _dot_nt(a, b):
    return lax.dot_general(a, b, (((1,), (1,)), ((), ())), preferred_element_type=F32)


def _dot_tn(a, b):
    return lax.dot_general(a, b, (((0,), (0,)), ((), ())), preferred_element_type=F32)


def _silu(v):
    return v * jax.nn.sigmoid(v)


def _rms(v, gain):
    return v * lax.rsqrt(jnp.mean(v * v, axis=-1, keepdims=True) + EPS) * gain


def _ada_kernel(c_ref, w_ref, b_ref, o_ref):
    s = _silu(c_ref[...]).astype(BF16)
    o_ref[...] = _dot(s, w_ref[...].astype(BF16)) + b_ref[...]


def _ada_modulation(cond, w_ada, b_ada):
    rows, d = cond.shape
    n = w_ada.shape[1]
    tn = _tile(n, 512, 128)
    return pl.pallas_call(
        _ada_kernel,
        out_shape=jax.ShapeDtypeStruct((rows, n), F32),
        grid=(n // tn,),
        in_specs=[pl.BlockSpec((rows, d), lambda j: (0, 0)),
                  pl.BlockSpec((d, tn), lambda j: (0, j)),
                  pl.BlockSpec((1, tn), lambda j: (0, j))],
        out_specs=pl.BlockSpec((rows, tn), lambda j: (0, j)),
        compiler_params=_cp("arbitrary"),
        name="ada_modulation",
    )(cond, w_ada, b_ada.reshape(1, n))


def _norm_mod_kernel(nx, tiles_per_batch, ctx_row, x_ref, c_ref, g_ref, sh_ref, sc_ref, o_ref):
    i = pl.program_id(0)
    is_x = i < nx
    v = jnp.where(is_x, x_ref[...], c_ref[...])
    r = jnp.where(is_x, i // tiles_per_batch, ctx_row)
    shift = sh_ref[pl.ds(r, 1), :]
    scale = sc_ref[pl.ds(r, 1), :]
    o_ref[...] = (_rms(v, g_ref[...]) * (1.0 + scale) + shift).astype(o_ref.dtype)


def _norm_modulate(x2, c2, gain, shift_tab, scale_tab, seq, ctx_row):
    m, d = x2.shape
    mc = c2.shape[0]
    tm = _tile(_gcd(seq, mc), 256, 16)
    nx, ncx = m // tm, mc // tm
    return pl.pallas_call(
        functools.partial(_norm_mod_kernel, nx, seq // tm, ctx_row),
        out_shape=jax.ShapeDtypeStruct((m + mc, d), BF16),
        grid=(nx + ncx,),
        in_specs=[pl.BlockSpec((tm, d), lambda i: (jnp.minimum(i, nx - 1), 0)),
                  pl.BlockSpec((tm, d), lambda i: (jnp.maximum(i - nx, 0), 0)),
                  pl.BlockSpec((1, d), lambda i: (0, 0)),
                  pl.BlockSpec((MOD_ROWS, d), lambda i: (0, 0)),
                  pl.BlockSpec((MOD_ROWS, d), lambda i: (0, 0))],
        out_specs=pl.BlockSpec((tm, d), lambda i: (i, 0)),
        compiler_params=_cp("arbitrary"),
        name="norm_modulate",
    )(x2, c2, gain.reshape(1, d), shift_tab, scale_tab)


def _gcd(a, b):
    while b:
        a, b = b, a % b
    return a


def _mm_kernel(a_ref, w_ref, o_ref):
    o_ref[...] = _dot(a_ref[...], w_ref[...].astype(BF16)).astype(o_ref.dtype)


def _project(a, w, col0, ncols, m_rows, out_dtype):
    k = a.shape[1]
    tn = _tile(_gcd(ncols, col0) if col0 else ncols, 512, 128)
    tm = _tile(m_rows, 1024, 16)
    jb = col0 // tn
    return pl.pallas_call(
        _mm_kernel,
        out_shape=jax.ShapeDtypeStruct((m_rows, ncols), out_dtype),
        grid=(ncols // tn, m_rows // tm),
        in_specs=[pl.BlockSpec((tm, k), lambda j, i: (i, 0)),
                  pl.BlockSpec((k, tn), lambda j, i: (0, jb + j))],
        out_specs=pl.BlockSpec((tm, tn), lambda j, i: (i, j)),
        compiler_params=_cp("arbitrary", "arbitrary"),
        name="in_projection",
    )(a, w)


def _mm2_kernel(ka, a1_ref, a2_ref, w_ref, o_ref):
    w = w_ref[...].astype(BF16)
    o_ref[...] = _dot(a1_ref[...], w[:ka]) + _dot(a2_ref[...], w[ka:])


def _out_projection(a1, a2, w):
    m, ka = a1.shape
    kb = a2.shape[1]
    n = w.shape[1]
    tn = _tile(n, 512, 128)
    tm = _tile(m, 1024, 16)
    return pl.pallas_call(
        functools.partial(_mm2_kernel, ka),
        out_shape=jax.ShapeDtypeStruct((m, n), F32),
        grid=(n // tn, m // tm),
        in_specs=[pl.BlockSpec((tm, ka), lambda j, i: (i, 0)),
                  pl.BlockSpec((tm, kb), lambda j, i: (i, 0)),
                  pl.BlockSpec((ka + kb, tn), lambda j, i: (0, j))],
        out_specs=pl.BlockSpec((tm, tn), lambda j, i: (i, j)),
        compiler_params=_cp("arbitrary", "arbitrary"),
        name="out_projection",
    )(a1, a2, w)


def _rope(v, cos, sin_signed):
    lane = lax.broadcasted_iota(I32, v.shape, 1)
    partner = jnp.where((lane & 32) == 0, pltpu.roll(v, HEAD_DIM - 32, 1), pltpu.roll(v, 32, 1))
    return v * cos + partner * sin_signed


def _attn_kernel(tq, seq, ctx, q_ref, kl_ref, kc_ref, vl_ref, vc_ref, qg_ref, kg_ref, cos_ref, sin_ref,
                 o_ref, k_sc, v_sc):
    qi = pl.program_id(2)

    @pl.when(qi == 0)
    def _():
        kg = kg_ref[...]
        k_sc[0:ctx, :] = _rms(kc_ref[...].astype(F32), kg).astype(BF16)
        kl = _rope(_rms(kl_ref[...].astype(F32), kg), cos_ref[...], sin_ref[...])
        k_sc[ctx:ctx + seq, :] = kl.astype(BF16)
        v_sc[0:ctx, :] = vc_ref[...]
        v_sc[ctx:ctx + seq, :] = vl_ref[...]

    r0 = pl.multiple_of(qi * tq, tq)
    cos = cos_ref[pl.ds(r0, tq), :]
    sin = sin_ref[pl.ds(r0, tq), :]
    qg = qg_ref[...] * (HEAD_DIM ** -0.5)
    keys = k_sc[...]
    vals = v_sc[...]
    for h in range(GQA_GROUP):
        sl = slice(h * HEAD_DIM, (h + 1) * HEAD_DIM)
        q = _rope(_rms(q_ref[:, sl].astype(F32), qg), cos, sin).astype(BF16)
        s = _dot_nt(q, keys)
        p = jnp.exp(s - jnp.max(s, axis=-1, keepdims=True))
        denom = jnp.sum(p, axis=-1, keepdims=True)
        o = _dot(p.astype(BF16), vals) / denom
        o_ref[:, sl] = o.astype(o_ref.dtype)


def _attention(p_head, p_kv, q_gain, k_gain, cos, sin_signed, batch, seq, ctx, kv_heads):
    tq = _tile(seq, 256, 16)
    nq = seq // tq
    gw = GQA_GROUP * HEAD_DIM
    ctx_blk0 = batch * seq // ctx
    return pl.pallas_call(
        functools.partial(_attn_kernel, tq, seq, ctx),
        out_shape=jax.ShapeDtypeStruct((batch * seq, kv_heads * gw), BF16),
        grid=(batch, kv_heads, nq),
        in_specs=[pl.BlockSpec((tq, gw), lambda b, h, q: (b * nq + q, h)),
                  pl.BlockSpec((seq, HEAD_DIM), lambda b, h, q: (b, h)),
                  pl.BlockSpec((ctx, HEAD_DIM), lambda b, h, q: (ctx_blk0 + b, h)),
                  pl.BlockSpec((seq, HEAD_DIM), lambda b, h, q: (b, kv_heads + h)),
                  pl.BlockSpec((ctx, HEAD_DIM), lambda b, h, q: (ctx_blk0 + b, kv_heads + h)),
                  pl.BlockSpec((1, HEAD_DIM), lambda b, h, q: (0, 0)),
                  pl.BlockSpec((1, HEAD_DIM), lambda b, h, q: (0, 0)),
                  pl.BlockSpec((seq, HEAD_DIM), lambda b, h, q: (0, 0)),
                  pl.BlockSpec((seq, HEAD_DIM), lambda b, h, q: (0, 0))],
        out_specs=pl.BlockSpec((tq, gw), lambda b, h, q: (b * nq + q, h)),
        scratch_shapes=[pltpu.VMEM((ctx + seq, HEAD_DIM), BF16),
                        pltpu.VMEM((ctx + seq, HEAD_DIM), BF16)],
        compiler_params=_cp("arbitrary", "arbitrary", "arbitrary"),
        name="gqa_attention",
    )(p_head, p_kv, p_kv, p_kv, p_kv, q_gain.reshape(1, HEAD_DIM), k_gain.reshape(1, HEAD_DIM), cos, sin_signed)


def _hgrn_kernel(hpb, seq, ctx, rows, lb_ref, gain_ref, q_ref, g_ref, ff_ref, fb_ref, v_ref,
                 cff_ref, cfb_ref, cv_ref, o_ref, st_ref, of_ref, ob_ref):
    cpb = rows // CHUNK
    lg_chunk = CHUNK.bit_length() - 1
    row = lax.broadcasted_iota(I32, (rows, rows), 0)
    col = lax.broadcasted_iota(I32, (rows, rows), 1)
    same = (row >> lg_chunk) == (col >> lg_chunk)
    masks = (same & (col <= row), same & (col >= row))
    tris = tuple(jnp.where(m, 1.0, 0.0).astype(BF16) for m in masks)
    row_chunk = lax.broadcasted_iota(I32, (rows, HGRN_EXPAND), 0) >> lg_chunk

    st_ref[...] = jnp.zeros(st_ref.shape, F32)

    def lane_stack(a):
        return jnp.concatenate([jnp.where(row_chunk == c, a, 0.0).astype(BF16) for c in range(cpb)], axis=1)

    def block(state_idx, raw, v, q, lb, rev):
        f = lb + (1.0 - lb) * jax.nn.sigmoid(raw)
        k = 1.0 - f
        logf = jnp.log(f)
        hi = logf.astype(BF16)
        lo = (logf - hi.astype(F32)).astype(BF16)
        b = _dot(tris[rev], hi) + _dot(tris[rev], lo)
        last, mid = (0, CHUNK // 2) if rev else (CHUNK - 1, CHUNK // 2 - 1)

        def chunk_rows(off):
            return jnp.concatenate([jnp.broadcast_to(b[c * CHUNK + off:c * CHUNK + off + 1, :], (CHUNK, HGRN_EXPAND))
                                    for c in range(cpb)], axis=0)

        b_last = chunk_rows(last)
        ut = _dot_tn(v, lane_stack(k * jnp.exp(b_last - b)))
        st = st_ref[state_idx]
        starts = [None] * cpb
        for c in (reversed(range(cpb)) if rev else range(cpb)):
            starts[c] = st.astype(BF16)
            st = (st * jnp.exp(b[c * CHUNK + last:c * CHUNK + last + 1, :])
                  + ut[:, c * HGRN_EXPAND:(c + 1) * HGRN_EXPAND])
        st_ref[state_idx] = st
        if q is None:
            return None
        b_mid = chunk_rows(mid)
        qt = (q * jnp.exp(b - b_mid)).astype(BF16)
        kt = (k * jnp.exp(b_mid - b)).astype(BF16)
        a = jnp.where(masks[rev], _dot_nt(qt, kt), 0.0)
        inter = _dot_nt(lane_stack(q * jnp.exp(b)), jnp.concatenate(starts, axis=1))
        return _dot(a.astype(BF16), v) + inter

    def ctx_body(i, carry):
        r_f = pl.multiple_of(i * rows, rows)
        r_b = pl.multiple_of((ctx // rows - 1 - i) * rows, rows)
        for g in range(hpb):
            sl = slice(g * HGRN_EXPAND, (g + 1) * HGRN_EXPAND)
            block(2 * g, cff_ref[pl.ds(r_f, rows), sl], cv_ref[pl.ds(r_f, rows), sl], None, lb_ref[0:1, sl], 0)
            block(2 * g + 1, cfb_ref[pl.ds(r_b, rows), sl], cv_ref[pl.ds(r_b, rows), sl], None, lb_ref[1:2, sl], 1)
        return carry

    lax.fori_loop(0, ctx // rows, ctx_body, 0)

    def seq_body(i, carry):
        r_f = pl.multiple_of(i * rows, rows)
        r_b = pl.multiple_of((seq // rows - 1 - i) * rows, rows)
        for g in range(hpb):
            sl = slice(g * HGRN_EXPAND, (g + 1) * HGRN_EXPAND)
            for rev, r, f_ref, out_ref in ((0, r_f, ff_ref, of_ref), (1, r_b, fb_ref, ob_ref)):
                q = _silu(q_ref[pl.ds(r, rows), sl].astype(F32))
                out_ref[pl.ds(r, rows), sl] = block(2 * g + rev, f_ref[pl.ds(r, rows), sl],
                                                    v_ref[pl.ds(r, rows), sl], q, lb_ref[rev:rev + 1, sl], rev)
        return carry

    lax.fori_loop(0, seq // rows, seq_body, 0)

    rb = _tile(seq, 256, 8)

    def read_body(i, carry):
        r = pl.multiple_of(i * rb, rb)
        for g in range(hpb):
            sl = slice(g * HGRN_EXPAND, (g + 1) * HGRN_EXPAND)
            o = of_ref[pl.ds(r, rb), sl] + ob_ref[pl.ds(r, rb), sl]
            y = _rms(o, gain_ref[...]) * _silu(g_ref[pl.ds(r, rb), sl].astype(F32))
            o_ref[pl.ds(r, rb), sl] = y.astype(o_ref.dtype)
        return carry

    lax.fori_loop(0, seq // rb, read_body, 0)


def _hgrn(p_head, p_f, p_v, lower_bounds, hg_gain, batch, seq, ctx, attn_width, hgrn_width):
    heads = hgrn_width // HGRN_EXPAND
    hpb = 2 if heads % 2 == 0 else 1
    bw = hpb * HGRN_EXPAND
    nh = heads // hpb
    q0, g0 = attn_width // bw, (attn_width + hgrn_width) // bw
    ctx_blk0 = batch * seq // ctx
    rows = _tile(_gcd(seq, ctx), 256, CHUNK)
    lat = lambda c0: pl.BlockSpec((seq, bw), lambda b, h: (b, c0 + h))
    cx = lambda c0: pl.BlockSpec((ctx, bw), lambda b, h: (ctx_blk0 + b, c0 + h))
    return pl.pallas_call(
        functools.partial(_hgrn_kernel, hpb, seq, ctx, rows),
        out_shape=jax.ShapeDtypeStruct((batch * seq, hgrn_width), BF16),
        grid=(batch, nh),
        in_specs=[pl.BlockSpec((2, bw), lambda b, h: (0, h)),
                  pl.BlockSpec((1, HGRN_EXPAND), lambda b, h: (0, 0)),
                  lat(q0), lat(g0), lat(0), lat(nh), lat(0), cx(0), cx(nh), cx(0)],
        out_specs=pl.BlockSpec((seq, bw), lambda b, h: (b, h)),
        scratch_shapes=[pltpu.VMEM((2 * hpb, HGRN_EXPAND, HGRN_EXPAND), F32),
                        pltpu.VMEM((seq, bw), F32),
                        pltpu.VMEM((seq, bw), F32)],
        compiler_params=_cp("arbitrary", "arbitrary"),
        name="hgrn2_scan",
    )(lower_bounds, hg_gain.reshape(1, HGRN_EXPAND), p_head, p_head, p_f, p_f, p_v, p_f, p_f, p_v)


def _epilogue_kernel(tiles_per_batch, x_ref, y_ref, g1n_ref, g2n_ref, gate_ref, sh_ref, sc_ref, wr_ref, br_ref,
                     x1_ref, hp_ref, lg_ref):
    r = pl.program_id(0) // tiles_per_batch
    x1 = x_ref[...] + gate_ref[pl.ds(r, 1), :] * _rms(y_ref[...], g1n_ref[...])
    x1_ref[...] = x1
    h = _rms(x1, g2n_ref[...]) * (1.0 + sc_ref[pl.ds(r, 1), :]) + sh_ref[pl.ds(r, 1), :]
    hp_ref[...] = h
    w = wr_ref[...]
    w_hi = w.astype(BF16)
    w_lo = (w - w_hi.astype(F32)).astype(BF16)
    h_hi = h.astype(BF16)
    h_lo = (h - h_hi.astype(F32)).astype(BF16)
    lg_ref[...] = _dot_nt(w_hi, h_hi) + _dot_nt(w_hi, h_lo) + _dot_nt(w_lo, h_hi) + br_ref[...]


def _epilogue(x2, y, gain1, gain2, gate_tab, shift_tab, scale_tab, w_router_t, b_router, seq):
    m, d = x2.shape
    ne = w_router_t.shape[0]
    tm = _tile(seq, 256, 128)
    row = lambda: pl.BlockSpec((tm, d), lambda i: (i, 0))
    tab = lambda: pl.BlockSpec((MOD_ROWS, d), lambda i: (0, 0))
    vec = lambda: pl.BlockSpec((1, d), lambda i: (0, 0))
    return pl.pallas_call(
        functools.partial(_epilogue_kernel, seq // tm),
        out_shape=(jax.ShapeDtypeStruct((m, d), F32),
                   jax.ShapeDtypeStruct((m, d), F32),
                   jax.ShapeDtypeStruct((ne, m), F32)),
        grid=(m // tm,),
        in_specs=[row(), row(), vec(), vec(), tab(), tab(), tab(),
                  pl.BlockSpec((ne, d), lambda i: (0, 0)),
                  pl.BlockSpec((ne, 1), lambda i: (0, 0))],
        out_specs=(row(), row(), pl.BlockSpec((ne, tm), lambda i: (0, i))),
        compiler_params=_cp("arbitrary"),
        name="residual_norm_router",
    )(x2, y, gain1.reshape(1, d), gain2.reshape(1, d), gate_tab, shift_tab, scale_tab,
      w_router_t, b_router.reshape(ne, 1))


def _route_kernel(lg_ref, idx_ref, wt_ref, pos_ref, cnt_ref, carry_ref):
    i = pl.program_id(0)
    ne, tn = lg_ref.shape

    @pl.when(i == 0)
    def _():
        carry_ref[...] = jnp.zeros(carry_ref.shape, F32)

    lg = lg_ref[...]
    eidx = lax.broadcasted_iota(I32, (ne, tn), 0)
    vals, hots = [], []
    for k in range(TOP_K):
        m = jnp.max(lg, axis=0, keepdims=True)
        sel = jnp.min(jnp.where(lg == m, eidx, ne), axis=0, keepdims=True)
        hot = eidx == sel
        idx_ref[k:k + 1, :] = sel
        vals.append(m)
        hots.append(hot)
        lg = jnp.where(hot, -jnp.inf, lg)
    exps = [jnp.exp(v - vals[0]) for v in vals]
    denom = exps[0] + exps[1] + exps[2] + exps[3]
    for k in range(TOP_K):
        wt_ref[k:k + 1, :] = exps[k] / denom

    member = hots[0] | hots[1] | hots[2] | hots[3]
    t_row = lax.broadcasted_iota(I32, (tn, tn), 0)
    t_col = lax.broadcasted_iota(I32, (tn, tn), 1)
    before = (t_row < t_col).astype(BF16)
    rank = _dot(member.astype(BF16), before) + carry_ref[:, 0:1]
    for k in range(TOP_K):
        pos_ref[k:k + 1, :] = jnp.sum(jnp.where(hots[k], rank, 0.0), axis=0, keepdims=True).astype(I32)
    carry_ref[...] = carry_ref[...] + jnp.sum(member.astype(F32), axis=1, keepdims=True)
    cnt_ref[...] = carry_ref[...].astype(I32)


def _route(logits_t):
    ne, m = logits_t.shape
    tn = _tile(m, 512, 128)
    out4 = lambda: pl.BlockSpec((TOP_K, tn), lambda i: (0, i))
    return pl.pallas_call(
        _route_kernel,
        out_shape=(jax.ShapeDtypeStruct((TOP_K, m), I32),
                   jax.ShapeDtypeStruct((TOP_K, m), F32),
                   jax.ShapeDtypeStruct((TOP_K, m), I32),
                   jax.ShapeDtypeStruct((ne, 128), I32)),
        grid=(m // tn,),
        in_specs=[pl.BlockSpec((ne, tn), lambda i: (0, i))],
        out_specs=(out4(), out4(), out4(), pl.BlockSpec((ne, 128), lambda i: (0, 0))),
        scratch_shapes=[pltpu.VMEM((ne, 128), F32)],
        compiler_params=_cp("arbitrary"),
        name="top4_route",
    )(logits_t)


def _slots_kernel(row_tile, lg_tile, idx_ref, pos_ref, cnt_ref, slot_ref, te_ref, start_ref, ends_ref):
    ne = cnt_ref.shape[0]
    tn = idx_ref.shape[1]
    tiles = (cnt_ref[...] + (row_tile - 1)) >> lg_tile
    e_row = lax.broadcasted_iota(I32, (ne, ne), 0)
    e_col = lax.broadcasted_iota(I32, (ne, ne), 1)
    upto = jnp.where(e_col <= e_row, 1.0, 0.0).astype(BF16)
    ends = _dot(upto, tiles.astype(F32).astype(BF16)).astype(I32)
    start = (ends - tiles) << lg_tile
    start_ref[...] = start
    ends_ref[...] = ends
    eidx = lax.broadcasted_iota(I32, (ne, tn), 0)
    start_col = start[:, 0:1]
    for k in range(TOP_K):
        hit = eidx == idx_ref[k:k + 1, :]
        slot_ref[k:k + 1, :] = jnp.sum(jnp.where(hit, start_col, 0), axis=0, keepdims=True) + pos_ref[k:k + 1, :]
    tile_i = lax.broadcasted_iota(I32, (ne, te_ref.shape[1]), 1)
    te = jnp.sum(jnp.where(tile_i >= ends[:, 0:1], 1, 0), axis=0, keepdims=True)
    te_ref[...] = jnp.minimum(te, ne - 1)


def _slots(idx, pos, cnt, row_tile, n_tiles):
    ne = cnt.shape[0]
    m = idx.shape[1]
    tn = _tile(m, 2048, 128)
    ntp = -(-n_tiles // 128) * 128
    lg_tile = row_tile.bit_length() - 1
    assert 1 << lg_tile == row_tile and n_tiles <= 256
    io4 = lambda: pl.BlockSpec((TOP_K, tn), lambda i: (0, i))
    small = lambda w: pl.BlockSpec((ne, w), lambda i: (0, 0))
    return pl.pallas_call(
        functools.partial(_slots_kernel, row_tile, lg_tile),
        out_shape=(jax.ShapeDtypeStruct((TOP_K, m), I32),
                   jax.ShapeDtypeStruct((1, ntp), I32),
                   jax.ShapeDtypeStruct((ne, 128), I32),
                   jax.ShapeDtypeStruct((ne, 128), I32)),
        grid=(m // tn,),
        in_specs=[io4(), io4(), small(128)],
        out_specs=(io4(), pl.BlockSpec((1, ntp), lambda i: (0, 0)), small(128), small(128)),
        compiler_params=_cp("arbitrary"),
        name="route_slots",
    )(idx, pos, cnt)


DMA_ISSUE_UNROLL = 8


def _inverse_kernel(tm, n_slots, slot_ref, tok_ref):
    i = pl.program_id(0)

    @pl.when(i == 0)
    def _():
        def clear(g, c):
            for u in range(DMA_ISSUE_UNROLL):
                tok_ref[g * DMA_ISSUE_UNROLL + u] = 0
            return c

        lax.fori_loop(0, n_slots // DMA_ISSUE_UNROLL, clear, 0)

    def body(g, c):
        for u in range(DMA_ISSUE_UNROLL):
            r = g * DMA_ISSUE_UNROLL + u
            for k in range(TOP_K):
                tok_ref[slot_ref[0, 0, k * tm + r]] = i * tm + r
        return c

    lax.fori_loop(0, tm // DMA_ISSUE_UNROLL, body, 0)


def _inverse_slots(slot_tiles, n_slots, tm):
    nt = slot_tiles.shape[0]
    return pl.pallas_call(
        functools.partial(_inverse_kernel, tm, n_slots),
        out_shape=jax.ShapeDtypeStruct((n_slots,), I32),
        grid=(nt,),
        in_specs=[pl.BlockSpec((1, 1, TOP_K * tm), lambda i: (i, 0, 0), memory_space=pltpu.SMEM)],
        out_specs=pl.BlockSpec(memory_space=pltpu.SMEM),
        compiler_params=_cp("arbitrary"),
        name="route_inverse",
    )(slot_tiles)


def _dispatch_kernel(tm, nu_ref, tok_ref, next_ref, src_hbm, o_ref, buf, sem):
    i = pl.program_id(0)
    n_used = nu_ref[0]

    def issue(t_ref, b):
        def body(g, c):
            for u in range(DMA_ISSUE_UNROLL):
                r = g * DMA_ISSUE_UNROLL + u
                pltpu.make_async_copy(src_hbm.at[pl.ds(t_ref[0, 0, r], 1)], buf.at[b, pl.ds(r, 1)],
                                      sem.at[b]).start()
            return c

        lax.fori_loop(0, tm // DMA_ISSUE_UNROLL, body, 0)

    @pl.when(i == 0)
    def _():
        issue(tok_ref, 0)

    @pl.when(i + 1 < n_used)
    def _():
        issue(next_ref, (i + 1) % 2)

    @pl.when(i < n_used)
    def _():
        b = i % 2
        pltpu.make_async_copy(src_hbm.at[pl.ds(0, tm)], buf.at[b], sem.at[b]).wait()
        o_ref[...] = buf[b].astype(o_ref.dtype)

    @pl.when(i >= n_used)
    def _():
        o_ref[...] = jnp.zeros(o_ref.shape, o_ref.dtype)


def _dispatch(n_used, tok_of_slot, src, tm):
    w = src.shape[1]
    nt = tok_of_slot.shape[0] // tm
    tok_tiles = tok_of_slot.reshape(nt, 1, tm)
    tok_spec = lambda f: pl.BlockSpec((1, 1, tm), f, memory_space=pltpu.SMEM)
    return pl.pallas_call(
        functools.partial(_dispatch_kernel, tm),
        out_shape=jax.ShapeDtypeStruct((nt * tm, w), BF16),
        grid_spec=pltpu.PrefetchScalarGridSpec(
            num_scalar_prefetch=1,
            grid=(nt,),
            in_specs=[tok_spec(lambda i, nu: (i, 0, 0)),
                      tok_spec(lambda i, nu: (jnp.minimum(i + 1, nt - 1), 0, 0)),
                      pl.BlockSpec(memory_space=pl.ANY)],
            out_specs=pl.BlockSpec((tm, w), lambda i, nu: (i, 0)),
            scratch_shapes=[pltpu.VMEM((2, tm, w), src.dtype), pltpu.SemaphoreType.DMA((2,))]),
        compiler_params=_cp("arbitrary"),
        name="moe_dispatch",
    )(n_used, tok_tiles, tok_tiles, src)


def _expert_up_kernel(te_ref, nu_ref, x_ref, wg_ref, wu_ref, bg_ref, bu_ref, h_ref):
    @pl.when(pl.program_id(1) < nu_ref[0])
    def _():
        x = x_ref[...]

        def proj(w_ref, b_ref):
            return _dot(x, w_ref[0].astype(BF16)) + b_ref[0]

        g = jnp.minimum(proj(wg_ref, bg_ref), SWIGLU_LIMIT)
        u = jnp.clip(proj(wu_ref, bu_ref), -SWIGLU_LIMIT, SWIGLU_LIMIT)
        h_ref[...] = (g * jax.nn.sigmoid(SWIGLU_ALPHA * g) * (u + 1.0)).astype(h_ref.dtype)

    @pl.when(pl.program_id(1) >= nu_ref[0])
    def _():
        h_ref[...] = jnp.zeros(h_ref.shape, h_ref.dtype)


def _used_tile(i, nu):
    return jnp.minimum(i, nu[0] - 1)


def _expert_up(tile_expert, n_used, xs, w_gate, w_up, b_gate, b_up, tm):
    s_pad = xs.shape[0]
    ne, d, f = w_gate.shape
    fc = _tile(f, 512, 128)
    nt = s_pad // tm
    wspec = lambda: pl.BlockSpec((1, d, fc), lambda c, i, te, nu: (te[i], 0, c))
    bspec = lambda: pl.BlockSpec((1, 1, fc), lambda c, i, te, nu: (te[i], 0, c))
    return pl.pallas_call(
        _expert_up_kernel,
        out_shape=jax.ShapeDtypeStruct((s_pad, f), BF16),
        grid_spec=pltpu.PrefetchScalarGridSpec(
            num_scalar_prefetch=2,
            grid=(f // fc, nt),
            in_specs=[pl.BlockSpec((tm, d), lambda c, i, te, nu: (_used_tile(i, nu), 0)),
                      wspec(), wspec(), bspec(), bspec()],
            out_specs=pl.BlockSpec((tm, fc), lambda c, i, te, nu: (i, c))),
        compiler_params=_cp("arbitrary", "arbitrary"),
        name="expert_gate_up",
    )(tile_expert, n_used, xs, w_gate, w_up, b_gate.reshape(ne, 1, f), b_up.reshape(ne, 1, f))


def _expert_down_kernel(te_ref, nu_ref, h_ref, wd_ref, bd_ref, y_ref):
    @pl.when(pl.program_id(1) < nu_ref[0])
    def _():
        y_ref[...] = _dot(h_ref[...], wd_ref[0].astype(BF16)) + bd_ref[0]

    @pl.when(pl.program_id(1) >= nu_ref[0])
    def _():
        y_ref[...] = jnp.zeros(y_ref.shape, y_ref.dtype)


def _expert_down(tile_expert, n_used, h, w_down, b_down, tm, tn):
    s_pad, f = h.shape
    ne, _, d = w_down.shape
    nt = s_pad // tm
    return pl.pallas_call(
        _expert_down_kernel,
        out_shape=jax.ShapeDtypeStruct((s_pad, d), F32),
        grid_spec=pltpu.PrefetchScalarGridSpec(
            num_scalar_prefetch=2,
            grid=(d // tn, nt),
            in_specs=[pl.BlockSpec((tm, f), lambda c, i, te, nu: (_used_tile(i, nu), 0)),
                      pl.BlockSpec((1, f, tn), lambda c, i, te, nu: (te[i], 0, c)),
                      pl.BlockSpec((1, 1, tn), lambda c, i, te, nu: (te[i], 0, c))],
            out_specs=pl.BlockSpec((tm, tn), lambda c, i, te, nu: (i, c))),
        compiler_params=_cp("arbitrary", "arbitrary"),
        name="expert_down",
    )(tile_expert, n_used, h, w_down, b_down.reshape(ne, 1, d))


def _combine_kernel(tm, tiles_per_batch, nt, slot_ref, next_ref, yp_hbm, wt_ref, x1_ref, gn_ref, gate_ref,
                    o_ref, buf, sem):
    i = pl.program_id(0)

    def issue(s_ref, b):
        def body(g, c):
            for u in range(DMA_ISSUE_UNROLL):
                r = g * DMA_ISSUE_UNROLL + u
                for k in range(TOP_K):
                    pltpu.make_async_copy(yp_hbm.at[pl.ds(s_ref[0, 0, k * tm + r], 1)],
                                          buf.at[b, k, pl.ds(r, 1)], sem.at[b]).start()
            return c

        lax.fori_loop(0, tm // DMA_ISSUE_UNROLL, body, 0)

    @pl.when(i == 0)
    def _():
        issue(slot_ref, 0)

    @pl.when(i + 1 < nt)
    def _():
        issue(next_ref, (i + 1) % 2)

    b = i % 2
    for k in range(TOP_K):
        pltpu.make_async_copy(yp_hbm.at[pl.ds(0, tm)], buf.at[b, k], sem.at[b]).wait()

    wt = wt_ref[...]
    moe = wt[:, 0:1] * buf[b, 0]
    for k in range(1, TOP_K):
        moe = moe + wt[:, k:k + 1] * buf[b, k]
    r = i // tiles_per_batch
    o_ref[...] = x1_ref[...] + gate_ref[pl.ds(r, 1), :] * _rms(moe, gn_ref[...])


def _combine(slot_tiles, yp, wts, x1, gain, gate_tab, seq, tm):
    m, d = x1.shape
    nt = m // tm
    slot_spec = lambda f: pl.BlockSpec((1, 1, TOP_K * tm), f, memory_space=pltpu.SMEM)
    return pl.pallas_call(
        functools.partial(_combine_kernel, tm, seq // tm, nt),
        out_shape=jax.ShapeDtypeStruct((m, d), F32),
        grid=(nt,),
        in_specs=[slot_spec(lambda i: (i, 0, 0)),
                  slot_spec(lambda i: (jnp.minimum(i + 1, nt - 1), 0, 0)),
                  pl.BlockSpec(memory_space=pl.ANY),
                  pl.BlockSpec((tm, TOP_K), lambda i: (i, 0)),
                  pl.BlockSpec((tm, d), lambda i: (i, 0)),
                  pl.BlockSpec((1, d), lambda i: (0, 0)),
                  pl.BlockSpec((MOD_ROWS, d), lambda i: (0, 0))],
        out_specs=pl.BlockSpec((tm, d), lambda i: (i, 0)),
        scratch_shapes=[pltpu.VMEM((2, TOP_K, tm, d), F32), pltpu.SemaphoreType.DMA((2,))],
        compiler_params=_cp("arbitrary"),
        name="moe_combine",
    )(slot_tiles, slot_tiles, yp, wts, x1, gain.reshape(1, d), gate_tab)


def _rope_tables(seq):
    rows = seq // GRID_W
    axis_dim = HEAD_DIM // 2
    row_ids = jnp.repeat(jnp.arange(rows), GRID_W).astype(F32)
    col_ids = jnp.tile(jnp.arange(GRID_W), rows).astype(F32)
    inv_freq = ROPE_THETA ** (-jnp.arange(0, axis_dim, 2, dtype=F32) / axis_dim)
    ang_r = row_ids[:, None] * inv_freq[None, :]
    ang_c = col_ids[:, None] * inv_freq[None, :]
    cos = jnp.concatenate([jnp.cos(ang_r)] * 2 + [jnp.cos(ang_c)] * 2, axis=1)
    sin = jnp.concatenate([-jnp.sin(ang_r), jnp.sin(ang_r), -jnp.sin(ang_c), jnp.sin(ang_c)], axis=1)
    return cos, sin


def _layer(x, xc, c, c_ctx, lower_bounds, w_ada, b_ada, gains, w_in, q_gain, k_gain, hg_gain, w_out,
           w_router, b_router, w_gate, b_gate, w_up, b_up, w_down, b_down):
    batch, seq, d = x.shape
    ctx = xc.shape[1]
    attn_width = d // 2
    hgrn_width = d // 2
    kv_heads = attn_width // HEAD_DIM // GQA_GROUP
    kv_width = kv_heads * HEAD_DIM
    head_cols = attn_width + 2 * hgrn_width
    n_experts = w_router.shape[1]
    m, mc = batch * seq, batch * ctx
    assert batch < MOD_ROWS and seq % ctx == 0

    cond = jnp.concatenate([c, c_ctx[None, :], jnp.zeros((MOD_ROWS - batch - 1, d), F32)], axis=0)
    mod = _ada_modulation(cond, w_ada, b_ada)
    sh1, sc1, g1, sh2, sc2, g2 = [mod[:, i * d:(i + 1) * d] for i in range(N_MOD)]

    x2 = x.reshape(m, d)
    hh = _norm_modulate(x2, xc.reshape(mc, d), gains[0], sh1, sc1, seq, batch)

    p_head = _project(hh, w_in, 0, head_cols, m, BF16)
    p_kv = _project(hh, w_in, head_cols, 2 * kv_width, m + mc, BF16)
    p_f = _project(hh, w_in, head_cols + 2 * kv_width, 2 * hgrn_width, m + mc, F32)
    p_v = _project(hh, w_in, head_cols + 2 * kv_width + 2 * hgrn_width, hgrn_width, m + mc, BF16)

    cos, sin = _rope_tables(seq)
    attn = _attention(p_head, p_kv, q_gain, k_gain, cos, sin, batch, seq, ctx, kv_heads)
    hgrn = _hgrn(p_head, p_f, p_v, lower_bounds, hg_gain, batch, seq, ctx, attn_width, hgrn_width)
    y = _out_projection(attn, hgrn, w_out)

    x1, h2, logits_t = _epilogue(x2, y, gains[1], gains[2], g1, sh2, sc2, w_router.T, b_router, seq)
    idx, wts, pos, cnt = _route(logits_t)

    row_tile = 256
    n_tiles = -(-(TOP_K * m + n_experts * (row_tile - 1)) // row_tile)
    slots, tile_expert, _, ends = _slots(idx, pos, cnt, row_tile, n_tiles)
    tile_expert = tile_expert[0, :n_tiles]
    n_used = ends[n_experts - 1, 0:1]

    tok_tile = _tile(seq, 128, 8)
    nt = m // tok_tile
    slot_tiles = slots.reshape(TOP_K, nt, tok_tile).transpose(1, 0, 2).reshape(nt, 1, TOP_K * tok_tile)
    tok_of_slot = _inverse_slots(slot_tiles, n_tiles * row_tile, tok_tile)
    xs = _dispatch(n_used, tok_of_slot, h2, row_tile)
    hmid = _expert_up(tile_expert, n_used, xs, w_gate, w_up, b_gate, b_up, row_tile)
    ye = _expert_down(tile_expert, n_used, hmid, w_down, b_down, row_tile, _tile(d, 2048, 256))
    out = _combine(slot_tiles, ye, wts.T, x1, gains[3], g2, seq, tok_tile)
    return out.reshape(batch, seq, d)


def kernel(x, c, ctx, c_ctx, w_ada, b_ada, norm_gains, w_in, q_norm_gain, k_norm_gain, hgrn_lb_logits,
           hgrn_norm_gain, w_out, w_router, b_router, w_gate, b_gate, w_up, b_up, w_down, b_down):
    assert w_ada.shape[0] == 1, "single-layer stack only"
    lower_bounds = jnp.cumsum(jax.nn.softmax(hgrn_lb_logits.astype(F32), axis=0), axis=0)
    return _layer(x, ctx, c, c_ctx, lower_bounds[0], w_ada[0], b_ada[0], norm_gains[0], w_in[0],
                  q_norm_gain[0], k_norm_gain[0], hgrn_norm_gain[0], w_out[0], w_router[0], b_router[0],
                  w_gate[0], b_gate[0], w_up[0], b_up[0], w_down[0], b_down[0])
```

```python
import functools

import jax
import jax.numpy as jnp
from jax import lax
from jax.experimental import pallas as pl
from jax.experimental.pallas import tpu as pltpu

F32 = jnp.float32
BF16 = jnp.bfloat16
U32 = jnp.uint32
I32 = jnp.int32

GRID_W = 64
N_MOD = 6
EPS = 1e-6
HEAD_DIM = 128
GQA_GROUP = 4
ROPE_THETA = 10000.0
HGRN_EXPAND = 128
CHUNK = 64
TOP_K = 4
SWIGLU_LIMIT = 7.0
SWIGLU_ALPHA = 1.702
MOD_ROWS = 8

V7X_VMEM_LIMIT = 56 * 1024 * 1024


def _cp(*sem):
    return pltpu.CompilerParams(dimension_semantics=sem, vmem_limit_bytes=V7X_VMEM_LIMIT)


def _tile(n, pref, mult):
    if n <= pref:
        return n
    t = (pref // mult) * mult
    while t > mult and n % t:
        t -= mult
    assert n % t == 0, (n, pref, mult)
    return t


def _dot(a, b):
    return jnp.dot(a, b, preferred_element_type=F32)


def _dot_nt(a, b):
    return lax.dot_general(a, b, (((1,), (1,)), ((), ())), preferred_element_type=F32)


def _dot_tn(a, b):
    return lax.dot_general(a, b, (((0,), (0,)), ((), ())), preferred_element_type=F32)


def _silu(v):
    return v * jax.nn.sigmoid(v)


def _rms(v, gain):
    return v * lax.rsqrt(jnp.mean(v * v, axis=-1, keepdims=True) + EPS) * gain


def _ada_kernel(c_ref, w_ref, b_ref, o_ref):
    s = _silu(c_ref[...]).astype(BF16)
    o_ref[...] = _dot(s, w_ref[...].astype(BF16)) + b_ref[...]


def _ada_modulation(cond, w_ada, b_ada):
    rows, d = cond.shape
    n = w_ada.shape[1]
    tn = _tile(n, 512, 128)
    return pl.pallas_call(
        _ada_kernel,
        out_shape=jax.ShapeDtypeStruct((rows, n), F32),
        grid=(n // tn,),
        in_specs=[pl.BlockSpec((rows, d), lambda j: (0, 0)),
                  pl.BlockSpec((d, tn), lambda j: (0, j)),
                  pl.BlockSpec((1, tn), lambda j: (0, j))],
        out_specs=pl.BlockSpec((rows, tn), lambda j: (0, j)),
        compiler_params=_cp("arbitrary"),
        name="ada_modulation",
    )(cond, w_ada, b_ada.reshape(1, n))


def _norm_mod_kernel(nx, tiles_per_batch, ctx_row, x_ref, c_ref, g_ref, sh_ref, sc_ref, o_ref):
    i = pl.program_id(0)
    is_x = i < nx
    v = jnp.where(is_x, x_ref[...], c_ref[...])
    r = jnp.where(is_x, i // tiles_per_batch, ctx_row)
    shift = sh_ref[pl.ds(r, 1), :]
    scale = sc_ref[pl.ds(r, 1), :]
    o_ref[...] = (_rms(v, g_ref[...]) * (1.0 + scale) + shift).astype(o_ref.dtype)


def _norm_modulate(x2, c2, gain, shift_tab, scale_tab, seq, ctx_row):
    m, d = x2.shape
    mc = c2.shape[0]
    tm = _tile(_gcd(seq, mc), 256, 16)
    nx, ncx = m // tm, mc // tm
    return pl.pallas_call(
        functools.partial(_norm_mod_kernel, nx, seq // tm, ctx_row),
        out_shape=jax.ShapeDtypeStruct((m + mc, d), BF16),
        grid=(nx + ncx,),
        in_specs=[pl.BlockSpec((tm, d), lambda i: (jnp.minimum(i, nx - 1), 0)),
                  pl.BlockSpec((tm, d), lambda i: (jnp.maximum(i - nx, 0), 0)),
                  pl.BlockSpec((1, d), lambda i: (0, 0)),
                  pl.BlockSpec((MOD_ROWS, d), lambda i: (0, 0)),
                  pl.BlockSpec((MOD_ROWS, d), lambda i: (0, 0))],
        out_specs=pl.BlockSpec((tm, d), lambda i: (i, 0)),
        compiler_params=_cp("arbitrary"),
        name="norm_modulate",
    )(x2, c2, gain.reshape(1, d), shift_tab, scale_tab)


def _gcd(a, b):
    while b:
        a, b = b, a % b
    return a


def _mm_kernel(a_ref, w_ref, o_ref):
    o_ref[...] = _dot(a_ref[...], w_ref[...].astype(BF16)).astype(o_ref.dtype)


def _project(a, w, col0, ncols, m_rows, out_dtype):
    k = a.shape[1]
    tn = _tile(_gcd(ncols, col0) if col0 else ncols, 512, 128)
    tm = _tile(m_rows, 1024, 16)
    jb = col0 // tn
    return pl.pallas_call(
        _mm_kernel,
        out_shape=jax.ShapeDtypeStruct((m_rows, ncols), out_dtype),
        grid=(ncols // tn, m_rows // tm),
        in_specs=[pl.BlockSpec((tm, k), lambda j, i: (i, 0)),
                  pl.BlockSpec((k, tn), lambda j, i: (0, jb + j))],
        out_specs=pl.BlockSpec((tm, tn), lambda j, i: (i, j)),
        compiler_params=_cp("arbitrary", "arbitrary"),
        name="in_projection",
    )(a, w)


def _mm2_kernel(ka, a1_ref, a2_ref, w_ref, o_ref):
    w = w_ref[...].astype(BF16)
    o_ref[...] = _dot(a1_ref[...], w[:ka]) + _dot(a2_ref[...], w[ka:])


def _out_projection(a1, a2, w):
    m, ka = a1.shape
    kb = a2.shape[1]
    n = w.shape[1]
    tn = _tile(n, 512, 128)
    tm = _tile(m, 1024, 16)
    return pl.pallas_call(
        functools.partial(_mm2_kernel, ka),
        out_shape=jax.ShapeDtypeStruct((m, n), F32),
        grid=(n // tn, m // tm),
        in_specs=[pl.BlockSpec((tm, ka), lambda j, i: (i, 0)),
                  pl.BlockSpec((tm, kb), lambda j, i: (i, 0)),
                  pl.BlockSpec((ka + kb, tn), lambda j, i: (0, j))],
        out_specs=pl.BlockSpec((tm, tn), lambda j, i: (i, j)),
        compiler_params=_cp("arbitrary", "arbitrary"),
        name="out_projection",
    )(a1, a2, w)


def _rope(v, cos, sin_signed):
    lane = lax.broadcasted_iota(I32, v.shape, 1)
    partner = jnp.where((lane & 32) == 0, pltpu.roll(v, HEAD_DIM - 32, 1), pltpu.roll(v, 32, 1))
    return v * cos + partner * sin_signed


def _attn_kernel(tq, seq, ctx, q_ref, kl_ref, kc_ref, vl_ref, vc_ref, qg_ref, kg_ref, cos_ref, sin_ref,
                 o_ref, k_sc, v_sc):
    qi = pl.program_id(2)

    @pl.when(qi == 0)
    def _():
        kg = kg_ref[...]
        k_sc[0:ctx, :] = _rms(kc_ref[...].astype(F32), kg).astype(BF16)
        kl = _rope(_rms(kl_ref[...].astype(F32), kg), cos_ref[...], sin_ref[...])
        k_sc[ctx:ctx + seq, :] = kl.astype(BF16)
        v_sc[0:ctx, :] = vc_ref[...]
        v_sc[ctx:ctx + seq, :] = vl_ref[...]

    r0 = pl.multiple_of(qi * tq, tq)
    cos = cos_ref[pl.ds(r0, tq), :]
    sin = sin_ref[pl.ds(r0, tq), :]
    qg = qg_ref[...] * (HEAD_DIM ** -0.5)
    keys = k_sc[...]
    vals = v_sc[...]
    for h in range(GQA_GROUP):
        sl = slice(h * HEAD_DIM, (h + 1) * HEAD_DIM)
        q = _rope(_rms(q_ref[:, sl].astype(F32), qg), cos, sin).astype(BF16)
        s = _dot_nt(q, keys)
        p = jnp.exp(s - jnp.max(s, axis=-1, keepdims=True))
        denom = jnp.sum(p, axis=-1, keepdims=True)
        o = _dot(p.astype(BF16), vals) / denom
        o_ref[:, sl] = o.astype(o_ref.dtype)


def _attention(p_head, p_kv, q_gain, k_gain, cos, sin_signed, batch, seq, ctx, kv_heads):
    tq = _tile(seq, 256, 16)
    nq = seq // tq
    gw = GQA_GROUP * HEAD_DIM
    ctx_blk0 = batch * seq // ctx
    return pl.pallas_call(
        functools.partial(_attn_kernel, tq, seq, ctx),
        out_shape=jax.ShapeDtypeStruct((batch * seq, kv_heads * gw), BF16),
        grid=(batch, kv_heads, nq),
        in_specs=[pl.BlockSpec((tq, gw), lambda b, h, q: (b * nq + q, h)),
                  pl.BlockSpec((seq, HEAD_DIM), lambda b, h, q: (b, h)),
                  pl.BlockSpec((ctx, HEAD_DIM), lambda b, h, q: (ctx_blk0 + b, h)),
                  pl.BlockSpec((seq, HEAD_DIM), lambda b, h, q: (b, kv_heads + h)),
                  pl.BlockSpec((ctx, HEAD_DIM), lambda b, h, q: (ctx_blk0 + b, kv_heads + h)),
                  pl.BlockSpec((1, HEAD_DIM), lambda b, h, q: (0, 0)),
                  pl.BlockSpec((1, HEAD_DIM), lambda b, h, q: (0, 0)),
                  pl.BlockSpec((seq, HEAD_DIM), lambda b, h, q: (0, 0)),
                  pl.BlockSpec((seq, HEAD_DIM), lambda b, h, q: (0, 0))],
        out_specs=pl.BlockSpec((tq, gw), lambda b, h, q: (b * nq + q, h)),
        scratch_shapes=[pltpu.VMEM((ctx + seq, HEAD_DIM), BF16),
                        pltpu.VMEM((ctx + seq, HEAD_DIM), BF16)],
        compiler_params=_cp("arbitrary", "arbitrary", "arbitrary"),
        name="gqa_attention",
    )(p_head, p_kv, p_kv, p_kv, p_kv, q_gain.reshape(1, HEAD_DIM), k_gain.reshape(1, HEAD_DIM), cos, sin_signed)


def _hgrn_kernel(hpb, seq, ctx, rows, lb_ref, gain_ref, q_ref, g_ref, ff_ref, fb_ref, v_ref,
                 cff_ref, cfb_ref, cv_ref, o_ref, st_ref, of_ref, ob_ref):
    cpb = rows // CHUNK
    lg_chunk = CHUNK.bit_length() - 1
    row = lax.broadcasted_iota(I32, (rows, rows), 0)
    col = lax.broadcasted_iota(I32, (rows, rows), 1)
    same = (row >> lg_chunk) == (col >> lg_chunk)
    masks = (same & (col <= row), same & (col >= row))
    tris = tuple(jnp.where(m, 1.0, 0.0).astype(BF16) for m in masks)
    row_chunk = lax.broadcasted_iota(I32, (rows, HGRN_EXPAND), 0) >> lg_chunk

    st_ref[...] = jnp.zeros(st_ref.shape, F32)

    def lane_stack(a):
        return jnp.concatenate([jnp.where(row_chunk == c, a, 0.0).astype(BF16) for c in range(cpb)], axis=1)

    def block(state_idx, raw, v, q, lb, rev):
        f = lb + (1.0 - lb) * jax.nn.sigmoid(raw)
        k = 1.0 - f
        logf = jnp.log(f)
        hi = logf.astype(BF16)
        lo = (logf - hi.astype(F32)).astype(BF16)
        b = _dot(tris[rev], hi) + _dot(tris[rev], lo)
        last, mid = (0, CHUNK // 2) if rev else (CHUNK - 1, CHUNK // 2 - 1)

        def chunk_rows(off):
            return jnp.concatenate([jnp.broadcast_to(b[c * CHUNK + off:c * CHUNK + off + 1, :], (CHUNK, HGRN_EXPAND))
                                    for c in range(cpb)], axis=0)

        b_last = chunk_rows(last)
        ut = _dot_tn(v, lane_stack(k * jnp.exp(b_last - b)))
        st = st_ref[state_idx]
        starts = [None] * cpb
        for c in (reversed(range(cpb)) if rev else range(cpb)):
            starts[c] = st.astype(BF16)
            st = (st * jnp.exp(b[c * CHUNK + last:c * CHUNK + last + 1, :])
                  + ut[:, c * HGRN_EXPAND:(c + 1) * HGRN_EXPAND])
        st_ref[state_idx] = st
        if q is None:
            return None
        b_mid = chunk_rows(mid)
        qt = (q * jnp.exp(b - b_mid)).astype(BF16)
        kt = (k * jnp.exp(b_mid - b)).astype(BF16)
        a = jnp.where(masks[rev], _dot_nt(qt, kt), 0.0)
        inter = _dot_nt(lane_stack(q * jnp.exp(b)), jnp.concatenate(starts, axis=1))
        return _dot(a.astype(BF16), v) + inter

    def ctx_body(i, carry):
        r_f = pl.multiple_of(i * rows, rows)
        r_b = pl.multiple_of((ctx // rows - 1 - i) * rows, rows)
        for g in range(hpb):
            sl = slice(g * HGRN_EXPAND, (g + 1) * HGRN_EXPAND)
            block(2 * g, cff_ref[pl.ds(r_f, rows), sl], cv_ref[pl.ds(r_f, rows), sl], None, lb_ref[0:1, sl], 0)
            block(2 * g + 1, cfb_ref[pl.ds(r_b, rows), sl], cv_ref[pl.ds(r_b, rows), sl], None, lb_ref[1:2, sl], 1)
        return carry

    lax.fori_loop(0, ctx // rows, ctx_body, 0)

    def seq_body(i, carry):
        r_f = pl.multiple_of(i * rows, rows)
        r_b = pl.multiple_of((seq // rows - 1 - i) * rows, rows)
        for g in range(hpb):
            sl = slice(g * HGRN_EXPAND, (g + 1) * HGRN_EXPAND)
            for rev, r, f_ref, out_ref in ((0, r_f, ff_ref, of_ref), (1, r_b, fb_ref, ob_ref)):
                q = _silu(q_ref[pl.ds(r, rows), sl].astype(F32))
                out_ref[pl.ds(r, rows), sl] = block(2 * g + rev, f_ref[pl.ds(r, rows), sl],
                                                    v_ref[pl.ds(r, rows), sl], q, lb_ref[rev:rev + 1, sl], rev)
        return carry

    lax.fori_loop(0, seq // rows, seq_body, 0)

    rb = _tile(seq, 256, 8)

    def read_body(i, carry):
        r = pl.multiple_of(i * rb, rb)
        for g in range(hpb):
            sl = slice(g * HGRN_EXPAND, (g + 1) * HGRN_EXPAND)
            o = of_ref[pl.ds(r, rb), sl] + ob_ref[pl.ds(r, rb), sl]
            y = _rms(o, gain_ref[...]) * _silu(g_ref[pl.ds(r, rb), sl].astype(F32))
            o_ref[pl.ds(r, rb), sl] = y.astype(o_ref.dtype)
        return carry

    lax.fori_loop(0, seq // rb, read_body, 0)


def _hgrn(p_head, p_f, p_v, lower_bounds, hg_gain, batch, seq, ctx, attn_width, hgrn_width):
    heads = hgrn_width // HGRN_EXPAND
    hpb = 2 if heads % 2 == 0 else 1
    bw = hpb * HGRN_EXPAND
    nh = heads // hpb
    q0, g0 = attn_width // bw, (attn_width + hgrn_width) // bw
    ctx_blk0 = batch * seq // ctx
    rows = _tile(_gcd(seq, ctx), 256, CHUNK)
    lat = lambda c0: pl.BlockSpec((seq, bw), lambda b, h: (b, c0 + h))
    cx = lambda c0: pl.BlockSpec((ctx, bw), lambda b, h: (ctx_blk0 + b, c0 + h))
    return pl.pallas_call(
        functools.partial(_hgrn_kernel, hpb, seq, ctx, rows),
        out_shape=jax.ShapeDtypeStruct((batch * seq, hgrn_width), BF16),
        grid=(batch, nh),
        in_specs=[pl.BlockSpec((2, bw), lambda b, h: (0, h)),
                  pl.BlockSpec((1, HGRN_EXPAND), lambda b, h: (0, 0)),
                  lat(q0), lat(g0), lat(0), lat(nh), lat(0), cx(0), cx(nh), cx(0)],
        out_specs=pl.BlockSpec((seq, bw), lambda b, h: (b, h)),
        scratch_shapes=[pltpu.VMEM((2 * hpb, HGRN_EXPAND, HGRN_EXPAND), F32),
                        pltpu.VMEM((seq, bw), F32),
                        pltpu.VMEM((seq, bw), F32)],
        compiler_params=_cp("arbitrary", "arbitrary"),
        name="hgrn2_scan",
    )(lower_bounds, hg_gain.reshape(1, HGRN_EXPAND), p_head, p_head, p_f, p_f, p_v, p_f, p_f, p_v)


def _epilogue_kernel(tiles_per_batch, x_ref, y_ref, g1n_ref, g2n_ref, gate_ref, sh_ref, sc_ref, wr_ref, br_ref,
                     x1_ref, hp_ref, lg_ref):
    r = pl.program_id(0) // tiles_per_batch
    x1 = x_ref[...] + gate_ref[pl.ds(r, 1), :] * _rms(y_ref[...], g1n_ref[...])
    x1_ref[...] = x1
    h = _rms(x1, g2n_ref[...]) * (1.0 + sc_ref[pl.ds(r, 1), :]) + sh_ref[pl.ds(r, 1), :]
    hp_ref[...] = h
    w = wr_ref[...]
    w_hi = w.astype(BF16)
    w_lo = (w - w_hi.astype(F32)).astype(BF16)
    h_hi = h.astype(BF16)
    h_lo = (h - h_hi.astype(F32)).astype(BF16)
    lg_ref[...] = _dot_nt(w_hi, h_hi) + _dot_nt(w_hi, h_lo) + _dot_nt(w_lo, h_hi) + br_ref[...]


def _epilogue(x2, y, gain1, gain2, gate_tab, shift_tab, scale_tab, w_router_t, b_router, seq):
    m, d = x2.shape
    ne = w_router_t.shape[0]
    tm = _tile(seq, 256, 128)
    row = lambda: pl.BlockSpec((tm, d), lambda i: (i, 0))
    tab = lambda: pl.BlockSpec((MOD_ROWS, d), lambda i: (0, 0))
    vec = lambda: pl.BlockSpec((1, d), lambda i: (0, 0))
    return pl.pallas_call(
        functools.partial(_epilogue_kernel, seq // tm),
        out_shape=(jax.ShapeDtypeStruct((m, d), F32),
                   jax.ShapeDtypeStruct((m, d), F32),
                   jax.ShapeDtypeStruct((ne, m), F32)),
        grid=(m // tm,),
        in_specs=[row(), row(), vec(), vec(), tab(), tab(), tab(),
                  pl.BlockSpec((ne, d), lambda i: (0, 0)),
                  pl.BlockSpec((ne, 1), lambda i: (0, 0))],
        out_specs=(row(), row(), pl.BlockSpec((ne, tm), lambda i: (0, i))),
        compiler_params=_cp("arbitrary"),
        name="residual_norm_router",
    )(x2, y, gain1.reshape(1, d), gain2.reshape(1, d), gate_tab, shift_tab, scale_tab,
      w_router_t, b_router.reshape(ne, 1))


def _route_kernel(lg_ref, idx_ref, wt_ref, pos_ref, cnt_ref, carry_ref):
    i = pl.program_id(0)
    ne, tn = lg_ref.shape

    @pl.when(i == 0)
    def _():
        carry_ref[...] = jnp.zeros(carry_ref.shape, F32)

    lg = lg_ref[...]
    eidx = lax.broadcasted_iota(I32, (ne, tn), 0)
    vals, hots = [], []
    for k in range(TOP_K):
        m = jnp.max(lg, axis=0, keepdims=True)
        sel = jnp.min(jnp.where(lg == m, eidx, ne), axis=0, keepdims=True)
        hot = eidx == sel
        idx_ref[k:k + 1, :] = sel
        vals.append(m)
        hots.append(hot)
        lg = jnp.where(hot, -jnp.inf, lg)
    exps = [jnp.exp(v - vals[0]) for v in vals]
    denom = exps[0] + exps[1] + exps[2] + exps[3]
    for k in range(TOP_K):
        wt_ref[k:k + 1, :] = exps[k] / denom

    member = hots[0] | hots[1] | hots[2] | hots[3]
    t_row = lax.broadcasted_iota(I32, (tn, tn), 0)
    t_col = lax.broadcasted_iota(I32, (tn, tn), 1)
    before = (t_row < t_col).astype(BF16)
    rank = _dot(member.astype(BF16), before) + carry_ref[:, 0:1]
    for k in range(TOP_K):
        pos_ref[k:k + 1, :] = jnp.sum(jnp.where(hots[k], rank, 0.0), axis=0, keepdims=True).astype(I32)
    carry_ref[...] = carry_ref[...] + jnp.sum(member.astype(F32), axis=1, keepdims=True)
    cnt_ref[...] = carry_ref[...].astype(I32)


def _route(logits_t):
    ne, m = logits_t.shape
    tn = _tile(m, 512, 128)
    out4 = lambda: pl.BlockSpec((TOP_K, tn), lambda i: (0, i))
    return pl.pallas_call(
        _route_kernel,
        out_shape=(jax.ShapeDtypeStruct((TOP_K, m), I32),
                   jax.ShapeDtypeStruct((TOP_K, m), F32),
                   jax.ShapeDtypeStruct((TOP_K, m), I32),
                   jax.ShapeDtypeStruct((ne, 128), I32)),
        grid=(m // tn,),
        in_specs=[pl.BlockSpec((ne, tn), lambda i: (0, i))],
        out_specs=(out4(), out4(), out4(), pl.BlockSpec((ne, 128), lambda i: (0, 0))),
        scratch_shapes=[pltpu.VMEM((ne, 128), F32)],
        compiler_params=_cp("arbitrary"),
        name="top4_route",
    )(logits_t)


def _slots_kernel(row_tile, lg_tile, idx_ref, pos_ref, cnt_ref, slot_ref, te_ref, start_ref, ends_ref):
    ne = cnt_ref.shape[0]
    tn = idx_ref.shape[1]
    tiles = (cnt_ref[...] + (row_tile - 1)) >> lg_tile
    e_row = lax.broadcasted_iota(I32, (ne, ne), 0)
    e_col = lax.broadcasted_iota(I32, (ne, ne), 1)
    upto = jnp.where(e_col <= e_row, 1.0, 0.0).astype(BF16)
    ends = _dot(upto, tiles.astype(F32).astype(BF16)).astype(I32)
    start = (ends - tiles) << lg_tile
    start_ref[...] = start
    ends_ref[...] = ends
    eidx = lax.broadcasted_iota(I32, (ne, tn), 0)
    start_col = start[:, 0:1]
    for k in range(TOP_K):
        hit = eidx == idx_ref[k:k + 1, :]
        slot_ref[k:k + 1, :] = jnp.sum(jnp.where(hit, start_col, 0), axis=0, keepdims=True) + pos_ref[k:k + 1, :]
    tile_i = lax.broadcasted_iota(I32, (ne, te_ref.shape[1]), 1)
    te = jnp.sum(jnp.where(tile_i >= ends[:, 0:1], 1, 0), axis=0, keepdims=True)
    te_ref[...] = jnp.minimum(te, ne - 1)


def _slots(idx, pos, cnt, row_tile, n_tiles):
    ne = cnt.shape[0]
    m = idx.shape[1]
    tn = _tile(m, 2048, 128)
    ntp = -(-n_tiles // 128) * 128
    lg_tile = row_tile.bit_length() - 1
    assert 1 << lg_tile == row_tile and n_tiles <= 256
    io4 = lambda: pl.BlockSpec((TOP_K, tn), lambda i: (0, i))
    small = lambda w: pl.BlockSpec((ne, w), lambda i: (0, 0))
    return pl.pallas_call(
        functools.partial(_slots_kernel, row_tile, lg_tile),
        out_shape=(jax.ShapeDtypeStruct((TOP_K, m), I32),
                   jax.ShapeDtypeStruct((1, ntp), I32),
                   jax.ShapeDtypeStruct((ne, 128), I32),
                   jax.ShapeDtypeStruct((ne, 128), I32)),
        grid=(m // tn,),
        in_specs=[io4(), io4(), small(128)],
        out_specs=(io4(), pl.BlockSpec((1, ntp), lambda i: (0, 0)), small(128), small(128)),
        compiler_params=_cp("arbitrary"),
        name="route_slots",
    )(idx, pos, cnt)


DMA_ISSUE_UNROLL = 8


def _inverse_kernel(tm, n_slots, slot_ref, tok_ref):
    i = pl.program_id(0)

    @pl.when(i == 0)
    def _():
        def clear(g, c):
            for u in range(DMA_ISSUE_UNROLL):
                tok_ref[g * DMA_ISSUE_UNROLL + u] = 0
            return c

        lax.fori_loop(0, n_slots // DMA_ISSUE_UNROLL, clear, 0)

    def body(g, c):
        for u in range(DMA_ISSUE_UNROLL):
            r = g * DMA_ISSUE_UNROLL + u
            for k in range(TOP_K):
                tok_ref[slot_ref[0, 0, k * tm + r]] = i * tm + r
        return c

    lax.fori_loop(0, tm // DMA_ISSUE_UNROLL, body, 0)


def _inverse_slots(slot_tiles, n_slots, tm):
    nt = slot_tiles.shape[0]
    return pl.pallas_call(
        functools.partial(_inverse_kernel, tm, n_slots),
        out_shape=jax.ShapeDtypeStruct((n_slots,), I32),
        grid=(nt,),
        in_specs=[pl.BlockSpec((1, 1, TOP_K * tm), lambda i: (i, 0, 0), memory_space=pltpu.SMEM)],
        out_specs=pl.BlockSpec(memory_space=pltpu.SMEM),
        compiler_params=_cp("arbitrary"),
        name="route_inverse",
    )(slot_tiles)


def _dispatch_kernel(tm, nu_ref, tok_ref, next_ref, src_hbm, o_ref, buf, sem):
    i = pl.program_id(0)
    n_used = nu_ref[0]

    def issue(t_ref, b):
        def body(g, c):
            for u in range(DMA_ISSUE_UNROLL):
                r = g * DMA_ISSUE_UNROLL + u
                pltpu.make_async_copy(src_hbm.at[pl.ds(t_ref[0, 0, r], 1)], buf.at[b, pl.ds(r, 1)],
                                      sem.at[b]).start(priority=u % 2)
            return c

        lax.fori_loop(0, tm // DMA_ISSUE_UNROLL, body, 0)

    @pl.when(i == 0)
    def _():
        issue(tok_ref, 0)

    @pl.when(i + 1 < n_used)
    def _():
        issue(next_ref, (i + 1) % 2)

    @pl.when(i < n_used)
    def _():
        b = i % 2
        pltpu.make_async_copy(src_hbm.at[pl.ds(0, tm)], buf.at[b], sem.at[b]).wait()
        o_ref[...] = buf[b].astype(o_ref.dtype)

    @pl.when(i >= n_used)
    def _():
        o_ref[...] = jnp.zeros(o_ref.shape, o_ref.dtype)


def _dispatch(n_used, tok_of_slot, src, tm):
    w = src.shape[1]
    nt = tok_of_slot.shape[0] // tm
    tok_tiles = tok_of_slot.reshape(nt, 1, tm)
    tok_spec = lambda f: pl.BlockSpec((1, 1, tm), f, memory_space=pltpu.SMEM)
    return pl.pallas_call(
        functools.partial(_dispatch_kernel, tm),
        out_shape=jax.ShapeDtypeStruct((nt * tm, w), BF16),
        grid_spec=pltpu.PrefetchScalarGridSpec(
            num_scalar_prefetch=1,
            grid=(nt,),
            in_specs=[tok_spec(lambda i, nu: (i, 0, 0)),
                      tok_spec(lambda i, nu: (jnp.minimum(i + 1, nt - 1), 0, 0)),
                      pl.BlockSpec(memory_space=pl.ANY)],
            out_specs=pl.BlockSpec((tm, w), lambda i, nu: (i, 0)),
            scratch_shapes=[pltpu.VMEM((2, tm, w), src.dtype), pltpu.SemaphoreType.DMA((2,))]),
        compiler_params=_cp("arbitrary"),
        name="moe_dispatch",
    )(n_used, tok_tiles, tok_tiles, src)


def _stream_expert_weights(n_pass, n_tiles, width, te_ref, nu_ref, grp_ref, sem, weights):
    c, i = pl.program_id(0), pl.program_id(1)
    n_used = nu_ref[0]

    def copies(e, cc, slot):
        col = pl.multiple_of(cc * width, width)
        return [pltpu.make_async_copy(hbm.at[e, :, pl.ds(col, width)], buf.at[slot], sem.at[j, slot])
                for j, (hbm, buf) in enumerate(weights)]

    @pl.when((c == 0) & (i == 0))
    def _():
        grp_ref[0] = 0
        for cp in copies(te_ref[0], 0, 0):
            cp.start()

    e = te_ref[i]
    first = (i < n_used) & ((i == 0) | (e != te_ref[jnp.maximum(i - 1, 0)]))

    @pl.when(first)
    def _():
        grp = grp_ref[0]
        slot = grp % 2
        for cp in copies(e, c, slot):
            cp.wait()
        nxt = lax.while_loop(lambda j: (j < n_used) & (te_ref[jnp.minimum(j, n_tiles - 1)] == e),
                             lambda j: j + 1, i + 1)
        more = nxt < n_used

        @pl.when(more)
        def _():
            for cp in copies(te_ref[jnp.minimum(nxt, n_tiles - 1)], c, 1 - slot):
                cp.start()

        @pl.when(jnp.logical_not(more) & (c + 1 < n_pass))
        def _():
            for cp in copies(te_ref[0], c + 1, 1 - slot):
                cp.start()

        grp_ref[0] = grp + 1

    return (grp_ref[0] + 1) % 2


def _expert_up_kernel(n_pass, n_tiles, te_ref, nu_ref, x_ref, wg_hbm, wu_hbm, bg_ref, bu_ref, h_ref,
                      wg_buf, wu_buf, sem, grp_ref):
    slot = _stream_expert_weights(n_pass, n_tiles, h_ref.shape[1], te_ref, nu_ref, grp_ref, sem,
                                  [(wg_hbm, wg_buf), (wu_hbm, wu_buf)])

    @pl.when(pl.program_id(1) < nu_ref[0])
    def _():
        x = x_ref[...]

        def proj(w_buf, b_ref):
            return _dot(x, w_buf[slot].astype(BF16)) + b_ref[0]

        g = jnp.minimum(proj(wg_buf, bg_ref), SWIGLU_LIMIT)
        u = jnp.clip(proj(wu_buf, bu_ref), -SWIGLU_LIMIT, SWIGLU_LIMIT)
        h_ref[...] = (g * jax.nn.sigmoid(SWIGLU_ALPHA * g) * (u + 1.0)).astype(h_ref.dtype)

    @pl.when(pl.program_id(1) >= nu_ref[0])
    def _():
        h_ref[...] = jnp.zeros(h_ref.shape, h_ref.dtype)


def _used_tile(i, nu):
    return jnp.minimum(i, nu[0] - 1)


def _expert_up(tile_expert, n_used, xs, w_gate, w_up, b_gate, b_up, tm):
    s_pad = xs.shape[0]
    ne, d, f = w_gate.shape
    fc = _tile(f, 512, 128)
    nt = s_pad // tm
    wspec = lambda: pl.BlockSpec(memory_space=pl.ANY)
    bspec = lambda: pl.BlockSpec((1, 1, fc), lambda c, i, te, nu: (te[i], 0, c))
    return pl.pallas_call(
        functools.partial(_expert_up_kernel, f // fc, nt),
        out_shape=jax.ShapeDtypeStruct((s_pad, f), BF16),
        grid_spec=pltpu.PrefetchScalarGridSpec(
            num_scalar_prefetch=2,
            grid=(f // fc, nt),
            in_specs=[pl.BlockSpec((tm, d), lambda c, i, te, nu: (_used_tile(i, nu), 0)),
                      wspec(), wspec(), bspec(), bspec()],
            out_specs=pl.BlockSpec((tm, fc), lambda c, i, te, nu: (i, c)),
            scratch_shapes=[pltpu.VMEM((2, d, fc), F32), pltpu.VMEM((2, d, fc), F32),
                            pltpu.SemaphoreType.DMA((2, 2)), pltpu.SMEM((1,), I32)]),
        compiler_params=_cp("arbitrary", "arbitrary"),
        name="expert_gate_up",
    )(tile_expert, n_used, xs, w_gate, w_up, b_gate.reshape(ne, 1, f), b_up.reshape(ne, 1, f))


def _expert_down_kernel(n_pass, n_tiles, te_ref, nu_ref, h_ref, wd_hbm, bd_ref, y_ref, wd_buf, sem, grp_ref):
    slot = _stream_expert_weights(n_pass, n_tiles, y_ref.shape[1], te_ref, nu_ref, grp_ref, sem, [(wd_hbm, wd_buf)])

    @pl.when(pl.program_id(1) < nu_ref[0])
    def _():
        y_ref[...] = _dot(h_ref[...], wd_buf[slot].astype(BF16)) + bd_ref[0]

    @pl.when(pl.program_id(1) >= nu_ref[0])
    def _():
        y_ref[...] = jnp.zeros(y_ref.shape, y_ref.dtype)


def _expert_down(tile_expert, n_used, h, w_down, b_down, tm, tn):
    s_pad, f = h.shape
    ne, _, d = w_down.shape
    nt = s_pad // tm
    return pl.pallas_call(
        functools.partial(_expert_down_kernel, d // tn, nt),
        out_shape=jax.ShapeDtypeStruct((s_pad, d), F32),
        grid_spec=pltpu.PrefetchScalarGridSpec(
            num_scalar_prefetch=2,
            grid=(d // tn, nt),
            in_specs=[pl.BlockSpec((tm, f), lambda c, i, te, nu: (_used_tile(i, nu), 0)),
                      pl.BlockSpec(memory_space=pl.ANY),
                      pl.BlockSpec((1, 1, tn), lambda c, i, te, nu: (te[i], 0, c))],
            out_specs=pl.BlockSpec((tm, tn), lambda c, i, te, nu: (i, c)),
            scratch_shapes=[pltpu.VMEM((2, f, tn), F32), pltpu.SemaphoreType.DMA((1, 2)),
                            pltpu.SMEM((1,), I32)]),
        compiler_params=_cp("arbitrary", "arbitrary"),
        name="expert_down",
    )(tile_expert, n_used, h, w_down, b_down.reshape(ne, 1, d))


def _combine_kernel(tm, tiles_per_batch, nt, slot_ref, next_ref, yp_hbm, wt_ref, x1_ref, gn_ref, gate_ref,
                    o_ref, buf, sem):
    i = pl.program_id(0)

    def issue(s_ref, b):
        def body(g, c):
            for u in range(DMA_ISSUE_UNROLL):
                r = g * DMA_ISSUE_UNROLL + u
                for k in range(TOP_K):
                    pltpu.make_async_copy(yp_hbm.at[pl.ds(s_ref[0, 0, k * tm + r], 1)],
                                          buf.at[b, k, pl.ds(r, 1)], sem.at[b]).start(priority=k % 2)
            return c

        lax.fori_loop(0, tm // DMA_ISSUE_UNROLL, body, 0)

    @pl.when(i == 0)
    def _():
        issue(slot_ref, 0)

    @pl.when(i + 1 < nt)
    def _():
        issue(next_ref, (i + 1) % 2)

    b = i % 2
    for k in range(TOP_K):
        pltpu.make_async_copy(yp_hbm.at[pl.ds(0, tm)], buf.at[b, k], sem.at[b]).wait()

    wt = wt_ref[...]
    moe = wt[:, 0:1] * buf[b, 0]
    for k in range(1, TOP_K):
        moe = moe + wt[:, k:k + 1] * buf[b, k]
    r = i // tiles_per_batch
    o_ref[...] = x1_ref[...] + gate_ref[pl.ds(r, 1), :] * _rms(moe, gn_ref[...])


def _combine(slot_tiles, yp, wts, x1, gain, gate_tab, seq, tm):
    m, d = x1.shape
    nt = m // tm
    slot_spec = lambda f: pl.BlockSpec((1, 1, TOP_K * tm), f, memory_space=pltpu.SMEM)
    return pl.pallas_call(
        functools.partial(_combine_kernel, tm, seq // tm, nt),
        out_shape=jax.ShapeDtypeStruct((m, d), F32),
        grid=(nt,),
        in_specs=[slot_spec(lambda i: (i, 0, 0)),
                  slot_spec(lambda i: (jnp.minimum(i + 1, nt - 1), 0, 0)),
                  pl.BlockSpec(memory_space=pl.ANY),
                  pl.BlockSpec((tm, TOP_K), lambda i: (i, 0)),
                  pl.BlockSpec((tm, d), lambda i: (i, 0)),
                  pl.BlockSpec((1, d), lambda i: (0, 0)),
                  pl.BlockSpec((MOD_ROWS, d), lambda i: (0, 0))],
        out_specs=pl.BlockSpec((tm, d), lambda i: (i, 0)),
        scratch_shapes=[pltpu.VMEM((2, TOP_K, tm, d), F32), pltpu.SemaphoreType.DMA((2,))],
        compiler_params=_cp("arbitrary"),
        name="moe_combine",
    )(slot_tiles, slot_tiles, yp, wts, x1, gain.reshape(1, d), gate_tab)


def _rope_tables(seq):
    rows = seq // GRID_W
    axis_dim = HEAD_DIM // 2
    row_ids = jnp.repeat(jnp.arange(rows), GRID_W).astype(F32)
    col_ids = jnp.tile(jnp.arange(GRID_W), rows).astype(F32)
    inv_freq = ROPE_THETA ** (-jnp.arange(0, axis_dim, 2, dtype=F32) / axis_dim)
    ang_r = row_ids[:, None] * inv_freq[None, :]
    ang_c = col_ids[:, None] * inv_freq[None, :]
    cos = jnp.concatenate([jnp.cos(ang_r)] * 2 + [jnp.cos(ang_c)] * 2, axis=1)
    sin = jnp.concatenate([-jnp.sin(ang_r), jnp.sin(ang_r), -jnp.sin(ang_c), jnp.sin(ang_c)], axis=1)
    return cos, sin


def _layer(x, xc, c, c_ctx, lower_bounds, w_ada, b_ada, gains, w_in, q_gain, k_gain, hg_gain, w_out,
           w_router, b_router, w_gate, b_gate, w_up, b_up, w_down, b_down):
    batch, seq, d = x.shape
    ctx = xc.shape[1]
    attn_width = d // 2
    hgrn_width = d // 2
    kv_heads = attn_width // HEAD_DIM // GQA_GROUP
    kv_width = kv_heads * HEAD_DIM
    head_cols = attn_width + 2 * hgrn_width
    n_experts = w_router.shape[1]
    m, mc = batch * seq, batch * ctx
    assert batch < MOD_ROWS and seq % ctx == 0

    cond = jnp.concatenate([c, c_ctx[None, :], jnp.zeros((MOD_ROWS - batch - 1, d), F32)], axis=0)
    mod = _ada_modulation(cond, w_ada, b_ada)
    sh1, sc1, g1, sh2, sc2, g2 = [mod[:, i * d:(i + 1) * d] for i in range(N_MOD)]

    x2 = x.reshape(m, d)
    hh = _norm_modulate(x2, xc.reshape(mc, d), gains[0], sh1, sc1, seq, batch)

    p_head = _project(hh, w_in, 0, head_cols, m, BF16)
    p_kv = _project(hh, w_in, head_cols, 2 * kv_width, m + mc, BF16)
    p_f = _project(hh, w_in, head_cols + 2 * kv_width, 2 * hgrn_width, m + mc, F32)
    p_v = _project(hh, w_in, head_cols + 2 * kv_width + 2 * hgrn_width, hgrn_width, m + mc, BF16)

    cos, sin = _rope_tables(seq)
    attn = _attention(p_head, p_kv, q_gain, k_gain, cos, sin, batch, seq, ctx, kv_heads)
    hgrn = _hgrn(p_head, p_f, p_v, lower_bounds, hg_gain, batch, seq, ctx, attn_width, hgrn_width)
    y = _out_projection(attn, hgrn, w_out)

    x1, h2, logits_t = _epilogue(x2, y, gains[1], gains[2], g1, sh2, sc2, w_router.T, b_router, seq)
    idx, wts, pos, cnt = _route(logits_t)

    row_tile = 256
    n_tiles = -(-(TOP_K * m + n_experts * (row_tile - 1)) // row_tile)
    slots, tile_expert, _, ends = _slots(idx, pos, cnt, row_tile, n_tiles)
    tile_expert = tile_expert[0, :n_tiles]
    n_used = ends[n_experts - 1, 0:1]

    tok_tile = _tile(seq, 128, 8)
    nt = m // tok_tile
    slot_tiles = slots.reshape(TOP_K, nt, tok_tile).transpose(1, 0, 2).reshape(nt, 1, TOP_K * tok_tile)
    tok_of_slot = _inverse_slots(slot_tiles, n_tiles * row_tile, tok_tile)
    xs = _dispatch(n_used, tok_of_slot, h2, row_tile)
    hmid = _expert_up(tile_expert, n_used, xs, w_gate, w_up, b_gate, b_up, row_tile)
    ye = _expert_down(tile_expert, n_used, hmid, w_down, b_down, row_tile, _tile(d, 2048, 256))
    out = _combine(slot_tiles, ye, wts.T, x1, gains[3], g2, seq, tok_tile)
    return out.reshape(batch, seq, d)


def kernel(x, c, ctx, c_ctx, w_ada, b_ada, norm_gains, w_in, q_norm_gain, k_norm_gain, hgrn_lb_logits,
           hgrn_norm_gain, w_out, w_router, b_router, w_gate, b_gate, w_up, b_up, w_down, b_down):
    assert w_ada.shape[0] == 1, "single-layer stack only"
    lower_bounds = jnp.cumsum(jax.nn.softmax(hgrn_lb_logits.astype(F32), axis=0), axis=0)
    return _layer(x, ctx, c, c_ctx, lower_bounds[0], w_ada[0], b_ada[0], norm_gains[0], w_in[0],
                  q_norm_gain[0], k_norm_gain[0], hgrn_norm_gain[0], w_out[0], w_router[0], b_router[0],
                  w_gate[0], b_gate[0], w_up[0], b_up[0], w_down[0], b_down[0])
```

```python
import functools

import jax
import jax.numpy as jnp
from jax import lax
from jax.experimental import pallas as pl
from jax.experimental.pallas import tpu as pltpu

F32 = jnp.float32
BF16 = jnp.bfloat16
U32 = jnp.uint32
I32 = jnp.int32

GRID_W = 64
N_MOD = 6
EPS = 1e-6
HEAD_DIM = 128
GQA_GROUP = 4
ROPE_THETA = 10000.0
LOG2_E = 1.4426950408889634
HGRN_EXPAND = 128
CHUNK = 64
TOP_K = 4
SWIGLU_LIMIT = 7.0
SWIGLU_ALPHA = 1.702
MOD_ROWS = 8

V7X_VMEM_LIMIT = 56 * 1024 * 1024


def _cp(*sem):
    return pltpu.CompilerParams(dimension_semantics=sem, vmem_limit_bytes=V7X_VMEM_LIMIT)


def _tile(n, pref, mult):
    if n <= pref:
        return n
    t = (pref // mult) * mult
    while t > mult and n % t:
        t -= mult
    assert n % t == 0, (n, pref, mult)
    return t


def _dot(a, b):
    return jnp.dot(a, b, preferred_element_type=F32)


def _dot_nt(a, b):
    return lax.dot_general(a, b, (((1,), (1,)), ((), ())), preferred_element_type=F32)


def _dot_tn(a, b):
    return lax.dot_general(a, b, (((0,), (0,)), ((), ())), preferred_element_type=F32)


def _silu(v):
    return v * jax.nn.sigmoid(v)


def _rms(v, gain):
    return v * lax.rsqrt(jnp.mean(v * v, axis=-1, keepdims=True) + EPS) * gain


def _ada_kernel(c_ref, w_ref, b_ref, o_ref):
    s = _silu(c_ref[...]).astype(BF16)
    o_ref[...] = _dot(s, w_ref[...].astype(BF16)) + b_ref[...]


def _ada_modulation(cond, w_ada, b_ada):
    rows, d = cond.shape
    n = w_ada.shape[1]
    tn = _tile(n, 512, 128)
    return pl.pallas_call(
        _ada_kernel,
        out_shape=jax.ShapeDtypeStruct((rows, n), F32),
        grid=(n // tn,),
        in_specs=[pl.BlockSpec((rows, d), lambda j: (0, 0)),
                  pl.BlockSpec((d, tn), lambda j: (0, j)),
                  pl.BlockSpec((1, tn), lambda j: (0, j))],
        out_specs=pl.BlockSpec((rows, tn), lambda j: (0, j)),
        compiler_params=_cp("arbitrary"),
        name="ada_modulation",
    )(cond, w_ada, b_ada.reshape(1, n))


def _norm_mod_kernel(nx, tiles_per_batch, ctx_row, x_ref, c_ref, g_ref, sh_ref, sc_ref, o_ref):
    i = pl.program_id(0)
    is_x = i < nx
    v = jnp.where(is_x, x_ref[...], c_ref[...])
    r = jnp.where(is_x, i // tiles_per_batch, ctx_row)
    shift = sh_ref[pl.ds(r, 1), :]
    scale = sc_ref[pl.ds(r, 1), :]
    o_ref[...] = (_rms(v, g_ref[...]) * (1.0 + scale) + shift).astype(o_ref.dtype)


def _norm_modulate(x2, c2, gain, shift_tab, scale_tab, seq, ctx_row):
    m, d = x2.shape
    mc = c2.shape[0]
    tm = _tile(_gcd(seq, mc), 256, 16)
    nx, ncx = m // tm, mc // tm
    return pl.pallas_call(
        functools.partial(_norm_mod_kernel, nx, seq // tm, ctx_row),
        out_shape=jax.ShapeDtypeStruct((m + mc, d), BF16),
        grid=(nx + ncx,),
        in_specs=[pl.BlockSpec((tm, d), lambda i: (jnp.minimum(i, nx - 1), 0)),
                  pl.BlockSpec((tm, d), lambda i: (jnp.maximum(i - nx, 0), 0)),
                  pl.BlockSpec((1, d), lambda i: (0, 0)),
                  pl.BlockSpec((MOD_ROWS, d), lambda i: (0, 0)),
                  pl.BlockSpec((MOD_ROWS, d), lambda i: (0, 0))],
        out_specs=pl.BlockSpec((tm, d), lambda i: (i, 0)),
        compiler_params=_cp("arbitrary"),
        name="norm_modulate",
    )(x2, c2, gain.reshape(1, d), shift_tab, scale_tab)


def _gcd(a, b):
    while b:
        a, b = b, a % b
    return a


def _mm_kernel(a_ref, w_ref, o_ref):
    o_ref[...] = _dot(a_ref[...], w_ref[...].astype(BF16)).astype(o_ref.dtype)


def _project(a, w, col0, ncols, m_rows, out_dtype):
    k = a.shape[1]
    tn = _tile(_gcd(ncols, col0) if col0 else ncols, 512, 128)
    tm = _tile(m_rows, 1024, 16)
    jb = col0 // tn
    return pl.pallas_call(
        _mm_kernel,
        out_shape=jax.ShapeDtypeStruct((m_rows, ncols), out_dtype),
        grid=(ncols // tn, m_rows // tm),
        in_specs=[pl.BlockSpec((tm, k), lambda j, i: (i, 0)),
                  pl.BlockSpec((k, tn), lambda j, i: (0, jb + j))],
        out_specs=pl.BlockSpec((tm, tn), lambda j, i: (i, j)),
        compiler_params=_cp("arbitrary", "arbitrary"),
        name="in_projection",
    )(a, w)


def _mm2_kernel(ka, a1_ref, a2_ref, w_ref, o_ref):
    w = w_ref[...].astype(BF16)
    o_ref[...] = _dot(a1_ref[...], w[:ka]) + _dot(a2_ref[...], w[ka:])


def _out_projection(a1, a2, w):
    m, ka = a1.shape
    kb = a2.shape[1]
    n = w.shape[1]
    tn = _tile(n, 512, 128)
    tm = _tile(m, 1024, 16)
    return pl.pallas_call(
        functools.partial(_mm2_kernel, ka),
        out_shape=jax.ShapeDtypeStruct((m, n), F32),
        grid=(n // tn, m // tm),
        in_specs=[pl.BlockSpec((tm, ka), lambda j, i: (i, 0)),
                  pl.BlockSpec((tm, kb), lambda j, i: (i, 0)),
                  pl.BlockSpec((ka + kb, tn), lambda j, i: (0, j))],
        out_specs=pl.BlockSpec((tm, tn), lambda j, i: (i, j)),
        compiler_params=_cp("arbitrary", "arbitrary"),
        name="out_projection",
    )(a1, a2, w)


def _rope(v, cos, sin_signed):
    lane = lax.broadcasted_iota(I32, v.shape, 1)
    partner = jnp.where((lane & 32) == 0, pltpu.roll(v, HEAD_DIM - 32, 1), pltpu.roll(v, 32, 1))
    return v * cos + partner * sin_signed


def _attn_kernel(tq, seq, ctx, q_ref, kl_ref, kc_ref, vl_ref, vc_ref, qg_ref, kg_ref, cos_ref, sin_ref,
                 o_ref, k_sc, v_sc):
    qi = pl.program_id(2)

    @pl.when(qi == 0)
    def _():
        kg = kg_ref[...]
        k_sc[0:ctx, :] = _rms(kc_ref[...].astype(F32), kg).astype(BF16)
        kl = _rope(_rms(kl_ref[...].astype(F32), kg), cos_ref[...], sin_ref[...])
        k_sc[ctx:ctx + seq, :] = kl.astype(BF16)
        v_sc[0:ctx, :] = vc_ref[...]
        v_sc[ctx:ctx + seq, :] = vl_ref[...]

    r0 = pl.multiple_of(qi * tq, tq)
    cos = cos_ref[pl.ds(r0, tq), :]
    sin = sin_ref[pl.ds(r0, tq), :]
    qg = qg_ref[...] * (HEAD_DIM ** -0.5 * LOG2_E)
    keys = k_sc[...]
    vals = v_sc[...]
    heads = [slice(h * HEAD_DIM, (h + 1) * HEAD_DIM) for h in range(GQA_GROUP)]
    scores = [_dot_nt(_rope(_rms(q_ref[:, sl].astype(F32), qg), cos, sin).astype(BF16), keys) for sl in heads]
    probs = [jnp.exp2(s - jnp.max(s, axis=-1, keepdims=True)) for s in scores]
    denoms = [jnp.sum(p, axis=-1, keepdims=True) for p in probs]
    outs = [_dot(p.astype(BF16), vals) for p in probs]
    for sl, o, denom in zip(heads, outs, denoms):
        o_ref[:, sl] = (o / denom).astype(o_ref.dtype)


def _attention(p_head, p_kv, q_gain, k_gain, cos, sin_signed, batch, seq, ctx, kv_heads):
    tq = _tile(seq, 256, 16)
    nq = seq // tq
    gw = GQA_GROUP * HEAD_DIM
    ctx_blk0 = batch * seq // ctx
    return pl.pallas_call(
        functools.partial(_attn_kernel, tq, seq, ctx),
        out_shape=jax.ShapeDtypeStruct((batch * seq, kv_heads * gw), BF16),
        grid=(batch, kv_heads, nq),
        in_specs=[pl.BlockSpec((tq, gw), lambda b, h, q: (b * nq + q, h)),
                  pl.BlockSpec((seq, HEAD_DIM), lambda b, h, q: (b, h)),
                  pl.BlockSpec((ctx, HEAD_DIM), lambda b, h, q: (ctx_blk0 + b, h)),
                  pl.BlockSpec((seq, HEAD_DIM), lambda b, h, q: (b, kv_heads + h)),
                  pl.BlockSpec((ctx, HEAD_DIM), lambda b, h, q: (ctx_blk0 + b, kv_heads + h)),
                  pl.BlockSpec((1, HEAD_DIM), lambda b, h, q: (0, 0)),
                  pl.BlockSpec((1, HEAD_DIM), lambda b, h, q: (0, 0)),
                  pl.BlockSpec((seq, HEAD_DIM), lambda b, h, q: (0, 0)),
                  pl.BlockSpec((seq, HEAD_DIM), lambda b, h, q: (0, 0))],
        out_specs=pl.BlockSpec((tq, gw), lambda b, h, q: (b * nq + q, h)),
        scratch_shapes=[pltpu.VMEM((ctx + seq, HEAD_DIM), BF16),
                        pltpu.VMEM((ctx + seq, HEAD_DIM), BF16)],
        compiler_params=_cp("arbitrary", "arbitrary", "arbitrary"),
        name="gqa_attention",
    )(p_head, p_kv, p_kv, p_kv, p_kv, q_gain.reshape(1, HEAD_DIM), k_gain.reshape(1, HEAD_DIM), cos, sin_signed)


def _hgrn_kernel(hpb, seq, ctx, rows, lb_ref, gain_ref, q_ref, g_ref, ff_ref, fb_ref, v_ref,
                 cff_ref, cfb_ref, cv_ref, o_ref, st_ref, of_ref, ob_ref):
    cpb = rows // CHUNK
    lg_chunk = CHUNK.bit_length() - 1
    row = lax.broadcasted_iota(I32, (rows, rows), 0)
    col = lax.broadcasted_iota(I32, (rows, rows), 1)
    same = (row >> lg_chunk) == (col >> lg_chunk)
    masks = (same & (col <= row), same & (col >= row))
    tris = tuple(jnp.where(m, 1.0, 0.0).astype(BF16) for m in masks)
    row_chunk = lax.broadcasted_iota(I32, (rows, HGRN_EXPAND), 0) >> lg_chunk

    st_ref[...] = jnp.zeros(st_ref.shape, F32)

    def lane_stack(a):
        return jnp.concatenate([jnp.where(row_chunk == c, a, 0.0).astype(BF16) for c in range(cpb)], axis=1)

    def chunk_rows(b, off):
        return jnp.concatenate([jnp.broadcast_to(b[c * CHUNK + off:c * CHUNK + off + 1, :], (CHUNK, HGRN_EXPAND))
                                for c in range(cpb)], axis=0)

    def blocks(items, with_out):
        n = len(items)
        last = [0 if it[5] else CHUNK - 1 for it in items]
        mid = [CHUNK // 2 if it[5] else CHUNK // 2 - 1 for it in items]
        ks, splits = [], []
        for _, raw, _, _, lb, _ in items:
            f = lb + (1.0 - lb) * jax.nn.sigmoid(raw)
            logf = jnp.log(f)
            hi = logf.astype(BF16)
            ks.append(1.0 - f)
            splits.append((hi, (logf - hi.astype(F32)).astype(BF16)))
        bs = [_dot(tris[items[j][5]], splits[j][0]) + _dot(tris[items[j][5]], splits[j][1]) for j in range(n)]
        uts = [_dot_tn(items[j][2], lane_stack(ks[j] * jnp.exp(chunk_rows(bs[j], last[j]) - bs[j]))) for j in range(n)]
        slabs = []
        for j in range(n):
            st = st_ref[items[j][0]]
            starts = [None] * cpb
            for c in (reversed(range(cpb)) if items[j][5] else range(cpb)):
                starts[c] = st.astype(BF16)
                st = (st * jnp.exp(bs[j][c * CHUNK + last[j]:c * CHUNK + last[j] + 1, :])
                      + uts[j][:, c * HGRN_EXPAND:(c + 1) * HGRN_EXPAND])
            st_ref[items[j][0]] = st
            slabs.append(jnp.concatenate(starts, axis=1))
        if not with_out:
            return None
        scores = []
        for j in range(n):
            b_mid = chunk_rows(bs[j], mid[j])
            qt = (items[j][3] * jnp.exp(bs[j] - b_mid)).astype(BF16)
            kt = (ks[j] * jnp.exp(b_mid - bs[j])).astype(BF16)
            scores.append(_dot_nt(qt, kt))
        outs = []
        for j in range(n):
            a = jnp.where(masks[items[j][5]], scores[j], 0.0).astype(BF16)
            inter = _dot_nt(lane_stack(items[j][3] * jnp.exp(bs[j])), slabs[j])
            outs.append(_dot(a, items[j][2]) + inter)
        return outs

    def ctx_body(i, carry):
        r_f = pl.multiple_of(i * rows, rows)
        r_b = pl.multiple_of((ctx // rows - 1 - i) * rows, rows)
        items = []
        for g in range(hpb):
            sl = slice(g * HGRN_EXPAND, (g + 1) * HGRN_EXPAND)
            items.append((2 * g, cff_ref[pl.ds(r_f, rows), sl], cv_ref[pl.ds(r_f, rows), sl], None, lb_ref[0:1, sl], 0))
            items.append((2 * g + 1, cfb_ref[pl.ds(r_b, rows), sl], cv_ref[pl.ds(r_b, rows), sl], None,
                          lb_ref[1:2, sl], 1))
        blocks(items, False)
        return carry

    lax.fori_loop(0, ctx // rows, ctx_body, 0)

    def seq_body(i, carry):
        r_f = pl.multiple_of(i * rows, rows)
        r_b = pl.multiple_of((seq // rows - 1 - i) * rows, rows)
        items, dests = [], []
        for g in range(hpb):
            sl = slice(g * HGRN_EXPAND, (g + 1) * HGRN_EXPAND)
            for rev, r, f_ref, out_ref in ((0, r_f, ff_ref, of_ref), (1, r_b, fb_ref, ob_ref)):
                q = _silu(q_ref[pl.ds(r, rows), sl].astype(F32))
                items.append((2 * g + rev, f_ref[pl.ds(r, rows), sl], v_ref[pl.ds(r, rows), sl], q,
                              lb_ref[rev:rev + 1, sl], rev))
                dests.append((out_ref, r, sl))
        for (out_ref, r, sl), o in zip(dests, blocks(items, True)):
            out_ref[pl.ds(r, rows), sl] = o
        return carry

    lax.fori_loop(0, seq // rows, seq_body, 0)

    rb = _tile(seq, 256, 8)

    def read_body(i, carry):
        r = pl.multiple_of(i * rb, rb)
        for g in range(hpb):
            sl = slice(g * HGRN_EXPAND, (g + 1) * HGRN_EXPAND)
            o = of_ref[pl.ds(r, rb), sl] + ob_ref[pl.ds(r, rb), sl]
            y = _rms(o, gain_ref[...]) * _silu(g_ref[pl.ds(r, rb), sl].astype(F32))
            o_ref[pl.ds(r, rb), sl] = y.astype(o_ref.dtype)
        return carry

    lax.fori_loop(0, seq // rb, read_body, 0)


def _hgrn(p_head, p_f, p_v, lower_bounds, hg_gain, batch, seq, ctx, attn_width, hgrn_width):
    heads = hgrn_width // HGRN_EXPAND
    hpb = 2 if heads % 2 == 0 else 1
    bw = hpb * HGRN_EXPAND
    nh = heads // hpb
    q0, g0 = attn_width // bw, (attn_width + hgrn_width) // bw
    ctx_blk0 = batch * seq // ctx
    rows = _tile(_gcd(seq, ctx), 256, CHUNK)
    lat = lambda c0: pl.BlockSpec((seq, bw), lambda b, h: (b, c0 + h))
    cx = lambda c0: pl.BlockSpec((ctx, bw), lambda b, h: (ctx_blk0 + b, c0 + h))
    return pl.pallas_call(
        functools.partial(_hgrn_kernel, hpb, seq, ctx, rows),
        out_shape=jax.ShapeDtypeStruct((batch * seq, hgrn_width), BF16),
        grid=(batch, nh),
        in_specs=[pl.BlockSpec((2, bw), lambda b, h: (0, h)),
                  pl.BlockSpec((1, HGRN_EXPAND), lambda b, h: (0, 0)),
                  lat(q0), lat(g0), lat(0), lat(nh), lat(0), cx(0), cx(nh), cx(0)],
        out_specs=pl.BlockSpec((seq, bw), lambda b, h: (b, h)),
        scratch_shapes=[pltpu.VMEM((2 * hpb, HGRN_EXPAND, HGRN_EXPAND), F32),
                        pltpu.VMEM((seq, bw), F32),
                        pltpu.VMEM((seq, bw), F32)],
        compiler_params=_cp("arbitrary", "arbitrary"),
        name="hgrn2_scan",
    )(lower_bounds, hg_gain.reshape(1, HGRN_EXPAND), p_head, p_head, p_f, p_f, p_v, p_f, p_f, p_v)


def _epilogue_kernel(tiles_per_batch, x_ref, y_ref, g1n_ref, g2n_ref, gate_ref, sh_ref, sc_ref, wr_ref, br_ref,
                     x1_ref, hp_ref, lg_ref):
    r = pl.program_id(0) // tiles_per_batch
    x1 = x_ref[...] + gate_ref[pl.ds(r, 1), :] * _rms(y_ref[...], g1n_ref[...])
    x1_ref[...] = x1
    h = _rms(x1, g2n_ref[...]) * (1.0 + sc_ref[pl.ds(r, 1), :]) + sh_ref[pl.ds(r, 1), :]
    hp_ref[...] = h
    w = wr_ref[...]
    w_hi = w.astype(BF16)
    w_lo = (w - w_hi.astype(F32)).astype(BF16)
    h_hi = h.astype(BF16)
    h_lo = (h - h_hi.astype(F32)).astype(BF16)
    lg_ref[...] = _dot_nt(w_hi, h_hi) + _dot_nt(w_hi, h_lo) + _dot_nt(w_lo, h_hi) + br_ref[...]


def _epilogue(x2, y, gain1, gain2, gate_tab, shift_tab, scale_tab, w_router_t, b_router, seq):
    m, d = x2.shape
    ne = w_router_t.shape[0]
    tm = _tile(seq, 256, 128)
    row = lambda: pl.BlockSpec((tm, d), lambda i: (i, 0))
    tab = lambda: pl.BlockSpec((MOD_ROWS, d), lambda i: (0, 0))
    vec = lambda: pl.BlockSpec((1, d), lambda i: (0, 0))
    return pl.pallas_call(
        functools.partial(_epilogue_kernel, seq // tm),
        out_shape=(jax.ShapeDtypeStruct((m, d), F32),
                   jax.ShapeDtypeStruct((m, d), F32),
                   jax.ShapeDtypeStruct((ne, m), F32)),
        grid=(m // tm,),
        in_specs=[row(), row(), vec(), vec(), tab(), tab(), tab(),
                  pl.BlockSpec((ne, d), lambda i: (0, 0)),
                  pl.BlockSpec((ne, 1), lambda i: (0, 0))],
        out_specs=(row(), row(), pl.BlockSpec((ne, tm), lambda i: (0, i))),
        compiler_params=_cp("arbitrary"),
        name="residual_norm_router",
    )(x2, y, gain1.reshape(1, d), gain2.reshape(1, d), gate_tab, shift_tab, scale_tab,
      w_router_t, b_router.reshape(ne, 1))


def _route_kernel(lg_ref, idx_ref, wt_ref, pos_ref, cnt_ref, carry_ref):
    i = pl.program_id(0)
    ne, tn = lg_ref.shape

    @pl.when(i == 0)
    def _():
        carry_ref[...] = jnp.zeros(carry_ref.shape, F32)

    lg = lg_ref[...]
    eidx = lax.broadcasted_iota(I32, (ne, tn), 0)
    vals, hots = [], []
    for k in range(TOP_K):
        m = jnp.max(lg, axis=0, keepdims=True)
        sel = jnp.min(jnp.where(lg == m, eidx, ne), axis=0, keepdims=True)
        hot = eidx == sel
        idx_ref[k:k + 1, :] = sel
        vals.append(m)
        hots.append(hot)
        lg = jnp.where(hot, -jnp.inf, lg)
    exps = [jnp.exp(v - vals[0]) for v in vals]
    denom = exps[0] + exps[1] + exps[2] + exps[3]
    for k in range(TOP_K):
        wt_ref[k:k + 1, :] = exps[k] / denom

    member = hots[0] | hots[1] | hots[2] | hots[3]
    t_row = lax.broadcasted_iota(I32, (tn, tn), 0)
    t_col = lax.broadcasted_iota(I32, (tn, tn), 1)
    before = (t_row < t_col).astype(BF16)
    rank = _dot(member.astype(BF16), before) + carry_ref[:, 0:1]
    for k in range(TOP_K):
        pos_ref[k:k + 1, :] = jnp.sum(jnp.where(hots[k], rank, 0.0), axis=0, keepdims=True).astype(I32)
    carry_ref[...] = carry_ref[...] + jnp.sum(member.astype(F32), axis=1, keepdims=True)
    cnt_ref[...] = carry_ref[...].astype(I32)


def _route(logits_t):
    ne, m = logits_t.shape
    tn = _tile(m, 512, 128)
    out4 = lambda: pl.BlockSpec((TOP_K, tn), lambda i: (0, i))
    return pl.pallas_call(
        _route_kernel,
        out_shape=(jax.ShapeDtypeStruct((TOP_K, m), I32),
                   jax.ShapeDtypeStruct((TOP_K, m), F32),
                   jax.ShapeDtypeStruct((TOP_K, m), I32),
                   jax.ShapeDtypeStruct((ne, 128), I32)),
        grid=(m // tn,),
        in_specs=[pl.BlockSpec((ne, tn), lambda i: (0, i))],
        out_specs=(out4(), out4(), out4(), pl.BlockSpec((ne, 128), lambda i: (0, 0))),
        scratch_shapes=[pltpu.VMEM((ne, 128), F32)],
        compiler_params=_cp("arbitrary"),
        name="top4_route",
    )(logits_t)


def _slots_kernel(row_tile, lg_tile, idx_ref, pos_ref, cnt_ref, slot_ref, te_ref, start_ref, ends_ref):
    ne = cnt_ref.shape[0]
    tn = idx_ref.shape[1]
    tiles = (cnt_ref[...] + (row_tile - 1)) >> lg_tile
    e_row = lax.broadcasted_iota(I32, (ne, ne), 0)
    e_col = lax.broadcasted_iota(I32, (ne, ne), 1)
    upto = jnp.where(e_col <= e_row, 1.0, 0.0).astype(BF16)
    ends = _dot(upto, tiles.astype(F32).astype(BF16)).astype(I32)
    start = (ends - tiles) << lg_tile
    start_ref[...] = start
    ends_ref[...] = ends
    eidx = lax.broadcasted_iota(I32, (ne, tn), 0)
    start_col = start[:, 0:1]
    for k in range(TOP_K):
        hit = eidx == idx_ref[k:k + 1, :]
        slot_ref[k:k + 1, :] = jnp.sum(jnp.where(hit, start_col, 0), axis=0, keepdims=True) + pos_ref[k:k + 1, :]
    tile_i = lax.broadcasted_iota(I32, (ne, te_ref.shape[1]), 1)
    te = jnp.sum(jnp.where(tile_i >= ends[:, 0:1], 1, 0), axis=0, keepdims=True)
    te_ref[...] = jnp.minimum(te, ne - 1)


def _slots(idx, pos, cnt, row_tile, n_tiles):
    ne = cnt.shape[0]
    m = idx.shape[1]
    tn = _tile(m, 2048, 128)
    ntp = -(-n_tiles // 128) * 128
    lg_tile = row_tile.bit_length() - 1
    assert 1 << lg_tile == row_tile and n_tiles <= 256
    io4 = lambda: pl.BlockSpec((TOP_K, tn), lambda i: (0, i))
    small = lambda w: pl.BlockSpec((ne, w), lambda i: (0, 0))
    return pl.pallas_call(
        functools.partial(_slots_kernel, row_tile, lg_tile),
        out_shape=(jax.ShapeDtypeStruct((TOP_K, m), I32),
                   jax.ShapeDtypeStruct((1, ntp), I32),
                   jax.ShapeDtypeStruct((ne, 128), I32),
                   jax.ShapeDtypeStruct((ne, 128), I32)),
        grid=(m // tn,),
        in_specs=[io4(), io4(), small(128)],
        out_specs=(io4(), pl.BlockSpec((1, ntp), lambda i: (0, 0)), small(128), small(128)),
        compiler_params=_cp("arbitrary"),
        name="route_slots",
    )(idx, pos, cnt)


DMA_ISSUE_UNROLL = 8


def _inverse_kernel(tm, n_slots, slot_ref, tok_ref):
    i = pl.program_id(0)

    @pl.when(i == 0)
    def _():
        def clear(g, c):
            for u in range(DMA_ISSUE_UNROLL):
                tok_ref[g * DMA_ISSUE_UNROLL + u] = 0
            return c

        lax.fori_loop(0, n_slots // DMA_ISSUE_UNROLL, clear, 0)

    def body(g, c):
        for u in range(DMA_ISSUE_UNROLL):
            r = g * DMA_ISSUE_UNROLL + u
            for k in range(TOP_K):
                tok_ref[slot_ref[0, 0, k * tm + r]] = i * tm + r
        return c

    lax.fori_loop(0, tm // DMA_ISSUE_UNROLL, body, 0)


def _inverse_slots(slot_tiles, n_slots, tm):
    nt = slot_tiles.shape[0]
    return pl.pallas_call(
        functools.partial(_inverse_kernel, tm, n_slots),
        out_shape=jax.ShapeDtypeStruct((n_slots,), I32),
        grid=(nt,),
        in_specs=[pl.BlockSpec((1, 1, TOP_K * tm), lambda i: (i, 0, 0), memory_space=pltpu.SMEM)],
        out_specs=pl.BlockSpec(memory_space=pltpu.SMEM),
        compiler_params=_cp("arbitrary"),
        name="route_inverse",
    )(slot_tiles)


def _dispatch_kernel(tm, nu_ref, tok_ref, next_ref, src_hbm, o_ref, buf, sem):
    i = pl.program_id(0)
    n_used = nu_ref[0]

    def issue(t_ref, b):
        def body(g, c):
            for u in range(DMA_ISSUE_UNROLL):
                r = g * DMA_ISSUE_UNROLL + u
                pltpu.make_async_copy(src_hbm.at[pl.ds(t_ref[0, 0, r], 1)], buf.at[b, pl.ds(r, 1)],
                                      sem.at[b]).start(priority=u % 2)
            return c

        lax.fori_loop(0, tm // DMA_ISSUE_UNROLL, body, 0)

    @pl.when(i == 0)
    def _():
        issue(tok_ref, 0)

    @pl.when(i + 1 < n_used)
    def _():
        issue(next_ref, (i + 1) % 2)

    @pl.when(i < n_used)
    def _():
        b = i % 2
        pltpu.make_async_copy(src_hbm.at[pl.ds(0, tm)], buf.at[b], sem.at[b]).wait()
        o_ref[...] = buf[b].astype(o_ref.dtype)

    @pl.when(i >= n_used)
    def _():
        o_ref[...] = jnp.zeros(o_ref.shape, o_ref.dtype)


def _dispatch(n_used, tok_of_slot, src, tm):
    w = src.shape[1]
    nt = tok_of_slot.shape[0] // tm
    tok_tiles = tok_of_slot.reshape(nt, 1, tm)
    tok_spec = lambda f: pl.BlockSpec((1, 1, tm), f, memory_space=pltpu.SMEM)
    return pl.pallas_call(
        functools.partial(_dispatch_kernel, tm),
        out_shape=jax.ShapeDtypeStruct((nt * tm, w), BF16),
        grid_spec=pltpu.PrefetchScalarGridSpec(
            num_scalar_prefetch=1,
            grid=(nt,),
            in_specs=[tok_spec(lambda i, nu: (i, 0, 0)),
                      tok_spec(lambda i, nu: (jnp.minimum(i + 1, nt - 1), 0, 0)),
                      pl.BlockSpec(memory_space=pl.ANY)],
            out_specs=pl.BlockSpec((tm, w), lambda i, nu: (i, 0)),
            scratch_shapes=[pltpu.VMEM((2, tm, w), src.dtype), pltpu.SemaphoreType.DMA((2,))]),
        compiler_params=_cp("arbitrary"),
        name="moe_dispatch",
    )(n_used, tok_tiles, tok_tiles, src)


def _stream_expert_weights(n_pass, n_tiles, width, te_ref, nu_ref, grp_ref, sem, weights):
    c, i = pl.program_id(0), pl.program_id(1)
    n_used = nu_ref[0]

    def copies(e, cc, slot):
        col = pl.multiple_of(cc * width, width)
        return [pltpu.make_async_copy(hbm.at[e, :, pl.ds(col, width)], buf.at[slot], sem.at[j, slot])
                for j, (hbm, buf) in enumerate(weights)]

    @pl.when((c == 0) & (i == 0))
    def _():
        grp_ref[0] = 0
        for cp in copies(te_ref[0], 0, 0):
            cp.start()

    e = te_ref[i]
    first = (i < n_used) & ((i == 0) | (e != te_ref[jnp.maximum(i - 1, 0)]))

    @pl.when(first)
    def _():
        grp = grp_ref[0]
        slot = grp % 2
        for cp in copies(e, c, slot):
            cp.wait()
        nxt = lax.while_loop(lambda j: (j < n_used) & (te_ref[jnp.minimum(j, n_tiles - 1)] == e),
                             lambda j: j + 1, i + 1)
        more = nxt < n_used

        @pl.when(more)
        def _():
            for cp in copies(te_ref[jnp.minimum(nxt, n_tiles - 1)], c, 1 - slot):
                cp.start()

        @pl.when(jnp.logical_not(more) & (c + 1 < n_pass))
        def _():
            for cp in copies(te_ref[0], c + 1, 1 - slot):
                cp.start()

        grp_ref[0] = grp + 1

    return (grp_ref[0] + 1) % 2


def _expert_up_kernel(n_pass, n_tiles, te_ref, nu_ref, x_ref, wg_hbm, wu_hbm, bg_ref, bu_ref, h_ref,
                      wg_buf, wu_buf, sem, grp_ref):
    slot = _stream_expert_weights(n_pass, n_tiles, h_ref.shape[1], te_ref, nu_ref, grp_ref, sem,
                                  [(wg_hbm, wg_buf), (wu_hbm, wu_buf)])

    @pl.when(pl.program_id(1) < nu_ref[0])
    def _():
        x = x_ref[...]

        def proj(w_buf, b_ref):
            return _dot(x, w_buf[slot].astype(BF16)) + b_ref[0]

        g = jnp.minimum(proj(wg_buf, bg_ref), SWIGLU_LIMIT)
        u = jnp.clip(proj(wu_buf, bu_ref), -SWIGLU_LIMIT, SWIGLU_LIMIT)
        h_ref[...] = (g * jax.nn.sigmoid(SWIGLU_ALPHA * g) * (u + 1.0)).astype(h_ref.dtype)

    @pl.when(pl.program_id(1) >= nu_ref[0])
    def _():
        h_ref[...] = jnp.zeros(h_ref.shape, h_ref.dtype)


def _used_tile(i, nu):
    return jnp.minimum(i, nu[0] - 1)


def _expert_up(tile_expert, n_used, xs, w_gate, w_up, b_gate, b_up, tm):
    s_pad = xs.shape[0]
    ne, d, f = w_gate.shape
    fc = _tile(f, 512, 128)
    nt = s_pad // tm
    wspec = lambda: pl.BlockSpec(memory_space=pl.ANY)
    bspec = lambda: pl.BlockSpec((1, 1, fc), lambda c, i, te, nu: (te[i], 0, c))
    return pl.pallas_call(
        functools.partial(_expert_up_kernel, f // fc, nt),
        out_shape=jax.ShapeDtypeStruct((s_pad, f), BF16),
        grid_spec=pltpu.PrefetchScalarGridSpec(
            num_scalar_prefetch=2,
            grid=(f // fc, nt),
            in_specs=[pl.BlockSpec((tm, d), lambda c, i, te, nu: (_used_tile(i, nu), 0)),
                      wspec(), wspec(), bspec(), bspec()],
            out_specs=pl.BlockSpec((tm, fc), lambda c, i, te, nu: (i, c)),
            scratch_shapes=[pltpu.VMEM((2, d, fc), F32), pltpu.VMEM((2, d, fc), F32),
                            pltpu.SemaphoreType.DMA((2, 2)), pltpu.SMEM((1,), I32)]),
        compiler_params=_cp("arbitrary", "arbitrary"),
        name="expert_gate_up",
    )(tile_expert, n_used, xs, w_gate, w_up, b_gate.reshape(ne, 1, f), b_up.reshape(ne, 1, f))


def _expert_down_kernel(n_pass, n_tiles, te_ref, nu_ref, h_ref, wd_hbm, bd_ref, y_ref, wd_buf, sem, grp_ref):
    slot = _stream_expert_weights(n_pass, n_tiles, y_ref.shape[1], te_ref, nu_ref, grp_ref, sem, [(wd_hbm, wd_buf)])

    @pl.when(pl.program_id(1) < nu_ref[0])
    def _():
        y_ref[...] = _dot(h_ref[...], wd_buf[slot].astype(BF16)) + bd_ref[0]

    @pl.when(pl.program_id(1) >= nu_ref[0])
    def _():
        y_ref[...] = jnp.zeros(y_ref.shape, y_ref.dtype)


def _expert_down(tile_expert, n_used, h, w_down, b_down, tm, tn):
    s_pad, f = h.shape
    ne, _, d = w_down.shape
    nt = s_pad // tm
    return pl.pallas_call(
        functools.partial(_expert_down_kernel, d // tn, nt),
        out_shape=jax.ShapeDtypeStruct((s_pad, d), F32),
        grid_spec=pltpu.PrefetchScalarGridSpec(
            num_scalar_prefetch=2,
            grid=(d // tn, nt),
            in_specs=[pl.BlockSpec((tm, f), lambda c, i, te, nu: (_used_tile(i, nu), 0)),
                      pl.BlockSpec(memory_space=pl.ANY),
                      pl.BlockSpec((1, 1, tn), lambda c, i, te, nu: (te[i], 0, c))],
            out_specs=pl.BlockSpec((tm, tn), lambda c, i, te, nu: (i, c)),
            scratch_shapes=[pltpu.VMEM((2, f, tn), F32), pltpu.SemaphoreType.DMA((1, 2)),
                            pltpu.SMEM((1,), I32)]),
        compiler_params=_cp("arbitrary", "arbitrary"),
        name="expert_down",
    )(tile_expert, n_used, h, w_down, b_down.reshape(ne, 1, d))


def _combine_kernel(tm, tiles_per_batch, nt, slot_ref, next_ref, yp_hbm, wt_ref, x1_ref, gn_ref, gate_ref,
                    o_ref, buf, sem):
    i = pl.program_id(0)

    def issue(s_ref, b):
        def body(g, c):
            for u in range(DMA_ISSUE_UNROLL):
                r = g * DMA_ISSUE_UNROLL + u
                for k in range(TOP_K):
                    pltpu.make_async_copy(yp_hbm.at[pl.ds(s_ref[0, 0, k * tm + r], 1)],
                                          buf.at[b, k, pl.ds(r, 1)], sem.at[b]).start(priority=k % 2)
            return c

        lax.fori_loop(0, tm // DMA_ISSUE_UNROLL, body, 0)

    @pl.when(i == 0)
    def _():
        issue(slot_ref, 0)

    @pl.when(i + 1 < nt)
    def _():
        issue(next_ref, (i + 1) % 2)

    b = i % 2
    for k in range(TOP_K):
        pltpu.make_async_copy(yp_hbm.at[pl.ds(0, tm)], buf.at[b, k], sem.at[b]).wait()

    wt = wt_ref[...]
    moe = wt[:, 0:1] * buf[b, 0]
    for k in range(1, TOP_K):
        moe = moe + wt[:, k:k + 1] * buf[b, k]
    r = i // tiles_per_batch
    o_ref[...] = x1_ref[...] + gate_ref[pl.ds(r, 1), :] * _rms(moe, gn_ref[...])


def _combine(slot_tiles, yp, wts, x1, gain, gate_tab, seq, tm):
    m, d = x1.shape
    nt = m // tm
    slot_spec = lambda f: pl.BlockSpec((1, 1, TOP_K * tm), f, memory_space=pltpu.SMEM)
    return pl.pallas_call(
        functools.partial(_combine_kernel, tm, seq // tm, nt),
        out_shape=jax.ShapeDtypeStruct((m, d), F32),
        grid=(nt,),
        in_specs=[slot_spec(lambda i: (i, 0, 0)),
                  slot_spec(lambda i: (jnp.minimum(i + 1, nt - 1), 0, 0)),
                  pl.BlockSpec(memory_space=pl.ANY),
                  pl.BlockSpec((tm, TOP_K), lambda i: (i, 0)),
                  pl.BlockSpec((tm, d), lambda i: (i, 0)),
                  pl.BlockSpec((1, d), lambda i: (0, 0)),
                  pl.BlockSpec((MOD_ROWS, d), lambda i: (0, 0))],
        out_specs=pl.BlockSpec((tm, d), lambda i: (i, 0)),
        scratch_shapes=[pltpu.VMEM((2, TOP_K, tm, d), F32), pltpu.SemaphoreType.DMA((2,))],
        compiler_params=_cp("arbitrary"),
        name="moe_combine",
    )(slot_tiles, slot_tiles, yp, wts, x1, gain.reshape(1, d), gate_tab)


def _rope_tables(seq):
    rows = seq // GRID_W
    axis_dim = HEAD_DIM // 2
    row_ids = jnp.repeat(jnp.arange(rows), GRID_W).astype(F32)
    col_ids = jnp.tile(jnp.arange(GRID_W), rows).astype(F32)
    inv_freq = ROPE_THETA ** (-jnp.arange(0, axis_dim, 2, dtype=F32) / axis_dim)
    ang_r = row_ids[:, None] * inv_freq[None, :]
    ang_c = col_ids[:, None] * inv_freq[None, :]
    cos = jnp.concatenate([jnp.cos(ang_r)] * 2 + [jnp.cos(ang_c)] * 2, axis=1)
    sin = jnp.concatenate([-jnp.sin(ang_r), jnp.sin(ang_r), -jnp.sin(ang_c), jnp.sin(ang_c)], axis=1)
    return cos, sin


def _layer(x, xc, c, c_ctx, lower_bounds, w_ada, b_ada, gains, w_in, q_gain, k_gain, hg_gain, w_out,
           w_router, b_router, w_gate, b_gate, w_up, b_up, w_down, b_down):
    batch, seq, d = x.shape
    ctx = xc.shape[1]
    attn_width = d // 2
    hgrn_width = d // 2
    kv_heads = attn_width // HEAD_DIM // GQA_GROUP
    kv_width = kv_heads * HEAD_DIM
    head_cols = attn_width + 2 * hgrn_width
    n_experts = w_router.shape[1]
    m, mc = batch * seq, batch * ctx
    assert batch < MOD_ROWS and seq % ctx == 0

    cond = jnp.concatenate([c, c_ctx[None, :], jnp.zeros((MOD_ROWS - batch - 1, d), F32)], axis=0)
    mod = _ada_modulation(cond, w_ada, b_ada)
    sh1, sc1, g1, sh2, sc2, g2 = [mod[:, i * d:(i + 1) * d] for i in range(N_MOD)]

    x2 = x.reshape(m, d)
    hh = _norm_modulate(x2, xc.reshape(mc, d), gains[0], sh1, sc1, seq, batch)

    p_head = _project(hh, w_in, 0, head_cols, m, BF16)
    p_kv = _project(hh, w_in, head_cols, 2 * kv_width, m + mc, BF16)
    p_f = _project(hh, w_in, head_cols + 2 * kv_width, 2 * hgrn_width, m + mc, F32)
    p_v = _project(hh, w_in, head_cols + 2 * kv_width + 2 * hgrn_width, hgrn_width, m + mc, BF16)

    cos, sin = _rope_tables(seq)
    attn = _attention(p_head, p_kv, q_gain, k_gain, cos, sin, batch, seq, ctx, kv_heads)
    hgrn = _hgrn(p_head, p_f, p_v, lower_bounds, hg_gain, batch, seq, ctx, attn_width, hgrn_width)
    y = _out_projection(attn, hgrn, w_out)

    x1, h2, logits_t = _epilogue(x2, y, gains[1], gains[2], g1, sh2, sc2, w_router.T, b_router, seq)
    idx, wts, pos, cnt = _route(logits_t)

    row_tile = 256
    n_tiles = -(-(TOP_K * m + n_experts * (row_tile - 1)) // row_tile)
    slots, tile_expert, _, ends = _slots(idx, pos, cnt, row_tile, n_tiles)
    tile_expert = tile_expert[0, :n_tiles]
    n_used = ends[n_experts - 1, 0:1]

    tok_tile = _tile(seq, 128, 8)
    nt = m // tok_tile
    slot_tiles = slots.reshape(TOP_K, nt, tok_tile).transpose(1, 0, 2).reshape(nt, 1, TOP_K * tok_tile)
    tok_of_slot = _inverse_slots(slot_tiles, n_tiles * row_tile, tok_tile)
    xs = _dispatch(n_used, tok_of_slot, h2, row_tile)
    hmid = _expert_up(tile_expert, n_used, xs, w_gate, w_up, b_gate, b_up, row_tile)
    ye = _expert_down(tile_expert, n_used, hmid, w_down, b_down, row_tile, _tile(d, 2048, 256))
    out = _combine(slot_tiles, ye, wts.T, x1, gains[3], g2, seq, tok_tile)
    return out.reshape(batch, seq, d)


def kernel(x, c, ctx, c_ctx, w_ada, b_ada, norm_gains, w_in, q_norm_gain, k_norm_gain, hgrn_lb_logits,
           hgrn_norm_gain, w_out, w_router, b_router, w_gate, b_gate, w_up, b_up, w_down, b_down):
    assert w_ada.shape[0] == 1, "single-layer stack only"
    lower_bounds = jnp.cumsum(jax.nn.softmax(hgrn_lb_logits.astype(F32), axis=0), axis=0)
    return _layer(x, ctx, c, c_ctx, lower_bounds[0], w_ada[0], b_ada[0], norm_gains[0], w_in[0],
                  q_norm_gain[0], k_norm_gain[0], hgrn_norm_gain[0], w_out[0], w_router[0], b_router[0],
                  w_gate[0], b_gate[0], w_up[0], b_up[0], w_down[0], b_down[0])
```

```python
import functools

import jax
import jax.numpy as jnp
from jax import lax
from jax.experimental import pallas as pl
from jax.experimental.pallas import tpu as pltpu

F32 = jnp.float32
BF16 = jnp.bfloat16
U32 = jnp.uint32
I32 = jnp.int32

GRID_W = 64
N_MOD = 6
EPS = 1e-6
HEAD_DIM = 128
GQA_GROUP = 4
ROPE_THETA = 10000.0
LOG2_E = 1.4426950408889634
HGRN_EXPAND = 128
CHUNK = 64
TOP_K = 4
SWIGLU_LIMIT = 7.0
SWIGLU_ALPHA = 1.702
MOD_ROWS = 8

V7X_VMEM_LIMIT = 56 * 1024 * 1024


def _cp(*sem):
    return pltpu.CompilerParams(dimension_semantics=sem, vmem_limit_bytes=V7X_VMEM_LIMIT)


def _tile(n, pref, mult):
    if n <= pref:
        return n
    t = (pref // mult) * mult
    while t > mult and n % t:
        t -= mult
    assert n % t == 0, (n, pref, mult)
    return t


def _dot(a, b):
    return jnp.dot(a, b, preferred_element_type=F32)


def _dot_nt(a, b):
    return lax.dot_general(a, b, (((1,), (1,)), ((), ())), preferred_element_type=F32)


def _dot_tn(a, b):
    return lax.dot_general(a, b, (((0,), (0,)), ((), ())), preferred_element_type=F32)


def _silu(v):
    return v * jax.nn.sigmoid(v)


def _rms(v, gain):
    return v * lax.rsqrt(jnp.mean(v * v, axis=-1, keepdims=True) + EPS) * gain


def _ada_kernel(c_ref, w_ref, b_ref, o_ref):
    s = _silu(c_ref[...]).astype(BF16)
    o_ref[...] = _dot(s, w_ref[...].astype(BF16)) + b_ref[...]


def _ada_modulation(cond, w_ada, b_ada):
    rows, d = cond.shape
    n = w_ada.shape[1]
    tn = _tile(n, 512, 128)
    return pl.pallas_call(
        _ada_kernel,
        out_shape=jax.ShapeDtypeStruct((rows, n), F32),
        grid=(n // tn,),
        in_specs=[pl.BlockSpec((rows, d), lambda j: (0, 0)),
                  pl.BlockSpec((d, tn), lambda j: (0, j)),
                  pl.BlockSpec((1, tn), lambda j: (0, j))],
        out_specs=pl.BlockSpec((rows, tn), lambda j: (0, j)),
        compiler_params=_cp("arbitrary"),
        name="ada_modulation",
    )(cond, w_ada, b_ada.reshape(1, n))


def _norm_mod_kernel(nx, tiles_per_batch, ctx_row, x_ref, c_ref, g_ref, sh_ref, sc_ref, o_ref):
    i = pl.program_id(0)
    is_x = i < nx
    v = jnp.where(is_x, x_ref[...], c_ref[...])
    r = jnp.where(is_x, i // tiles_per_batch, ctx_row)
    shift = sh_ref[pl.ds(r, 1), :]
    scale = sc_ref[pl.ds(r, 1), :]
    o_ref[...] = (_rms(v, g_ref[...]) * (1.0 + scale) + shift).astype(o_ref.dtype)


def _norm_modulate(x2, c2, gain, shift_tab, scale_tab, seq, ctx_row):
    m, d = x2.shape
    mc = c2.shape[0]
    tm = _tile(_gcd(seq, mc), 256, 16)
    nx, ncx = m // tm, mc // tm
    return pl.pallas_call(
        functools.partial(_norm_mod_kernel, nx, seq // tm, ctx_row),
        out_shape=jax.ShapeDtypeStruct((m + mc, d), BF16),
        grid=(nx + ncx,),
        in_specs=[pl.BlockSpec((tm, d), lambda i: (jnp.minimum(i, nx - 1), 0)),
                  pl.BlockSpec((tm, d), lambda i: (jnp.maximum(i - nx, 0), 0)),
                  pl.BlockSpec((1, d), lambda i: (0, 0)),
                  pl.BlockSpec((MOD_ROWS, d), lambda i: (0, 0)),
                  pl.BlockSpec((MOD_ROWS, d), lambda i: (0, 0))],
        out_specs=pl.BlockSpec((tm, d), lambda i: (i, 0)),
        compiler_params=_cp("arbitrary"),
        name="norm_modulate",
    )(x2, c2, gain.reshape(1, d), shift_tab, scale_tab)


def _gcd(a, b):
    while b:
        a, b = b, a % b
    return a


def _mm_kernel(a_ref, w_ref, o_ref):
    o_ref[...] = _dot(a_ref[...], w_ref[...].astype(BF16)).astype(o_ref.dtype)


def _project(a, w, col0, ncols, m_rows, out_dtype):
    k = a.shape[1]
    tn = _tile(_gcd(ncols, col0) if col0 else ncols, 1024, 128)
    tm = _tile(m_rows, 512, 16)
    jb = col0 // tn
    return pl.pallas_call(
        _mm_kernel,
        out_shape=jax.ShapeDtypeStruct((m_rows, ncols), out_dtype),
        grid=(ncols // tn, m_rows // tm),
        in_specs=[pl.BlockSpec((tm, k), lambda j, i: (i, 0)),
                  pl.BlockSpec((k, tn), lambda j, i: (0, jb + j))],
        out_specs=pl.BlockSpec((tm, tn), lambda j, i: (i, j)),
        compiler_params=_cp("arbitrary", "arbitrary"),
        name="in_projection",
    )(a, w)


def _mm2_kernel(ka, a1_ref, a2_ref, w_ref, o_ref):
    w = w_ref[...].astype(BF16)
    o_ref[...] = _dot(a1_ref[...], w[:ka]) + _dot(a2_ref[...], w[ka:])


def _out_projection(a1, a2, w):
    m, ka = a1.shape
    kb = a2.shape[1]
    n = w.shape[1]
    tn = _tile(n, 1024, 128)
    tm = _tile(m, 512, 16)
    return pl.pallas_call(
        functools.partial(_mm2_kernel, ka),
        out_shape=jax.ShapeDtypeStruct((m, n), F32),
        grid=(n // tn, m // tm),
        in_specs=[pl.BlockSpec((tm, ka), lambda j, i: (i, 0)),
                  pl.BlockSpec((tm, kb), lambda j, i: (i, 0)),
                  pl.BlockSpec((ka + kb, tn), lambda j, i: (0, j))],
        out_specs=pl.BlockSpec((tm, tn), lambda j, i: (i, j)),
        compiler_params=_cp("arbitrary", "arbitrary"),
        name="out_projection",
    )(a1, a2, w)


def _rope(v, cos, sin_signed):
    lane = lax.broadcasted_iota(I32, v.shape, 1)
    partner = jnp.where((lane & 32) == 0, pltpu.roll(v, HEAD_DIM - 32, 1), pltpu.roll(v, 32, 1))
    return v * cos + partner * sin_signed


def _attn_kernel(tq, seq, ctx, q_ref, kl_ref, kc_ref, vl_ref, vc_ref, qg_ref, kg_ref, cos_ref, sin_ref,
                 o_ref, k_sc, v_sc):
    qi = pl.program_id(2)

    @pl.when(qi == 0)
    def _():
        kg = kg_ref[...]
        k_sc[0:ctx, :] = _rms(kc_ref[...].astype(F32), kg).astype(BF16)
        kl = _rope(_rms(kl_ref[...].astype(F32), kg), cos_ref[...], sin_ref[...])
        k_sc[ctx:ctx + seq, :] = kl.astype(BF16)
        v_sc[0:ctx, :] = vc_ref[...]
        v_sc[ctx:ctx + seq, :] = vl_ref[...]

    r0 = pl.multiple_of(qi * tq, tq)
    cos = cos_ref[pl.ds(r0, tq), :]
    sin = sin_ref[pl.ds(r0, tq), :]
    qg = qg_ref[...] * (HEAD_DIM ** -0.5 * LOG2_E)
    keys = k_sc[...]
    vals = v_sc[...]
    heads = [slice(h * HEAD_DIM, (h + 1) * HEAD_DIM) for h in range(GQA_GROUP)]
    scores = [_dot_nt(_rope(_rms(q_ref[:, sl].astype(F32), qg), cos, sin).astype(BF16), keys) for sl in heads]
    probs = [jnp.exp2(s - jnp.max(s, axis=-1, keepdims=True)) for s in scores]
    denoms = [jnp.sum(p, axis=-1, keepdims=True) for p in probs]
    outs = [_dot(p.astype(BF16), vals) for p in probs]
    for sl, o, denom in zip(heads, outs, denoms):
        o_ref[:, sl] = (o / denom).astype(o_ref.dtype)


def _attention(p_head, p_kv, q_gain, k_gain, cos, sin_signed, batch, seq, ctx, kv_heads):
    tq = _tile(seq, 256, 16)
    nq = seq // tq
    gw = GQA_GROUP * HEAD_DIM
    ctx_blk0 = batch * seq // ctx
    return pl.pallas_call(
        functools.partial(_attn_kernel, tq, seq, ctx),
        out_shape=jax.ShapeDtypeStruct((batch * seq, kv_heads * gw), BF16),
        grid=(batch, kv_heads, nq),
        in_specs=[pl.BlockSpec((tq, gw), lambda b, h, q: (b * nq + q, h)),
                  pl.BlockSpec((seq, HEAD_DIM), lambda b, h, q: (b, h)),
                  pl.BlockSpec((ctx, HEAD_DIM), lambda b, h, q: (ctx_blk0 + b, h)),
                  pl.BlockSpec((seq, HEAD_DIM), lambda b, h, q: (b, kv_heads + h)),
                  pl.BlockSpec((ctx, HEAD_DIM), lambda b, h, q: (ctx_blk0 + b, kv_heads + h)),
                  pl.BlockSpec((1, HEAD_DIM), lambda b, h, q: (0, 0)),
                  pl.BlockSpec((1, HEAD_DIM), lambda b, h, q: (0, 0)),
                  pl.BlockSpec((seq, HEAD_DIM), lambda b, h, q: (0, 0)),
                  pl.BlockSpec((seq, HEAD_DIM), lambda b, h, q: (0, 0))],
        out_specs=pl.BlockSpec((tq, gw), lambda b, h, q: (b * nq + q, h)),
        scratch_shapes=[pltpu.VMEM((ctx + seq, HEAD_DIM), BF16),
                        pltpu.VMEM((ctx + seq, HEAD_DIM), BF16)],
        compiler_params=_cp("arbitrary", "arbitrary", "arbitrary"),
        name="gqa_attention",
    )(p_head, p_kv, p_kv, p_kv, p_kv, q_gain.reshape(1, HEAD_DIM), k_gain.reshape(1, HEAD_DIM), cos, sin_signed)


def _hgrn_kernel(hpb, seq, ctx, rows, lb_ref, gain_ref, q_ref, g_ref, ff_ref, fb_ref, v_ref,
                 cff_ref, cfb_ref, cv_ref, o_ref, st_ref, of_ref, ob_ref):
    cpb = rows // CHUNK
    lg_chunk = CHUNK.bit_length() - 1
    row = lax.broadcasted_iota(I32, (rows, rows), 0)
    col = lax.broadcasted_iota(I32, (rows, rows), 1)
    same = (row >> lg_chunk) == (col >> lg_chunk)
    masks = (same & (col <= row), same & (col >= row))
    tris = tuple(jnp.where(m, 1.0, 0.0).astype(BF16) for m in masks)
    row_chunk = lax.broadcasted_iota(I32, (rows, HGRN_EXPAND), 0) >> lg_chunk

    st_ref[...] = jnp.zeros(st_ref.shape, F32)

    def lane_stack(a):
        return jnp.concatenate([jnp.where(row_chunk == c, a, 0.0).astype(BF16) for c in range(cpb)], axis=1)

    def chunk_rows(b, off):
        return jnp.concatenate([jnp.broadcast_to(b[c * CHUNK + off:c * CHUNK + off + 1, :], (CHUNK, HGRN_EXPAND))
                                for c in range(cpb)], axis=0)

    def blocks(items, with_out):
        n = len(items)
        last = [0 if it[5] else CHUNK - 1 for it in items]
        mid = [CHUNK // 2 if it[5] else CHUNK // 2 - 1 for it in items]
        ks, splits = [], []
        for _, raw, _, _, lb, _ in items:
            f = lb + (1.0 - lb) * jax.nn.sigmoid(raw)
            logf = jnp.log(f)
            hi = logf.astype(BF16)
            ks.append(1.0 - f)
            splits.append((hi, (logf - hi.astype(F32)).astype(BF16)))
        bs = [_dot(tris[items[j][5]], splits[j][0]) + _dot(tris[items[j][5]], splits[j][1]) for j in range(n)]
        uts = [_dot_tn(items[j][2], lane_stack(ks[j] * jnp.exp(chunk_rows(bs[j], last[j]) - bs[j]))) for j in range(n)]
        slabs = []
        for j in range(n):
            st = st_ref[items[j][0]]
            starts = [None] * cpb
            for c in (reversed(range(cpb)) if items[j][5] else range(cpb)):
                starts[c] = st.astype(BF16)
                st = (st * jnp.exp(bs[j][c * CHUNK + last[j]:c * CHUNK + last[j] + 1, :])
                      + uts[j][:, c * HGRN_EXPAND:(c + 1) * HGRN_EXPAND])
            st_ref[items[j][0]] = st
            slabs.append(jnp.concatenate(starts, axis=1))
        if not with_out:
            return None
        scores = []
        for j in range(n):
            b_mid = chunk_rows(bs[j], mid[j])
            qt = (items[j][3] * jnp.exp(bs[j] - b_mid)).astype(BF16)
            kt = (ks[j] * jnp.exp(b_mid - bs[j])).astype(BF16)
            scores.append(_dot_nt(qt, kt))
        outs = []
        for j in range(n):
            a = jnp.where(masks[items[j][5]], scores[j], 0.0).astype(BF16)
            inter = _dot_nt(lane_stack(items[j][3] * jnp.exp(bs[j])), slabs[j])
            outs.append(_dot(a, items[j][2]) + inter)
        return outs

    def ctx_body(i, carry):
        r_f = pl.multiple_of(i * rows, rows)
        r_b = pl.multiple_of((ctx // rows - 1 - i) * rows, rows)
        items = []
        for g in range(hpb):
            sl = slice(g * HGRN_EXPAND, (g + 1) * HGRN_EXPAND)
            items.append((2 * g, cff_ref[pl.ds(r_f, rows), sl], cv_ref[pl.ds(r_f, rows), sl], None, lb_ref[0:1, sl], 0))
            items.append((2 * g + 1, cfb_ref[pl.ds(r_b, rows), sl], cv_ref[pl.ds(r_b, rows), sl], None,
                          lb_ref[1:2, sl], 1))
        blocks(items, False)
        return carry

    lax.fori_loop(0, ctx // rows, ctx_body, 0)

    def seq_body(i, carry):
        r_f = pl.multiple_of(i * rows, rows)
        r_b = pl.multiple_of((seq // rows - 1 - i) * rows, rows)
        items, dests = [], []
        for g in range(hpb):
            sl = slice(g * HGRN_EXPAND, (g + 1) * HGRN_EXPAND)
            for rev, r, f_ref, out_ref in ((0, r_f, ff_ref, of_ref), (1, r_b, fb_ref, ob_ref)):
                q = _silu(q_ref[pl.ds(r, rows), sl].astype(F32))
                items.append((2 * g + rev, f_ref[pl.ds(r, rows), sl], v_ref[pl.ds(r, rows), sl], q,
                              lb_ref[rev:rev + 1, sl], rev))
                dests.append((out_ref, r, sl))
        for (out_ref, r, sl), o in zip(dests, blocks(items, True)):
            out_ref[pl.ds(r, rows), sl] = o
        return carry

    lax.fori_loop(0, seq // rows, seq_body, 0)

    rb = _tile(seq, 256, 8)

    def read_body(i, carry):
        r = pl.multiple_of(i * rb, rb)
        for g in range(hpb):
            sl = slice(g * HGRN_EXPAND, (g + 1) * HGRN_EXPAND)
            o = of_ref[pl.ds(r, rb), sl] + ob_ref[pl.ds(r, rb), sl]
            y = _rms(o, gain_ref[...]) * _silu(g_ref[pl.ds(r, rb), sl].astype(F32))
            o_ref[pl.ds(r, rb), sl] = y.astype(o_ref.dtype)
        return carry

    lax.fori_loop(0, seq // rb, read_body, 0)


def _hgrn(p_head, p_f, p_v, lower_bounds, hg_gain, batch, seq, ctx, attn_width, hgrn_width):
    heads = hgrn_width // HGRN_EXPAND
    hpb = 2 if heads % 2 == 0 else 1
    bw = hpb * HGRN_EXPAND
    nh = heads // hpb
    q0, g0 = attn_width // bw, (attn_width + hgrn_width) // bw
    ctx_blk0 = batch * seq // ctx
    rows = _tile(_gcd(seq, ctx), 256, CHUNK)
    lat = lambda c0: pl.BlockSpec((seq, bw), lambda b, h: (b, c0 + h))
    cx = lambda c0: pl.BlockSpec((ctx, bw), lambda b, h: (ctx_blk0 + b, c0 + h))
    return pl.pallas_call(
        functools.partial(_hgrn_kernel, hpb, seq, ctx, rows),
        out_shape=jax.ShapeDtypeStruct((batch * seq, hgrn_width), BF16),
        grid=(batch, nh),
        in_specs=[pl.BlockSpec((2, bw), lambda b, h: (0, h)),
                  pl.BlockSpec((1, HGRN_EXPAND), lambda b, h: (0, 0)),
                  lat(q0), lat(g0), lat(0), lat(nh), lat(0), cx(0), cx(nh), cx(0)],
        out_specs=pl.BlockSpec((seq, bw), lambda b, h: (b, h)),
        scratch_shapes=[pltpu.VMEM((2 * hpb, HGRN_EXPAND, HGRN_EXPAND), F32),
                        pltpu.VMEM((seq, bw), F32),
                        pltpu.VMEM((seq, bw), F32)],
        compiler_params=_cp("arbitrary", "arbitrary"),
        name="hgrn2_scan",
    )(lower_bounds, hg_gain.reshape(1, HGRN_EXPAND), p_head, p_head, p_f, p_f, p_v, p_f, p_f, p_v)


def _epilogue_kernel(tiles_per_batch, x_ref, y_ref, g1n_ref, g2n_ref, gate_ref, sh_ref, sc_ref, wr_ref, br_ref,
                     x1_ref, hp_ref, lg_ref):
    r = pl.program_id(0) // tiles_per_batch
    x1 = x_ref[...] + gate_ref[pl.ds(r, 1), :] * _rms(y_ref[...], g1n_ref[...])
    x1_ref[...] = x1
    h = _rms(x1, g2n_ref[...]) * (1.0 + sc_ref[pl.ds(r, 1), :]) + sh_ref[pl.ds(r, 1), :]
    hp_ref[...] = h
    w = wr_ref[...]
    w_hi = w.astype(BF16)
    w_lo = (w - w_hi.astype(F32)).astype(BF16)
    h_hi = h.astype(BF16)
    h_lo = (h - h_hi.astype(F32)).astype(BF16)
    lg_ref[...] = _dot_nt(w_hi, h_hi) + _dot_nt(w_hi, h_lo) + _dot_nt(w_lo, h_hi) + br_ref[...]


def _epilogue(x2, y, gain1, gain2, gate_tab, shift_tab, scale_tab, w_router_t, b_router, seq):
    m, d = x2.shape
    ne = w_router_t.shape[0]
    tm = _tile(seq, 256, 128)
    row = lambda: pl.BlockSpec((tm, d), lambda i: (i, 0))
    tab = lambda: pl.BlockSpec((MOD_ROWS, d), lambda i: (0, 0))
    vec = lambda: pl.BlockSpec((1, d), lambda i: (0, 0))
    return pl.pallas_call(
        functools.partial(_epilogue_kernel, seq // tm),
        out_shape=(jax.ShapeDtypeStruct((m, d), F32),
                   jax.ShapeDtypeStruct((m, d), F32),
                   jax.ShapeDtypeStruct((ne, m), F32)),
        grid=(m // tm,),
        in_specs=[row(), row(), vec(), vec(), tab(), tab(), tab(),
                  pl.BlockSpec((ne, d), lambda i: (0, 0)),
                  pl.BlockSpec((ne, 1), lambda i: (0, 0))],
        out_specs=(row(), row(), pl.BlockSpec((ne, tm), lambda i: (0, i))),
        compiler_params=_cp("arbitrary"),
        name="residual_norm_router",
    )(x2, y, gain1.reshape(1, d), gain2.reshape(1, d), gate_tab, shift_tab, scale_tab,
      w_router_t, b_router.reshape(ne, 1))


def _route_kernel(lg_ref, idx_ref, wt_ref, pos_ref, cnt_ref, carry_ref):
    i = pl.program_id(0)
    ne, tn = lg_ref.shape

    @pl.when(i == 0)
    def _():
        carry_ref[...] = jnp.zeros(carry_ref.shape, F32)

    lg = lg_ref[...]
    eidx = lax.broadcasted_iota(I32, (ne, tn), 0)
    vals, hots = [], []
    for k in range(TOP_K):
        m = jnp.max(lg, axis=0, keepdims=True)
        sel = jnp.min(jnp.where(lg == m, eidx, ne), axis=0, keepdims=True)
        hot = eidx == sel
        idx_ref[k:k + 1, :] = sel
        vals.append(m)
        hots.append(hot)
        lg = jnp.where(hot, -jnp.inf, lg)
    exps = [jnp.exp(v - vals[0]) for v in vals]
    denom = exps[0] + exps[1] + exps[2] + exps[3]
    for k in range(TOP_K):
        wt_ref[k:k + 1, :] = exps[k] / denom

    member = hots[0] | hots[1] | hots[2] | hots[3]
    t_row = lax.broadcasted_iota(I32, (tn, tn), 0)
    t_col = lax.broadcasted_iota(I32, (tn, tn), 1)
    before = (t_row < t_col).astype(BF16)
    rank = _dot(member.astype(BF16), before) + carry_ref[:, 0:1]
    for k in range(TOP_K):
        pos_ref[k:k + 1, :] = jnp.sum(jnp.where(hots[k], rank, 0.0), axis=0, keepdims=True).astype(I32)
    carry_ref[...] = carry_ref[...] + jnp.sum(member.astype(F32), axis=1, keepdims=True)
    cnt_ref[...] = carry_ref[...].astype(I32)


def _route(logits_t):
    ne, m = logits_t.shape
    tn = _tile(m, 512, 128)
    out4 = lambda: pl.BlockSpec((TOP_K, tn), lambda i: (0, i))
    return pl.pallas_call(
        _route_kernel,
        out_shape=(jax.ShapeDtypeStruct((TOP_K, m), I32),
                   jax.ShapeDtypeStruct((TOP_K, m), F32),
                   jax.ShapeDtypeStruct((TOP_K, m), I32),
                   jax.ShapeDtypeStruct((ne, 128), I32)),
        grid=(m // tn,),
        in_specs=[pl.BlockSpec((ne, tn), lambda i: (0, i))],
        out_specs=(out4(), out4(), out4(), pl.BlockSpec((ne, 128), lambda i: (0, 0))),
        scratch_shapes=[pltpu.VMEM((ne, 128), F32)],
        compiler_params=_cp("arbitrary"),
        name="top4_route",
    )(logits_t)


def _slots_kernel(row_tile, lg_tile, idx_ref, pos_ref, cnt_ref, slot_ref, te_ref, start_ref, ends_ref):
    ne = cnt_ref.shape[0]
    tn = idx_ref.shape[1]
    tiles = (cnt_ref[...] + (row_tile - 1)) >> lg_tile
    e_row = lax.broadcasted_iota(I32, (ne, ne), 0)
    e_col = lax.broadcasted_iota(I32, (ne, ne), 1)
    upto = jnp.where(e_col <= e_row, 1.0, 0.0).astype(BF16)
    ends = _dot(upto, tiles.astype(F32).astype(BF16)).astype(I32)
    start = (ends - tiles) << lg_tile
    start_ref[...] = start
    ends_ref[...] = ends
    eidx = lax.broadcasted_iota(I32, (ne, tn), 0)
    start_col = start[:, 0:1]
    for k in range(TOP_K):
        hit = eidx == idx_ref[k:k + 1, :]
        slot_ref[k:k + 1, :] = jnp.sum(jnp.where(hit, start_col, 0), axis=0, keepdims=True) + pos_ref[k:k + 1, :]
    tile_i = lax.broadcasted_iota(I32, (ne, te_ref.shape[1]), 1)
    te = jnp.sum(jnp.where(tile_i >= ends[:, 0:1], 1, 0), axis=0, keepdims=True)
    te_ref[...] = jnp.minimum(te, ne - 1)


def _slots(idx, pos, cnt, row_tile, n_tiles):
    ne = cnt.shape[0]
    m = idx.shape[1]
    tn = _tile(m, 2048, 128)
    ntp = -(-n_tiles // 128) * 128
    lg_tile = row_tile.bit_length() - 1
    assert 1 << lg_tile == row_tile and -(-m // row_tile) <= 256
    io4 = lambda: pl.BlockSpec((TOP_K, tn), lambda i: (0, i))
    small = lambda w: pl.BlockSpec((ne, w), lambda i: (0, 0))
    return pl.pallas_call(
        functools.partial(_slots_kernel, row_tile, lg_tile),
        out_shape=(jax.ShapeDtypeStruct((TOP_K, m), I32),
                   jax.ShapeDtypeStruct((1, ntp), I32),
                   jax.ShapeDtypeStruct((ne, 128), I32),
                   jax.ShapeDtypeStruct((ne, 128), I32)),
        grid=(m // tn,),
        in_specs=[io4(), io4(), small(128)],
        out_specs=(io4(), pl.BlockSpec((1, ntp), lambda i: (0, 0)), small(128), small(128)),
        compiler_params=_cp("arbitrary"),
        name="route_slots",
    )(idx, pos, cnt)


DMA_ISSUE_UNROLL = 8


def _inverse_kernel(tm, n_slots, slot_ref, tok_ref):
    i = pl.program_id(0)

    @pl.when(i == 0)
    def _():
        def clear(g, c):
            for u in range(DMA_ISSUE_UNROLL):
                tok_ref[g * DMA_ISSUE_UNROLL + u] = 0
            return c

        lax.fori_loop(0, n_slots // DMA_ISSUE_UNROLL, clear, 0)

    def body(g, c):
        for u in range(DMA_ISSUE_UNROLL):
            r = g * DMA_ISSUE_UNROLL + u
            for k in range(TOP_K):
                tok_ref[slot_ref[0, 0, k * tm + r]] = i * tm + r
        return c

    lax.fori_loop(0, tm // DMA_ISSUE_UNROLL, body, 0)


def _inverse_slots(slot_tiles, n_slots, tm):
    nt = slot_tiles.shape[0]
    return pl.pallas_call(
        functools.partial(_inverse_kernel, tm, n_slots),
        out_shape=jax.ShapeDtypeStruct((n_slots,), I32),
        grid=(nt,),
        in_specs=[pl.BlockSpec((1, 1, TOP_K * tm), lambda i: (i, 0, 0), memory_space=pltpu.SMEM)],
        out_specs=pl.BlockSpec(memory_space=pltpu.SMEM),
        compiler_params=_cp("arbitrary"),
        name="route_inverse",
    )(slot_tiles)


def _dispatch_kernel(tm, nu_ref, tok_ref, next_ref, src_hbm, o_ref, buf, sem):
    i = pl.program_id(0)
    n_used = nu_ref[0]

    def issue(t_ref, b):
        def body(g, c):
            for u in range(DMA_ISSUE_UNROLL):
                r = g * DMA_ISSUE_UNROLL + u
                pltpu.make_async_copy(src_hbm.at[pl.ds(t_ref[0, 0, r], 1)], buf.at[b, pl.ds(r, 1)],
                                      sem.at[b]).start(priority=u % 2)
            return c

        lax.fori_loop(0, tm // DMA_ISSUE_UNROLL, body, 0)

    @pl.when(i == 0)
    def _():
        issue(tok_ref, 0)

    @pl.when(i + 1 < n_used)
    def _():
        issue(next_ref, (i + 1) % 2)

    @pl.when(i < n_used)
    def _():
        b = i % 2
        pltpu.make_async_copy(src_hbm.at[pl.ds(0, tm)], buf.at[b], sem.at[b]).wait()
        o_ref[...] = buf[b].astype(o_ref.dtype)

    @pl.when(i >= n_used)
    def _():
        o_ref[...] = jnp.zeros(o_ref.shape, o_ref.dtype)


def _dispatch(n_used, tok_of_slot, src, tm):
    w = src.shape[1]
    nt = tok_of_slot.shape[0] // tm
    tok_tiles = tok_of_slot.reshape(nt, 1, tm)
    tok_spec = lambda f: pl.BlockSpec((1, 1, tm), f, memory_space=pltpu.SMEM)
    return pl.pallas_call(
        functools.partial(_dispatch_kernel, tm),
        out_shape=jax.ShapeDtypeStruct((nt * tm, w), BF16),
        grid_spec=pltpu.PrefetchScalarGridSpec(
            num_scalar_prefetch=1,
            grid=(nt,),
            in_specs=[tok_spec(lambda i, nu: (i, 0, 0)),
                      tok_spec(lambda i, nu: (jnp.minimum(i + 1, nt - 1), 0, 0)),
                      pl.BlockSpec(memory_space=pl.ANY)],
            out_specs=pl.BlockSpec((tm, w), lambda i, nu: (i, 0)),
            scratch_shapes=[pltpu.VMEM((2, tm, w), src.dtype), pltpu.SemaphoreType.DMA((2,))]),
        compiler_params=_cp("arbitrary"),
        name="moe_dispatch",
    )(n_used, tok_tiles, tok_tiles, src)


def _stream_expert_weights(n_pass, n_tiles, width, te_ref, nu_ref, grp_ref, sem, weights):
    c, i = pl.program_id(0), pl.program_id(1)
    n_used = nu_ref[0]

    def copies(e, cc, slot):
        col = pl.multiple_of(cc * width, width)
        return [pltpu.make_async_copy(hbm.at[e, :, pl.ds(col, width)], buf.at[slot], sem.at[j, slot])
                for j, (hbm, buf) in enumerate(weights)]

    @pl.when((c == 0) & (i == 0))
    def _():
        grp_ref[0] = 0
        for cp in copies(te_ref[0], 0, 0):
            cp.start()

    e = te_ref[i]
    first = (i < n_used) & ((i == 0) | (e != te_ref[jnp.maximum(i - 1, 0)]))

    @pl.when(first)
    def _():
        grp = grp_ref[0]
        slot = grp % 2
        for cp in copies(e, c, slot):
            cp.wait()
        nxt = lax.while_loop(lambda j: (j < n_used) & (te_ref[jnp.minimum(j, n_tiles - 1)] == e),
                             lambda j: j + 1, i + 1)
        more = nxt < n_used

        @pl.when(more)
        def _():
            for cp in copies(te_ref[jnp.minimum(nxt, n_tiles - 1)], c, 1 - slot):
                cp.start(priority=1)

        @pl.when(jnp.logical_not(more) & (c + 1 < n_pass))
        def _():
            for cp in copies(te_ref[0], c + 1, 1 - slot):
                cp.start(priority=1)

        grp_ref[0] = grp + 1

    return (grp_ref[0] + 1) % 2


def _expert_up_kernel(n_pass, n_tiles, te_ref, nu_ref, x_ref, wg_hbm, wu_hbm, bg_ref, bu_ref, h_ref,
                      wg_buf, wu_buf, sem, grp_ref):
    slot = _stream_expert_weights(n_pass, n_tiles, h_ref.shape[1], te_ref, nu_ref, grp_ref, sem,
                                  [(wg_hbm, wg_buf), (wu_hbm, wu_buf)])

    @pl.when(pl.program_id(1) < nu_ref[0])
    def _():
        x = x_ref[...]

        def proj(w_buf, b_ref):
            return _dot(x, w_buf[slot].astype(BF16)) + b_ref[0]

        g = jnp.minimum(proj(wg_buf, bg_ref), SWIGLU_LIMIT)
        u = jnp.clip(proj(wu_buf, bu_ref), -SWIGLU_LIMIT, SWIGLU_LIMIT)
        h_ref[...] = (g * jax.nn.sigmoid(SWIGLU_ALPHA * g) * (u + 1.0)).astype(h_ref.dtype)

    @pl.when(pl.program_id(1) >= nu_ref[0])
    def _():
        h_ref[...] = jnp.zeros(h_ref.shape, h_ref.dtype)


def _used_tile(i, nu):
    return jnp.minimum(i, nu[0] - 1)


def _expert_up(tile_expert, n_used, xs, w_gate, w_up, b_gate, b_up, tm):
    s_pad = xs.shape[0]
    ne, d, f = w_gate.shape
    fc = _tile(f, 512, 128)
    nt = s_pad // tm
    wspec = lambda: pl.BlockSpec(memory_space=pl.ANY)
    bspec = lambda: pl.BlockSpec((1, 1, fc), lambda c, i, te, nu: (te[i], 0, c))
    return pl.pallas_call(
        functools.partial(_expert_up_kernel, f // fc, nt),
        out_shape=jax.ShapeDtypeStruct((s_pad, f), BF16),
        grid_spec=pltpu.PrefetchScalarGridSpec(
            num_scalar_prefetch=2,
            grid=(f // fc, nt),
            in_specs=[pl.BlockSpec((tm, d), lambda c, i, te, nu: (_used_tile(i, nu), 0)),
                      wspec(), wspec(), bspec(), bspec()],
            out_specs=pl.BlockSpec((tm, fc), lambda c, i, te, nu: (i, c)),
            scratch_shapes=[pltpu.VMEM((2, d, fc), F32), pltpu.VMEM((2, d, fc), F32),
                            pltpu.SemaphoreType.DMA((2, 2)), pltpu.SMEM((1,), I32)]),
        compiler_params=_cp("arbitrary", "arbitrary"),
        name="expert_gate_up",
    )(tile_expert, n_used, xs, w_gate, w_up, b_gate.reshape(ne, 1, f), b_up.reshape(ne, 1, f))


def _expert_down_kernel(n_pass, n_tiles, te_ref, nu_ref, h_ref, wd_hbm, bd_ref, y_ref, wd_buf, sem, grp_ref):
    slot = _stream_expert_weights(n_pass, n_tiles, y_ref.shape[1], te_ref, nu_ref, grp_ref, sem, [(wd_hbm, wd_buf)])

    @pl.when(pl.program_id(1) < nu_ref[0])
    def _():
        y_ref[...] = _dot(h_ref[...], wd_buf[slot].astype(BF16)) + bd_ref[0]

    @pl.when(pl.program_id(1) >= nu_ref[0])
    def _():
        y_ref[...] = jnp.zeros(y_ref.shape, y_ref.dtype)


def _expert_down(tile_expert, n_used, h, w_down, b_down, tm, tn):
    s_pad, f = h.shape
    ne, _, d = w_down.shape
    nt = s_pad // tm
    return pl.pallas_call(
        functools.partial(_expert_down_kernel, d // tn, nt),
        out_shape=jax.ShapeDtypeStruct((s_pad, d), F32),
        grid_spec=pltpu.PrefetchScalarGridSpec(
            num_scalar_prefetch=2,
            grid=(d // tn, nt),
            in_specs=[pl.BlockSpec((tm, f), lambda c, i, te, nu: (_used_tile(i, nu), 0)),
                      pl.BlockSpec(memory_space=pl.ANY),
                      pl.BlockSpec((1, 1, tn), lambda c, i, te, nu: (te[i], 0, c))],
            out_specs=pl.BlockSpec((tm, tn), lambda c, i, te, nu: (i, c)),
            scratch_shapes=[pltpu.VMEM((2, f, tn), F32), pltpu.SemaphoreType.DMA((1, 2)),
                            pltpu.SMEM((1,), I32)]),
        compiler_params=_cp("arbitrary", "arbitrary"),
        name="expert_down",
    )(tile_expert, n_used, h, w_down, b_down.reshape(ne, 1, d))


def _combine_kernel(tm, tiles_per_batch, nt, slot_ref, next_ref, yp_hbm, wt_ref, x1_ref, gn_ref, gate_ref,
                    o_ref, buf, sem):
    i = pl.program_id(0)

    def issue(s_ref, b):
        def body(g, c):
            for u in range(DMA_ISSUE_UNROLL):
                r = g * DMA_ISSUE_UNROLL + u
                for k in range(TOP_K):
                    pltpu.make_async_copy(yp_hbm.at[pl.ds(s_ref[0, 0, k * tm + r], 1)],
                                          buf.at[b, k, pl.ds(r, 1)], sem.at[b]).start(priority=k % 2)
            return c

        lax.fori_loop(0, tm // DMA_ISSUE_UNROLL, body, 0)

    @pl.when(i == 0)
    def _():
        issue(slot_ref, 0)

    @pl.when(i + 1 < nt)
    def _():
        issue(next_ref, (i + 1) % 2)

    b = i % 2
    for k in range(TOP_K):
        pltpu.make_async_copy(yp_hbm.at[pl.ds(0, tm)], buf.at[b, k], sem.at[b]).wait()

    wt = wt_ref[...]
    moe = wt[:, 0:1] * buf[b, 0]
    for k in range(1, TOP_K):
        moe = moe + wt[:, k:k + 1] * buf[b, k]
    r = i // tiles_per_batch
    o_ref[...] = x1_ref[...] + gate_ref[pl.ds(r, 1), :] * _rms(moe, gn_ref[...])


def _combine(slot_tiles, yp, wts, x1, gain, gate_tab, seq, tm):
    m, d = x1.shape
    nt = m // tm
    slot_spec = lambda f: pl.BlockSpec((1, 1, TOP_K * tm), f, memory_space=pltpu.SMEM)
    return pl.pallas_call(
        functools.partial(_combine_kernel, tm, seq // tm, nt),
        out_shape=jax.ShapeDtypeStruct((m, d), F32),
        grid=(nt,),
        in_specs=[slot_spec(lambda i: (i, 0, 0)),
                  slot_spec(lambda i: (jnp.minimum(i + 1, nt - 1), 0, 0)),
                  pl.BlockSpec(memory_space=pl.ANY),
                  pl.BlockSpec((tm, TOP_K), lambda i: (i, 0)),
                  pl.BlockSpec((tm, d), lambda i: (i, 0)),
                  pl.BlockSpec((1, d), lambda i: (0, 0)),
                  pl.BlockSpec((MOD_ROWS, d), lambda i: (0, 0))],
        out_specs=pl.BlockSpec((tm, d), lambda i: (i, 0)),
        scratch_shapes=[pltpu.VMEM((2, TOP_K, tm, d), F32), pltpu.SemaphoreType.DMA((2,))],
        compiler_params=_cp("arbitrary"),
        name="moe_combine",
    )(slot_tiles, slot_tiles, yp, wts, x1, gain.reshape(1, d), gate_tab)


def _rope_tables(seq):
    rows = seq // GRID_W
    axis_dim = HEAD_DIM // 2
    row_ids = jnp.repeat(jnp.arange(rows), GRID_W).astype(F32)
    col_ids = jnp.tile(jnp.arange(GRID_W), rows).astype(F32)
    inv_freq = ROPE_THETA ** (-jnp.arange(0, axis_dim, 2, dtype=F32) / axis_dim)
    ang_r = row_ids[:, None] * inv_freq[None, :]
    ang_c = col_ids[:, None] * inv_freq[None, :]
    cos = jnp.concatenate([jnp.cos(ang_r)] * 2 + [jnp.cos(ang_c)] * 2, axis=1)
    sin = jnp.concatenate([-jnp.sin(ang_r), jnp.sin(ang_r), -jnp.sin(ang_c), jnp.sin(ang_c)], axis=1)
    return cos, sin


def _layer(x, xc, c, c_ctx, lower_bounds, w_ada, b_ada, gains, w_in, q_gain, k_gain, hg_gain, w_out,
           w_router, b_router, w_gate, b_gate, w_up, b_up, w_down, b_down):
    batch, seq, d = x.shape
    ctx = xc.shape[1]
    attn_width = d // 2
    hgrn_width = d // 2
    kv_heads = attn_width // HEAD_DIM // GQA_GROUP
    kv_width = kv_heads * HEAD_DIM
    head_cols = attn_width + 2 * hgrn_width
    n_experts = w_router.shape[1]
    m, mc = batch * seq, batch * ctx
    assert batch < MOD_ROWS and seq % ctx == 0

    cond = jnp.concatenate([c, c_ctx[None, :], jnp.zeros((MOD_ROWS - batch - 1, d), F32)], axis=0)
    mod = _ada_modulation(cond, w_ada, b_ada)
    sh1, sc1, g1, sh2, sc2, g2 = [mod[:, i * d:(i + 1) * d] for i in range(N_MOD)]

    x2 = x.reshape(m, d)
    hh = _norm_modulate(x2, xc.reshape(mc, d), gains[0], sh1, sc1, seq, batch)

    p_head = _project(hh, w_in, 0, head_cols, m, BF16)
    p_kv = _project(hh, w_in, head_cols, 2 * kv_width, m + mc, BF16)
    p_f = _project(hh, w_in, head_cols + 2 * kv_width, 2 * hgrn_width, m + mc, F32)
    p_v = _project(hh, w_in, head_cols + 2 * kv_width + 2 * hgrn_width, hgrn_width, m + mc, BF16)

    cos, sin = _rope_tables(seq)
    attn = _attention(p_head, p_kv, q_gain, k_gain, cos, sin, batch, seq, ctx, kv_heads)
    hgrn = _hgrn(p_head, p_f, p_v, lower_bounds, hg_gain, batch, seq, ctx, attn_width, hgrn_width)
    y = _out_projection(attn, hgrn, w_out)

    x1, h2, logits_t = _epilogue(x2, y, gains[1], gains[2], g1, sh2, sc2, w_router.T, b_router, seq)
    idx, wts, pos, cnt = _route(logits_t)

    row_tile = 256
    n_tiles = -(-(TOP_K * m + n_experts * (row_tile - 1)) // row_tile)
    slots, tile_expert, _, ends = _slots(idx, pos, cnt, row_tile, n_tiles)
    tile_expert = tile_expert[0, :n_tiles]
    n_used = ends[n_experts - 1, 0:1]

    tok_tile = _tile(seq, 128, 8)
    nt = m // tok_tile
    slot_tiles = slots.reshape(TOP_K, nt, tok_tile).transpose(1, 0, 2).reshape(nt, 1, TOP_K * tok_tile)
    tok_of_slot = _inverse_slots(slot_tiles, n_tiles * row_tile, tok_tile)
    xs = _dispatch(n_used, tok_of_slot, h2, row_tile)
    hmid = _expert_up(tile_expert, n_used, xs, w_gate, w_up, b_gate, b_up, row_tile)
    ye = _expert_down(tile_expert, n_used, hmid, w_down, b_down, row_tile, _tile(d, 2048, 256))
    out = _combine(slot_tiles, ye, wts.T, x1, gains[3], g2, seq, tok_tile)
    return out.reshape(batch, seq, d)


def kernel(x, c, ctx, c_ctx, w_ada, b_ada, norm_gains, w_in, q_norm_gain, k_norm_gain, hgrn_lb_logits,
           hgrn_norm_gain, w_out, w_router, b_router, w_gate, b_gate, w_up, b_up, w_down, b_down):
    assert w_ada.shape[0] == 1, "single-layer stack only"
    lower_bounds = jnp.cumsum(jax.nn.softmax(hgrn_lb_logits.astype(F32), axis=0), axis=0)
    return _layer(x, ctx, c, c_ctx, lower_bounds[0], w_ada[0], b_ada[0], norm_gains[0], w_in[0],
                  q_norm_gain[0], k_norm_gain[0], hgrn_norm_gain[0], w_out[0], w_router[0], b_router[0],
                  w_gate[0], b_gate[0], w_up[0], b_up[0], w_down[0], b_down[0])
```

```python
import functools

import jax
import jax.numpy as jnp
from jax import lax
from jax.experimental import pallas as pl
from jax.experimental.pallas import tpu as pltpu

F32 = jnp.float32
BF16 = jnp.bfloat16
U32 = jnp.uint32
I32 = jnp.int32

GRID_W = 64
N_MOD = 6
EPS = 1e-6
HEAD_DIM = 128
GQA_GROUP = 4
ROPE_THETA = 10000.0
LOG2_E = 1.4426950408889634
HGRN_EXPAND = 128
CHUNK = 64
TOP_K = 4
SWIGLU_LIMIT = 7.0
SWIGLU_ALPHA = 1.702
MOD_ROWS = 8

V7X_VMEM_LIMIT = 56 * 1024 * 1024


def _cp(*sem):
    return pltpu.CompilerParams(dimension_semantics=sem, vmem_limit_bytes=V7X_VMEM_LIMIT)


def _tile(n, pref, mult):
    if n <= pref:
        return n
    t = (pref // mult) * mult
    while t > mult and n % t:
        t -= mult
    assert n % t == 0, (n, pref, mult)
    return t


def _dot(a, b):
    return jnp.dot(a, b, preferred_element_type=F32)


def _dot_nt(a, b):
    return lax.dot_general(a, b, (((1,), (1,)), ((), ())), preferred_element_type=F32)


def _dot_tn(a, b):
    return lax.dot_general(a, b, (((0,), (0,)), ((), ())), preferred_element_type=F32)


def _silu(v):
    return v * jax.nn.sigmoid(v)


def _rms(v, gain):
    return v * lax.rsqrt(jnp.mean(v * v, axis=-1, keepdims=True) + EPS) * gain


def _ada_kernel(c_ref, w_ref, b_ref, o_ref):
    s = _silu(c_ref[...]).astype(BF16)
    o_ref[...] = _dot(s, w_ref[...].astype(BF16)) + b_ref[...]


def _ada_modulation(cond, w_ada, b_ada):
    rows, d = cond.shape
    n = w_ada.shape[1]
    tn = _tile(n, 512, 128)
    return pl.pallas_call(
        _ada_kernel,
        out_shape=jax.ShapeDtypeStruct((rows, n), F32),
        grid=(n // tn,),
        in_specs=[pl.BlockSpec((rows, d), lambda j: (0, 0)),
                  pl.BlockSpec((d, tn), lambda j: (0, j)),
                  pl.BlockSpec((1, tn), lambda j: (0, j))],
        out_specs=pl.BlockSpec((rows, tn), lambda j: (0, j)),
        compiler_params=_cp("arbitrary"),
        name="ada_modulation",
    )(cond, w_ada, b_ada.reshape(1, n))


def _norm_mod_kernel(nx, tiles_per_batch, ctx_row, x_ref, c_ref, g_ref, sh_ref, sc_ref, o_ref):
    i = pl.program_id(0)
    is_x = i < nx
    v = jnp.where(is_x, x_ref[...], c_ref[...])
    r = jnp.where(is_x, i // tiles_per_batch, ctx_row)
    shift = sh_ref[pl.ds(r, 1), :]
    scale = sc_ref[pl.ds(r, 1), :]
    o_ref[...] = (_rms(v, g_ref[...]) * (1.0 + scale) + shift).astype(o_ref.dtype)


def _norm_modulate(x2, c2, gain, shift_tab, scale_tab, seq, ctx_row):
    m, d = x2.shape
    mc = c2.shape[0]
    tm = _tile(_gcd(seq, mc), 256, 16)
    nx, ncx = m // tm, mc // tm
    return pl.pallas_call(
        functools.partial(_norm_mod_kernel, nx, seq // tm, ctx_row),
        out_shape=jax.ShapeDtypeStruct((m + mc, d), BF16),
        grid=(nx + ncx,),
        in_specs=[pl.BlockSpec((tm, d), lambda i: (jnp.minimum(i, nx - 1), 0)),
                  pl.BlockSpec((tm, d), lambda i: (jnp.maximum(i - nx, 0), 0)),
                  pl.BlockSpec((1, d), lambda i: (0, 0)),
                  pl.BlockSpec((MOD_ROWS, d), lambda i: (0, 0)),
                  pl.BlockSpec((MOD_ROWS, d), lambda i: (0, 0))],
        out_specs=pl.BlockSpec((tm, d), lambda i: (i, 0)),
        compiler_params=_cp("arbitrary"),
        name="norm_modulate",
    )(x2, c2, gain.reshape(1, d), shift_tab, scale_tab)


def _gcd(a, b):
    while b:
        a, b = b, a % b
    return a


def _mm_kernel(a_ref, w_ref, o_ref):
    o_ref[...] = _dot(a_ref[...], w_ref[...].astype(BF16)).astype(o_ref.dtype)


def _project(a, w, col0, ncols, m_rows, out_dtype):
    k = a.shape[1]
    tn = _tile(_gcd(ncols, col0) if col0 else ncols, 1024, 128)
    tm = _tile(m_rows, 512, 16)
    jb = col0 // tn
    return pl.pallas_call(
        _mm_kernel,
        out_shape=jax.ShapeDtypeStruct((m_rows, ncols), out_dtype),
        grid=(ncols // tn, m_rows // tm),
        in_specs=[pl.BlockSpec((tm, k), lambda j, i: (i, 0)),
                  pl.BlockSpec((k, tn), lambda j, i: (0, jb + j))],
        out_specs=pl.BlockSpec((tm, tn), lambda j, i: (i, j)),
        compiler_params=_cp("arbitrary", "arbitrary"),
        name="in_projection",
    )(a, w)


def _mm2_kernel(ka, a1_ref, a2_ref, w_ref, o_ref):
    w = w_ref[...].astype(BF16)
    o_ref[...] = _dot(a1_ref[...], w[:ka]) + _dot(a2_ref[...], w[ka:])


def _out_projection(a1, a2, w):
    m, ka = a1.shape
    kb = a2.shape[1]
    n = w.shape[1]
    tn = _tile(n, 1024, 128)
    tm = _tile(m, 512, 16)
    return pl.pallas_call(
        functools.partial(_mm2_kernel, ka),
        out_shape=jax.ShapeDtypeStruct((m, n), F32),
        grid=(n // tn, m // tm),
        in_specs=[pl.BlockSpec((tm, ka), lambda j, i: (i, 0)),
                  pl.BlockSpec((tm, kb), lambda j, i: (i, 0)),
                  pl.BlockSpec((ka + kb, tn), lambda j, i: (0, j))],
        out_specs=pl.BlockSpec((tm, tn), lambda j, i: (i, j)),
        compiler_params=_cp("arbitrary", "arbitrary"),
        name="out_projection",
    )(a1, a2, w)


def _rope(v, cos, sin_signed):
    lane = lax.broadcasted_iota(I32, v.shape, 1)
    partner = jnp.where((lane & 32) == 0, pltpu.roll(v, HEAD_DIM - 32, 1), pltpu.roll(v, 32, 1))
    return v * cos + partner * sin_signed


def _attn_kernel(tq, seq, ctx, q_ref, kl_ref, kc_ref, vl_ref, vc_ref, qg_ref, kg_ref, cos_ref, sin_ref,
                 o_ref, k_sc, v_sc):
    qi = pl.program_id(2)

    @pl.when(qi == 0)
    def _():
        kg = kg_ref[...]
        k_sc[0:ctx, :] = _rms(kc_ref[...].astype(F32), kg).astype(BF16)
        kl = _rope(_rms(kl_ref[...].astype(F32), kg), cos_ref[...], sin_ref[...])
        k_sc[ctx:ctx + seq, :] = kl.astype(BF16)
        v_sc[0:ctx, 0:HEAD_DIM] = vc_ref[...]
        v_sc[ctx:ctx + seq, 0:HEAD_DIM] = vl_ref[...]
        v_sc[:, HEAD_DIM:2 * HEAD_DIM] = jnp.ones((ctx + seq, HEAD_DIM), BF16)

    r0 = pl.multiple_of(qi * tq, tq)
    cos = cos_ref[pl.ds(r0, tq), :]
    sin = sin_ref[pl.ds(r0, tq), :]
    qg = qg_ref[...] * (HEAD_DIM ** -0.5 * LOG2_E)
    keys = k_sc[...]
    vals = v_sc[...]
    parts = [(slice(r * (tq // 2), (r + 1) * (tq // 2)), slice(h * HEAD_DIM, (h + 1) * HEAD_DIM))
             for r in range(2) for h in range(GQA_GROUP)]
    def score(part):
        rs, hs = part
        return _dot_nt(_rope(_rms(q_ref[rs, hs].astype(F32), qg), cos[rs], sin[rs]).astype(BF16), keys)

    def finish(part, s):
        rs, hs = part
        p = jnp.exp2(s - jnp.max(s, axis=-1, keepdims=True))
        ov = _dot(p.astype(BF16), vals)
        o_ref[rs, hs] = (ov[:, 0:HEAD_DIM] / ov[:, HEAD_DIM:HEAD_DIM + 1]).astype(o_ref.dtype)

    s_prev = score(parts[0])
    for i in range(1, len(parts)):
        s_next = score(parts[i])
        finish(parts[i - 1], s_prev)
        s_prev = s_next
    finish(parts[-1], s_prev)


def _attention(p_head, p_kv, q_gain, k_gain, cos, sin_signed, batch, seq, ctx, kv_heads):
    tq = _tile(seq, 256, 16)
    nq = seq // tq
    gw = GQA_GROUP * HEAD_DIM
    ctx_blk0 = batch * seq // ctx
    return pl.pallas_call(
        functools.partial(_attn_kernel, tq, seq, ctx),
        out_shape=jax.ShapeDtypeStruct((batch * seq, kv_heads * gw), BF16),
        grid=(batch, kv_heads, nq),
        in_specs=[pl.BlockSpec((tq, gw), lambda b, h, q: (b * nq + q, h)),
                  pl.BlockSpec((seq, HEAD_DIM), lambda b, h, q: (b, h)),
                  pl.BlockSpec((ctx, HEAD_DIM), lambda b, h, q: (ctx_blk0 + b, h)),
                  pl.BlockSpec((seq, HEAD_DIM), lambda b, h, q: (b, kv_heads + h)),
                  pl.BlockSpec((ctx, HEAD_DIM), lambda b, h, q: (ctx_blk0 + b, kv_heads + h)),
                  pl.BlockSpec((1, HEAD_DIM), lambda b, h, q: (0, 0)),
                  pl.BlockSpec((1, HEAD_DIM), lambda b, h, q: (0, 0)),
                  pl.BlockSpec((seq, HEAD_DIM), lambda b, h, q: (0, 0)),
                  pl.BlockSpec((seq, HEAD_DIM), lambda b, h, q: (0, 0))],
        out_specs=pl.BlockSpec((tq, gw), lambda b, h, q: (b * nq + q, h)),
        scratch_shapes=[pltpu.VMEM((ctx + seq, HEAD_DIM), BF16),
                        pltpu.VMEM((ctx + seq, 2 * HEAD_DIM), BF16)],
        compiler_params=_cp("arbitrary", "arbitrary", "arbitrary"),
        name="gqa_attention",
    )(p_head, p_kv, p_kv, p_kv, p_kv, q_gain.reshape(1, HEAD_DIM), k_gain.reshape(1, HEAD_DIM), cos, sin_signed)


def _hgrn_kernel(hpb, seq, ctx, rows, lb_ref, gain_ref, q_ref, g_ref, ff_ref, fb_ref, v_ref,
                 cff_ref, cfb_ref, cv_ref, o_ref, st_ref, of_ref, ob_ref):
    cpb = rows // CHUNK
    lg_chunk = CHUNK.bit_length() - 1
    row = lax.broadcasted_iota(I32, (rows, rows), 0)
    col = lax.broadcasted_iota(I32, (rows, rows), 1)
    same = (row >> lg_chunk) == (col >> lg_chunk)
    masks = (same & (col <= row), same & (col >= row))
    tris = tuple(jnp.where(m, 1.0, 0.0).astype(BF16) for m in masks)
    row_chunk = lax.broadcasted_iota(I32, (rows, HGRN_EXPAND), 0) >> lg_chunk

    st_ref[...] = jnp.zeros(st_ref.shape, F32)

    def lane_stack(a):
        return jnp.concatenate([jnp.where(row_chunk == c, a, 0.0).astype(BF16) for c in range(cpb)], axis=1)

    def chunk_rows(b, off):
        return jnp.concatenate([jnp.broadcast_to(b[c * CHUNK + off:c * CHUNK + off + 1, :], (CHUNK, HGRN_EXPAND))
                                for c in range(cpb)], axis=0)

    def blocks(items, with_out):
        n = len(items)
        last = [0 if it[5] else CHUNK - 1 for it in items]
        mid = [CHUNK // 2 if it[5] else CHUNK // 2 - 1 for it in items]
        ks, splits = [], []
        for _, raw, _, _, lb, _ in items:
            f = lb + (1.0 - lb) * jax.nn.sigmoid(raw)
            logf = jnp.log(f)
            hi = logf.astype(BF16)
            ks.append(1.0 - f)
            splits.append((hi, (logf - hi.astype(F32)).astype(BF16)))
        bs = [_dot(tris[items[j][5]], splits[j][0]) + _dot(tris[items[j][5]], splits[j][1]) for j in range(n)]
        uts = [_dot_tn(items[j][2], lane_stack(ks[j] * jnp.exp(chunk_rows(bs[j], last[j]) - bs[j]))) for j in range(n)]
        slabs = []
        for j in range(n):
            st = st_ref[items[j][0]]
            starts = [None] * cpb
            for c in (reversed(range(cpb)) if items[j][5] else range(cpb)):
                starts[c] = st.astype(BF16)
                st = (st * jnp.exp(bs[j][c * CHUNK + last[j]:c * CHUNK + last[j] + 1, :])
                      + uts[j][:, c * HGRN_EXPAND:(c + 1) * HGRN_EXPAND])
            st_ref[items[j][0]] = st
            slabs.append(jnp.concatenate(starts, axis=1))
        if not with_out:
            return None
        scores = []
        for j in range(n):
            b_mid = chunk_rows(bs[j], mid[j])
            qt = (items[j][3] * jnp.exp(bs[j] - b_mid)).astype(BF16)
            kt = (ks[j] * jnp.exp(b_mid - bs[j])).astype(BF16)
            scores.append(_dot_nt(qt, kt))
        outs = []
        for j in range(n):
            a = jnp.where(masks[items[j][5]], scores[j], 0.0).astype(BF16)
            inter = _dot_nt(lane_stack(items[j][3] * jnp.exp(bs[j])), slabs[j])
            outs.append(_dot(a, items[j][2]) + inter)
        return outs

    def ctx_body(i, carry):
        r_f = pl.multiple_of(i * rows, rows)
        r_b = pl.multiple_of((ctx // rows - 1 - i) * rows, rows)
        items = []
        for g in range(hpb):
            sl = slice(g * HGRN_EXPAND, (g + 1) * HGRN_EXPAND)
            items.append((2 * g, cff_ref[pl.ds(r_f, rows), sl], cv_ref[pl.ds(r_f, rows), sl], None, lb_ref[0:1, sl], 0))
            items.append((2 * g + 1, cfb_ref[pl.ds(r_b, rows), sl], cv_ref[pl.ds(r_b, rows), sl], None,
                          lb_ref[1:2, sl], 1))
        blocks(items, False)
        return carry

    lax.fori_loop(0, ctx // rows, ctx_body, 0)

    def seq_body(i, carry):
        r_f = pl.multiple_of(i * rows, rows)
        r_b = pl.multiple_of((seq // rows - 1 - i) * rows, rows)
        items, dests = [], []
        for g in range(hpb):
            sl = slice(g * HGRN_EXPAND, (g + 1) * HGRN_EXPAND)
            for rev, r, f_ref, out_ref in ((0, r_f, ff_ref, of_ref), (1, r_b, fb_ref, ob_ref)):
                q = _silu(q_ref[pl.ds(r, rows), sl].astype(F32))
                items.append((2 * g + rev, f_ref[pl.ds(r, rows), sl], v_ref[pl.ds(r, rows), sl], q,
                              lb_ref[rev:rev + 1, sl], rev))
                dests.append((out_ref, r, sl))
        for (out_ref, r, sl), o in zip(dests, blocks(items, True)):
            out_ref[pl.ds(r, rows), sl] = o
        return carry

    lax.fori_loop(0, seq // rows, seq_body, 0)

    rb = _tile(seq, 256, 8)

    def read_body(i, carry):
        r = pl.multiple_of(i * rb, rb)
        for g in range(hpb):
            sl = slice(g * HGRN_EXPAND, (g + 1) * HGRN_EXPAND)
            o = of_ref[pl.ds(r, rb), sl] + ob_ref[pl.ds(r, rb), sl]
            y = _rms(o, gain_ref[...]) * _silu(g_ref[pl.ds(r, rb), sl].astype(F32))
            o_ref[pl.ds(r, rb), sl] = y.astype(o_ref.dtype)
        return carry

    lax.fori_loop(0, seq // rb, read_body, 0)


def _hgrn(p_head, p_f, p_v, lower_bounds, hg_gain, batch, seq, ctx, attn_width, hgrn_width):
    heads = hgrn_width // HGRN_EXPAND
    hpb = 2 if heads % 2 == 0 else 1
    bw = hpb * HGRN_EXPAND
    nh = heads // hpb
    q0, g0 = attn_width // bw, (attn_width + hgrn_width) // bw
    ctx_blk0 = batch * seq // ctx
    rows = _tile(_gcd(seq, ctx), 256, CHUNK)
    lat = lambda c0: pl.BlockSpec((seq, bw), lambda b, h: (b, c0 + h))
    cx = lambda c0: pl.BlockSpec((ctx, bw), lambda b, h: (ctx_blk0 + b, c0 + h))
    return pl.pallas_call(
        functools.partial(_hgrn_kernel, hpb, seq, ctx, rows),
        out_shape=jax.ShapeDtypeStruct((batch * seq, hgrn_width), BF16),
        grid=(batch, nh),
        in_specs=[pl.BlockSpec((2, bw), lambda b, h: (0, h)),
                  pl.BlockSpec((1, HGRN_EXPAND), lambda b, h: (0, 0)),
                  lat(q0), lat(g0), lat(0), lat(nh), lat(0), cx(0), cx(nh), cx(0)],
        out_specs=pl.BlockSpec((seq, bw), lambda b, h: (b, h)),
        scratch_shapes=[pltpu.VMEM((2 * hpb, HGRN_EXPAND, HGRN_EXPAND), F32),
                        pltpu.VMEM((seq, bw), F32),
                        pltpu.VMEM((seq, bw), F32)],
        compiler_params=_cp("arbitrary", "arbitrary"),
        name="hgrn2_scan",
    )(lower_bounds, hg_gain.reshape(1, HGRN_EXPAND), p_head, p_head, p_f, p_f, p_v, p_f, p_f, p_v)


def _epilogue_kernel(tiles_per_batch, x_ref, y_ref, g1n_ref, g2n_ref, gate_ref, sh_ref, sc_ref, wr_ref, br_ref,
                     x1_ref, hp_ref, lg_ref):
    r = pl.program_id(0) // tiles_per_batch
    x1 = x_ref[...] + gate_ref[pl.ds(r, 1), :] * _rms(y_ref[...], g1n_ref[...])
    x1_ref[...] = x1
    h = _rms(x1, g2n_ref[...]) * (1.0 + sc_ref[pl.ds(r, 1), :]) + sh_ref[pl.ds(r, 1), :]
    hp_ref[...] = h
    w = wr_ref[...]
    w_hi = w.astype(BF16)
    w_lo = (w - w_hi.astype(F32)).astype(BF16)
    h_hi = h.astype(BF16)
    h_lo = (h - h_hi.astype(F32)).astype(BF16)
    lg_ref[...] = _dot_nt(w_hi, h_hi) + _dot_nt(w_hi, h_lo) + _dot_nt(w_lo, h_hi) + br_ref[...]


def _epilogue(x2, y, gain1, gain2, gate_tab, shift_tab, scale_tab, w_router_t, b_router, seq):
    m, d = x2.shape
    ne = w_router_t.shape[0]
    tm = _tile(seq, 256, 128)
    row = lambda: pl.BlockSpec((tm, d), lambda i: (i, 0))
    tab = lambda: pl.BlockSpec((MOD_ROWS, d), lambda i: (0, 0))
    vec = lambda: pl.BlockSpec((1, d), lambda i: (0, 0))
    return pl.pallas_call(
        functools.partial(_epilogue_kernel, seq // tm),
        out_shape=(jax.ShapeDtypeStruct((m, d), F32),
                   jax.ShapeDtypeStruct((m, d), F32),
                   jax.ShapeDtypeStruct((ne, m), F32)),
        grid=(m // tm,),
        in_specs=[row(), row(), vec(), vec(), tab(), tab(), tab(),
                  pl.BlockSpec((ne, d), lambda i: (0, 0)),
                  pl.BlockSpec((ne, 1), lambda i: (0, 0))],
        out_specs=(row(), row(), pl.BlockSpec((ne, tm), lambda i: (0, i))),
        compiler_params=_cp("arbitrary"),
        name="residual_norm_router",
    )(x2, y, gain1.reshape(1, d), gain2.reshape(1, d), gate_tab, shift_tab, scale_tab,
      w_router_t, b_router.reshape(ne, 1))


def _route_kernel(lg_ref, idx_ref, wt_ref, pos_ref, cnt_ref, carry_ref):
    i = pl.program_id(0)
    ne, tn = lg_ref.shape

    @pl.when(i == 0)
    def _():
        carry_ref[...] = jnp.zeros(carry_ref.shape, F32)

    lg = lg_ref[...]
    eidx = lax.broadcasted_iota(I32, (ne, tn), 0)
    vals, hots = [], []
    for k in range(TOP_K):
        m = jnp.max(lg, axis=0, keepdims=True)
        sel = jnp.min(jnp.where(lg == m, eidx, ne), axis=0, keepdims=True)
        hot = eidx == sel
        idx_ref[k:k + 1, :] = sel
        vals.append(m)
        hots.append(hot)
        lg = jnp.where(hot, -jnp.inf, lg)
    exps = [jnp.exp(v - vals[0]) for v in vals]
    denom = exps[0] + exps[1] + exps[2] + exps[3]
    for k in range(TOP_K):
        wt_ref[k:k + 1, :] = exps[k] / denom

    member = hots[0] | hots[1] | hots[2] | hots[3]
    t_row = lax.broadcasted_iota(I32, (tn, tn), 0)
    t_col = lax.broadcasted_iota(I32, (tn, tn), 1)
    before = (t_row < t_col).astype(BF16)
    rank = _dot(member.astype(BF16), before) + carry_ref[:, 0:1]
    for k in range(TOP_K):
        pos_ref[k:k + 1, :] = jnp.sum(jnp.where(hots[k], rank, 0.0), axis=0, keepdims=True).astype(I32)
    carry_ref[...] = carry_ref[...] + jnp.sum(member.astype(F32), axis=1, keepdims=True)
    cnt_ref[...] = carry_ref[...].astype(I32)


def _route(logits_t):
    ne, m = logits_t.shape
    tn = _tile(m, 512, 128)
    out4 = lambda: pl.BlockSpec((TOP_K, tn), lambda i: (0, i))
    return pl.pallas_call(
        _route_kernel,
        out_shape=(jax.ShapeDtypeStruct((TOP_K, m), I32),
                   jax.ShapeDtypeStruct((TOP_K, m), F32),
                   jax.ShapeDtypeStruct((TOP_K, m), I32),
                   jax.ShapeDtypeStruct((ne, 128), I32)),
        grid=(m // tn,),
        in_specs=[pl.BlockSpec((ne, tn), lambda i: (0, i))],
        out_specs=(out4(), out4(), out4(), pl.BlockSpec((ne, 128), lambda i: (0, 0))),
        scratch_shapes=[pltpu.VMEM((ne, 128), F32)],
        compiler_params=_cp("arbitrary"),
        name="top4_route",
    )(logits_t)


def _slots_kernel(row_tile, lg_tile, idx_ref, pos_ref, cnt_ref, slot_ref, te_ref, start_ref, ends_ref):
    ne = cnt_ref.shape[0]
    tn = idx_ref.shape[1]
    tiles = (cnt_ref[...] + (row_tile - 1)) >> lg_tile
    e_row = lax.broadcasted_iota(I32, (ne, ne), 0)
    e_col = lax.broadcasted_iota(I32, (ne, ne), 1)
    upto = jnp.where(e_col <= e_row, 1.0, 0.0).astype(BF16)
    ends = _dot(upto, tiles.astype(F32).astype(BF16)).astype(I32)
    start = (ends - tiles) << lg_tile
    start_ref[...] = start
    ends_ref[...] = ends
    eidx = lax.broadcasted_iota(I32, (ne, tn), 0)
    start_col = start[:, 0:1]
    for k in range(TOP_K):
        hit = eidx == idx_ref[k:k + 1, :]
        slot_ref[k:k + 1, :] = jnp.sum(jnp.where(hit, start_col, 0), axis=0, keepdims=True) + pos_ref[k:k + 1, :]
    tile_i = lax.broadcasted_iota(I32, (ne, te_ref.shape[1]), 1)
    te = jnp.sum(jnp.where(tile_i >= ends[:, 0:1], 1, 0), axis=0, keepdims=True)
    te_ref[...] = jnp.minimum(te, ne - 1)


def _slots(idx, pos, cnt, row_tile, n_tiles):
    ne = cnt.shape[0]
    m = idx.shape[1]
    tn = _tile(m, 2048, 128)
    ntp = -(-n_tiles // 128) * 128
    lg_tile = row_tile.bit_length() - 1
    assert 1 << lg_tile == row_tile and -(-m // row_tile) <= 256
    io4 = lambda: pl.BlockSpec((TOP_K, tn), lambda i: (0, i))
    small = lambda w: pl.BlockSpec((ne, w), lambda i: (0, 0))
    return pl.pallas_call(
        functools.partial(_slots_kernel, row_tile, lg_tile),
        out_shape=(jax.ShapeDtypeStruct((TOP_K, m), I32),
                   jax.ShapeDtypeStruct((1, ntp), I32),
                   jax.ShapeDtypeStruct((ne, 128), I32),
                   jax.ShapeDtypeStruct((ne, 128), I32)),
        grid=(m // tn,),
        in_specs=[io4(), io4(), small(128)],
        out_specs=(io4(), pl.BlockSpec((1, ntp), lambda i: (0, 0)), small(128), small(128)),
        compiler_params=_cp("arbitrary"),
        name="route_slots",
    )(idx, pos, cnt)


DMA_ISSUE_UNROLL = 8


def _inverse_kernel(tm, row_tile, ne, n_slots, start_ref, cnt_ref, nu_ref, slot_ref, tok_ref):
    i = pl.program_id(0)

    @pl.when(i == 0)
    def _():
        def zero(j, c):
            tok_ref[j] = 0
            return c

        for e in range(ne):
            first = start_ref[e] + cnt_ref[e]
            lax.fori_loop(first, first + ((-cnt_ref[e]) & (row_tile - 1)), zero, 0)
        lax.fori_loop(nu_ref[0] * row_tile, n_slots, zero, 0)

    def body(g, c):
        for u in range(DMA_ISSUE_UNROLL):
            r = g * DMA_ISSUE_UNROLL + u
            for k in range(TOP_K):
                tok_ref[slot_ref[0, 0, k * tm + r]] = i * tm + r
        return c

    lax.fori_loop(0, tm // DMA_ISSUE_UNROLL, body, 0)


def _inverse_slots(starts, counts, n_used, slot_tiles, n_slots, tm, row_tile):
    nt = slot_tiles.shape[0]
    return pl.pallas_call(
        functools.partial(_inverse_kernel, tm, row_tile, starts.shape[0], n_slots),
        out_shape=jax.ShapeDtypeStruct((n_slots,), I32),
        grid_spec=pltpu.PrefetchScalarGridSpec(
            num_scalar_prefetch=3,
            grid=(nt,),
            in_specs=[pl.BlockSpec((1, 1, TOP_K * tm), lambda i, st, ct, nu: (i, 0, 0), memory_space=pltpu.SMEM)],
            out_specs=pl.BlockSpec(memory_space=pltpu.SMEM)),
        compiler_params=_cp("arbitrary"),
        name="route_inverse",
    )(starts, counts, n_used, slot_tiles)


def _dispatch_kernel(tm, nu_ref, te_ref, start_ref, cnt_ref, tok_ref, next_ref, src_hbm, o_ref, buf, sem):
    i = pl.program_id(0)
    n_used = nu_ref[0]
    n_tiles = pl.num_programs(0)

    def row_groups(t):
        e = te_ref[jnp.minimum(t, n_tiles - 1)]
        rows = jnp.clip(start_ref[e] + cnt_ref[e] - t * tm, 0, tm)
        return (rows + (DMA_ISSUE_UNROLL - 1)) // DMA_ISSUE_UNROLL

    def issue(t_ref, t, b):
        def body(g, c):
            for u in range(DMA_ISSUE_UNROLL):
                r = g * DMA_ISSUE_UNROLL + u
                pltpu.make_async_copy(src_hbm.at[pl.ds(t_ref[0, 0, r], 1)], buf.at[b, pl.ds(r, 1)],
                                      sem.at[b]).start(priority=u % 2)
            return c

        lax.fori_loop(0, row_groups(t), body, 0)

    @pl.when(i == 0)
    def _():
        buf[...] = jnp.zeros(buf.shape, buf.dtype)
        issue(tok_ref, 0, 0)

    @pl.when(i + 1 < n_used)
    def _():
        issue(next_ref, i + 1, (i + 1) % 2)

    @pl.when(i < n_used)
    def _():
        b = i % 2

        def wait_group(g, c):
            pltpu.make_async_copy(src_hbm.at[pl.ds(0, DMA_ISSUE_UNROLL)], buf.at[b, pl.ds(0, DMA_ISSUE_UNROLL)],
                                  sem.at[b]).wait()
            return c

        lax.fori_loop(0, row_groups(i), wait_group, 0)
        o_ref[...] = buf[b].astype(o_ref.dtype)

    @pl.when(i >= n_used)
    def _():
        o_ref[...] = jnp.zeros(o_ref.shape, o_ref.dtype)


def _dispatch(n_used, tile_expert, starts, counts, tok_of_slot, src, tm):
    w = src.shape[1]
    nt = tok_of_slot.shape[0] // tm
    tok_tiles = tok_of_slot.reshape(nt, 1, tm)
    tok_spec = lambda f: pl.BlockSpec((1, 1, tm), f, memory_space=pltpu.SMEM)
    return pl.pallas_call(
        functools.partial(_dispatch_kernel, tm),
        out_shape=jax.ShapeDtypeStruct((nt * tm, w), BF16),
        grid_spec=pltpu.PrefetchScalarGridSpec(
            num_scalar_prefetch=4,
            grid=(nt,),
            in_specs=[tok_spec(lambda i, *_: (i, 0, 0)),
                      tok_spec(lambda i, *_: (jnp.minimum(i + 1, nt - 1), 0, 0)),
                      pl.BlockSpec(memory_space=pl.ANY)],
            out_specs=pl.BlockSpec((tm, w), lambda i, *_: (i, 0)),
            scratch_shapes=[pltpu.VMEM((2, tm, w), src.dtype), pltpu.SemaphoreType.DMA((2,))]),
        compiler_params=_cp("arbitrary"),
        name="moe_dispatch",
    )(n_used, tile_expert, starts, counts, tok_tiles, tok_tiles, src)


def _stream_expert_weights(n_pass, n_tiles, width, te_ref, nu_ref, grp_ref, sem, weights):
    c, i = pl.program_id(0), pl.program_id(1)
    n_used = nu_ref[0]

    def copies(e, cc, slot):
        col = pl.multiple_of(cc * width, width)
        return [pltpu.make_async_copy(hbm.at[e, :, pl.ds(col, width)], buf.at[slot], sem.at[j, slot])
                for j, (hbm, buf) in enumerate(weights)]

    @pl.when((c == 0) & (i == 0))
    def _():
        grp_ref[0] = 0
        for cp in copies(te_ref[0], 0, 0):
            cp.start()

    e = te_ref[i]
    first = (i < n_used) & ((i == 0) | (e != te_ref[jnp.maximum(i - 1, 0)]))

    @pl.when(first)
    def _():
        grp = grp_ref[0]
        slot = grp % 2
        for cp in copies(e, c, slot):
            cp.wait()
        nxt = lax.while_loop(lambda j: (j < n_used) & (te_ref[jnp.minimum(j, n_tiles - 1)] == e),
                             lambda j: j + 1, i + 1)
        more = nxt < n_used

        @pl.when(more)
        def _():
            for cp in copies(te_ref[jnp.minimum(nxt, n_tiles - 1)], c, 1 - slot):
                cp.start(priority=1)

        @pl.when(jnp.logical_not(more) & (c + 1 < n_pass))
        def _():
            for cp in copies(te_ref[0], c + 1, 1 - slot):
                cp.start(priority=1)

        grp_ref[0] = grp + 1

    return (grp_ref[0] + 1) % 2


def _expert_up_kernel(n_pass, n_tiles, te_ref, nu_ref, x_ref, wg_hbm, wu_hbm, bg_ref, bu_ref, h_ref,
                      wg_buf, wu_buf, sem, grp_ref):
    slot = _stream_expert_weights(n_pass, n_tiles, h_ref.shape[1], te_ref, nu_ref, grp_ref, sem,
                                  [(wg_hbm, wg_buf), (wu_hbm, wu_buf)])

    @pl.when(pl.program_id(1) < nu_ref[0])
    def _():
        x = x_ref[...]

        def proj(w_buf, b_ref):
            return _dot(x, w_buf[slot].astype(BF16)) + b_ref[0]

        g = jnp.minimum(proj(wg_buf, bg_ref), SWIGLU_LIMIT)
        u = jnp.clip(proj(wu_buf, bu_ref), -SWIGLU_LIMIT, SWIGLU_LIMIT)
        h_ref[...] = (g * jax.nn.sigmoid(SWIGLU_ALPHA * g) * (u + 1.0)).astype(h_ref.dtype)

    @pl.when(pl.program_id(1) >= nu_ref[0])
    def _():
        h_ref[...] = jnp.zeros(h_ref.shape, h_ref.dtype)


def _used_tile(i, nu):
    return jnp.minimum(i, nu[0] - 1)


def _expert_up(tile_expert, n_used, xs, w_gate, w_up, b_gate, b_up, tm):
    s_pad = xs.shape[0]
    ne, d, f = w_gate.shape
    fc = _tile(f, 512, 128)
    nt = s_pad // tm
    wspec = lambda: pl.BlockSpec(memory_space=pl.ANY)
    bspec = lambda: pl.BlockSpec((1, 1, fc), lambda c, i, te, nu: (te[i], 0, c))
    return pl.pallas_call(
        functools.partial(_expert_up_kernel, f // fc, nt),
        out_shape=jax.ShapeDtypeStruct((s_pad, f), BF16),
        grid_spec=pltpu.PrefetchScalarGridSpec(
            num_scalar_prefetch=2,
            grid=(f // fc, nt),
            in_specs=[pl.BlockSpec((tm, d), lambda c, i, te, nu: (_used_tile(i, nu), 0)),
                      wspec(), wspec(), bspec(), bspec()],
            out_specs=pl.BlockSpec((tm, fc), lambda c, i, te, nu: (i, c)),
            scratch_shapes=[pltpu.VMEM((2, d, fc), F32), pltpu.VMEM((2, d, fc), F32),
                            pltpu.SemaphoreType.DMA((2, 2)), pltpu.SMEM((1,), I32)]),
        compiler_params=_cp("arbitrary", "arbitrary"),
        name="expert_gate_up",
    )(tile_expert, n_used, xs, w_gate, w_up, b_gate.reshape(ne, 1, f), b_up.reshape(ne, 1, f))


def _expert_down_kernel(n_pass, n_tiles, te_ref, nu_ref, h_ref, wd_hbm, bd_ref, y_ref, wd_buf, sem, grp_ref):
    slot = _stream_expert_weights(n_pass, n_tiles, y_ref.shape[1], te_ref, nu_ref, grp_ref, sem, [(wd_hbm, wd_buf)])

    @pl.when(pl.program_id(1) < nu_ref[0])
    def _():
        y_ref[...] = _dot(h_ref[...], wd_buf[slot].astype(BF16)) + bd_ref[0]

    @pl.when(pl.program_id(1) >= nu_ref[0])
    def _():
        y_ref[...] = jnp.zeros(y_ref.shape, y_ref.dtype)


def _expert_down(tile_expert, n_used, h, w_down, b_down, tm, tn):
    s_pad, f = h.shape
    ne, _, d = w_down.shape
    nt = s_pad // tm
    return pl.pallas_call(
        functools.partial(_expert_down_kernel, d // tn, nt),
        out_shape=jax.ShapeDtypeStruct((s_pad, d), F32),
        grid_spec=pltpu.PrefetchScalarGridSpec(
            num_scalar_prefetch=2,
            grid=(d // tn, nt),
            in_specs=[pl.BlockSpec((tm, f), lambda c, i, te, nu: (_used_tile(i, nu), 0)),
                      pl.BlockSpec(memory_space=pl.ANY),
                      pl.BlockSpec((1, 1, tn), lambda c, i, te, nu: (te[i], 0, c))],
            out_specs=pl.BlockSpec((tm, tn), lambda c, i, te, nu: (i, c)),
            scratch_shapes=[pltpu.VMEM((2, f, tn), F32), pltpu.SemaphoreType.DMA((1, 2)),
                            pltpu.SMEM((1,), I32)]),
        compiler_params=_cp("arbitrary", "arbitrary"),
        name="expert_down",
    )(tile_expert, n_used, h, w_down, b_down.reshape(ne, 1, d))


def _combine_kernel(tm, tiles_per_batch, nt, slot_ref, next_ref, yp_hbm, wt_ref, x1_ref, gn_ref, gate_ref,
                    o_ref, buf, sem):
    i = pl.program_id(0)

    def issue(s_ref, b):
        def body(g, c):
            for u in range(DMA_ISSUE_UNROLL):
                r = g * DMA_ISSUE_UNROLL + u
                for k in range(TOP_K):
                    pltpu.make_async_copy(yp_hbm.at[pl.ds(s_ref[0, 0, k * tm + r], 1)],
                                          buf.at[b, k, pl.ds(r, 1)], sem.at[b]).start(priority=k % 2)
            return c

        lax.fori_loop(0, tm // DMA_ISSUE_UNROLL, body, 0)

    @pl.when(i == 0)
    def _():
        issue(slot_ref, 0)

    @pl.when(i + 1 < nt)
    def _():
        issue(next_ref, (i + 1) % 2)

    b = i % 2
    for k in range(TOP_K):
        pltpu.make_async_copy(yp_hbm.at[pl.ds(0, tm)], buf.at[b, k], sem.at[b]).wait()

    wt = wt_ref[...]
    moe = wt[:, 0:1] * buf[b, 0]
    for k in range(1, TOP_K):
        moe = moe + wt[:, k:k + 1] * buf[b, k]
    r = i // tiles_per_batch
    o_ref[...] = x1_ref[...] + gate_ref[pl.ds(r, 1), :] * _rms(moe, gn_ref[...])


def _combine(slot_tiles, yp, wts, x1, gain, gate_tab, seq, tm):
    m, d = x1.shape
    nt = m // tm
    slot_spec = lambda f: pl.BlockSpec((1, 1, TOP_K * tm), f, memory_space=pltpu.SMEM)
    return pl.pallas_call(
        functools.partial(_combine_kernel, tm, seq // tm, nt),
        out_shape=jax.ShapeDtypeStruct((m, d), F32),
        grid=(nt,),
        in_specs=[slot_spec(lambda i: (i, 0, 0)),
                  slot_spec(lambda i: (jnp.minimum(i + 1, nt - 1), 0, 0)),
                  pl.BlockSpec(memory_space=pl.ANY),
                  pl.BlockSpec((tm, TOP_K), lambda i: (i, 0)),
                  pl.BlockSpec((tm, d), lambda i: (i, 0)),
                  pl.BlockSpec((1, d), lambda i: (0, 0)),
                  pl.BlockSpec((MOD_ROWS, d), lambda i: (0, 0))],
        out_specs=pl.BlockSpec((tm, d), lambda i: (i, 0)),
        scratch_shapes=[pltpu.VMEM((2, TOP_K, tm, d), F32), pltpu.SemaphoreType.DMA((2,))],
        compiler_params=_cp("arbitrary"),
        name="moe_combine",
    )(slot_tiles, slot_tiles, yp, wts, x1, gain.reshape(1, d), gate_tab)


def _rope_tables(seq):
    rows = seq // GRID_W
    axis_dim = HEAD_DIM // 2
    row_ids = jnp.repeat(jnp.arange(rows), GRID_W).astype(F32)
    col_ids = jnp.tile(jnp.arange(GRID_W), rows).astype(F32)
    inv_freq = ROPE_THETA ** (-jnp.arange(0, axis_dim, 2, dtype=F32) / axis_dim)
    ang_r = row_ids[:, None] * inv_freq[None, :]
    ang_c = col_ids[:, None] * inv_freq[None, :]
    cos = jnp.concatenate([jnp.cos(ang_r)] * 2 + [jnp.cos(ang_c)] * 2, axis=1)
    sin = jnp.concatenate([-jnp.sin(ang_r), jnp.sin(ang_r), -jnp.sin(ang_c), jnp.sin(ang_c)], axis=1)
    return cos, sin


def _layer(x, xc, c, c_ctx, lower_bounds, w_ada, b_ada, gains, w_in, q_gain, k_gain, hg_gain, w_out,
           w_router, b_router, w_gate, b_gate, w_up, b_up, w_down, b_down):
    batch, seq, d = x.shape
    ctx = xc.shape[1]
    attn_width = d // 2
    hgrn_width = d // 2
    kv_heads = attn_width // HEAD_DIM // GQA_GROUP
    kv_width = kv_heads * HEAD_DIM
    head_cols = attn_width + 2 * hgrn_width
    n_experts = w_router.shape[1]
    m, mc = batch * seq, batch * ctx
    assert batch < MOD_ROWS and seq % ctx == 0

    cond = jnp.concatenate([c, c_ctx[None, :], jnp.zeros((MOD_ROWS - batch - 1, d), F32)], axis=0)
    mod = _ada_modulation(cond, w_ada, b_ada)
    sh1, sc1, g1, sh2, sc2, g2 = [mod[:, i * d:(i + 1) * d] for i in range(N_MOD)]

    x2 = x.reshape(m, d)
    hh = _norm_modulate(x2, xc.reshape(mc, d), gains[0], sh1, sc1, seq, batch)

    p_head = _project(hh, w_in, 0, head_cols, m, BF16)
    p_kv = _project(hh, w_in, head_cols, 2 * kv_width, m + mc, BF16)
    p_f = _project(hh, w_in, head_cols + 2 * kv_width, 2 * hgrn_width, m + mc, F32)
    p_v = _project(hh, w_in, head_cols + 2 * kv_width + 2 * hgrn_width, hgrn_width, m + mc, BF16)

    cos, sin = _rope_tables(seq)
    attn = _attention(p_head, p_kv, q_gain, k_gain, cos, sin, batch, seq, ctx, kv_heads)
    hgrn = _hgrn(p_head, p_f, p_v, lower_bounds, hg_gain, batch, seq, ctx, attn_width, hgrn_width)
    y = _out_projection(attn, hgrn, w_out)

    x1, h2, logits_t = _epilogue(x2, y, gains[1], gains[2], g1, sh2, sc2, w_router.T, b_router, seq)
    idx, wts, pos, cnt = _route(logits_t)

    row_tile = 256
    n_tiles = -(-(TOP_K * m + n_experts * (row_tile - 1)) // row_tile)
    slots, tile_expert, starts, ends = _slots(idx, pos, cnt, row_tile, n_tiles)
    tile_expert = tile_expert[0, :n_tiles]
    n_used = ends[n_experts - 1, 0:1]
    starts, counts = starts[:, 0], cnt[:, 0]

    tok_tile = _tile(seq, 128, 8)
    nt = m // tok_tile
    slot_tiles = slots.reshape(TOP_K, nt, tok_tile).transpose(1, 0, 2).reshape(nt, 1, TOP_K * tok_tile)
    tok_of_slot = _inverse_slots(starts, counts, n_used, slot_tiles, n_tiles * row_tile, tok_tile, row_tile)
    xs = _dispatch(n_used, tile_expert, starts, counts, tok_of_slot, h2, row_tile)
    hmid = _expert_up(tile_expert, n_used, xs, w_gate, w_up, b_gate, b_up, row_tile)
    ye = _expert_down(tile_expert, n_used, hmid, w_down, b_down, row_tile, _tile(d, 2048, 256))
    out = _combine(slot_tiles, ye, wts.T, x1, gains[3], g2, seq, tok_tile)
    return out.reshape(batch, seq, d)


def kernel(x, c, ctx, c_ctx, w_ada, b_ada, norm_gains, w_in, q_norm_gain, k_norm_gain, hgrn_lb_logits,
           hgrn_norm_gain, w_out, w_router, b_router, w_gate, b_gate, w_up, b_up, w_down, b_down):
    assert w_ada.shape[0] == 1, "single-layer stack only"
    lower_bounds = jnp.cumsum(jax.nn.softmax(hgrn_lb_logits.astype(F32), axis=0), axis=0)
    return _layer(x, ctx, c, c_ctx, lower_bounds[0], w_ada[0], b_ada[0], norm_gains[0], w_in[0],
                  q_norm_gain[0], k_norm_gain[0], hgrn_norm_gain[0], w_out[0], w_router[0], b_router[0],
                  w_gate[0], b_gate[0], w_up[0], b_up[0], w_down[0], b_down[0])
```

```python
import functools

import jax
import jax.numpy as jnp
from jax import lax
from jax.experimental import pallas as pl
from jax.experimental.pallas import tpu as pltpu

F32 = jnp.float32
BF16 = jnp.bfloat16
U32 = jnp.uint32
I32 = jnp.int32

GRID_W = 64
N_MOD = 6
EPS = 1e-6
HEAD_DIM = 128
GQA_GROUP = 4
ROPE_THETA = 10000.0
LOG2_E = 1.4426950408889634
HGRN_EXPAND = 128
CHUNK = 64
TOP_K = 4
SWIGLU_LIMIT = 7.0
SWIGLU_ALPHA = 1.702
MOD_ROWS = 8

V7X_VMEM_LIMIT = 56 * 1024 * 1024


def _cp(*sem):
    return pltpu.CompilerParams(dimension_semantics=sem, vmem_limit_bytes=V7X_VMEM_LIMIT)


def _tile(n, pref, mult):
    if n <= pref:
        return n
    t = (pref // mult) * mult
    while t > mult and n % t:
        t -= mult
    assert n % t == 0, (n, pref, mult)
    return t


def _dot(a, b):
    return jnp.dot(a, b, preferred_element_type=F32)


def _dot_nt(a, b):
    return lax.dot_general(a, b, (((1,), (1,)), ((), ())), preferred_element_type=F32)


def _dot_tn(a, b):
    return lax.dot_general(a, b, (((0,), (0,)), ((), ())), preferred_element_type=F32)


def _silu(v):
    return v * jax.nn.sigmoid(v)


def _rms(v, gain):
    return v * lax.rsqrt(jnp.mean(v * v, axis=-1, keepdims=True) + EPS) * gain


def _ada_kernel(c_ref, w_ref, b_ref, o_ref):
    s = _silu(c_ref[...]).astype(BF16)
    o_ref[...] = _dot(s, w_ref[...].astype(BF16)) + b_ref[...]


def _ada_modulation(cond, w_ada, b_ada):
    rows, d = cond.shape
    n = w_ada.shape[1]
    tn = _tile(n, 512, 128)
    return pl.pallas_call(
        _ada_kernel,
        out_shape=jax.ShapeDtypeStruct((rows, n), F32),
        grid=(n // tn,),
        in_specs=[pl.BlockSpec((rows, d), lambda j: (0, 0)),
                  pl.BlockSpec((d, tn), lambda j: (0, j)),
                  pl.BlockSpec((1, tn), lambda j: (0, j))],
        out_specs=pl.BlockSpec((rows, tn), lambda j: (0, j)),
        compiler_params=_cp("arbitrary"),
        name="ada_modulation",
    )(cond, w_ada, b_ada.reshape(1, n))


def _norm_mod_kernel(nx, tiles_per_batch, ctx_row, x_ref, c_ref, g_ref, sh_ref, sc_ref, o_ref):
    i = pl.program_id(0)
    is_x = i < nx
    v = jnp.where(is_x, x_ref[...], c_ref[...])
    r = jnp.where(is_x, i // tiles_per_batch, ctx_row)
    shift = sh_ref[pl.ds(r, 1), :]
    scale = sc_ref[pl.ds(r, 1), :]
    o_ref[...] = (_rms(v, g_ref[...]) * (1.0 + scale) + shift).astype(o_ref.dtype)


def _norm_modulate(x2, c2, gain, shift_tab, scale_tab, seq, ctx_row):
    m, d = x2.shape
    mc = c2.shape[0]
    tm = _tile(_gcd(seq, mc), 256, 16)
    nx, ncx = m // tm, mc // tm
    return pl.pallas_call(
        functools.partial(_norm_mod_kernel, nx, seq // tm, ctx_row),
        out_shape=jax.ShapeDtypeStruct((m + mc, d), BF16),
        grid=(nx + ncx,),
        in_specs=[pl.BlockSpec((tm, d), lambda i: (jnp.minimum(i, nx - 1), 0)),
                  pl.BlockSpec((tm, d), lambda i: (jnp.maximum(i - nx, 0), 0)),
                  pl.BlockSpec((1, d), lambda i: (0, 0)),
                  pl.BlockSpec((MOD_ROWS, d), lambda i: (0, 0)),
                  pl.BlockSpec((MOD_ROWS, d), lambda i: (0, 0))],
        out_specs=pl.BlockSpec((tm, d), lambda i: (i, 0)),
        compiler_params=_cp("arbitrary"),
        name="norm_modulate",
    )(x2, c2, gain.reshape(1, d), shift_tab, scale_tab)


def _gcd(a, b):
    while b:
        a, b = b, a % b
    return a


def _mm_kernel(a_ref, w_ref, o_ref):
    o_ref[...] = _dot(a_ref[...], w_ref[...].astype(BF16)).astype(o_ref.dtype)


def _project(a, w, col0, ncols, m_rows, out_dtype):
    k = a.shape[1]
    tn = _tile(_gcd(ncols, col0) if col0 else ncols, 1024, 128)
    tm = _tile(m_rows, 512, 16)
    jb = col0 // tn
    return pl.pallas_call(
        _mm_kernel,
        out_shape=jax.ShapeDtypeStruct((m_rows, ncols), out_dtype),
        grid=(ncols // tn, m_rows // tm),
        in_specs=[pl.BlockSpec((tm, k), lambda j, i: (i, 0)),
                  pl.BlockSpec((k, tn), lambda j, i: (0, jb + j))],
        out_specs=pl.BlockSpec((tm, tn), lambda j, i: (i, j)),
        compiler_params=_cp("arbitrary", "arbitrary"),
        name="in_projection",
    )(a, w)


def _mm2_kernel(ka, a1_ref, a2_ref, w_ref, o_ref):
    w = w_ref[...].astype(BF16)
    o_ref[...] = _dot(a1_ref[...], w[:ka]) + _dot(a2_ref[...], w[ka:])


def _out_projection(a1, a2, w):
    m, ka = a1.shape
    kb = a2.shape[1]
    n = w.shape[1]
    tn = _tile(n, 1024, 128)
    tm = _tile(m, 512, 16)
    return pl.pallas_call(
        functools.partial(_mm2_kernel, ka),
        out_shape=jax.ShapeDtypeStruct((m, n), F32),
        grid=(n // tn, m // tm),
        in_specs=[pl.BlockSpec((tm, ka), lambda j, i: (i, 0)),
                  pl.BlockSpec((tm, kb), lambda j, i: (i, 0)),
                  pl.BlockSpec((ka + kb, tn), lambda j, i: (0, j))],
        out_specs=pl.BlockSpec((tm, tn), lambda j, i: (i, j)),
        compiler_params=_cp("arbitrary", "arbitrary"),
        name="out_projection",
    )(a1, a2, w)


def _rope(v, cos, sin_signed):
    lane = lax.broadcasted_iota(I32, v.shape, 1)
    partner = jnp.where((lane & 32) == 0, pltpu.roll(v, HEAD_DIM - 32, 1), pltpu.roll(v, 32, 1))
    return v * cos + partner * sin_signed


def _attn_kernel(tq, seq, ctx, q_ref, kl_ref, kc_ref, vl_ref, vc_ref, qg_ref, kg_ref, cos_ref, sin_ref,
                 o_ref, k_sc, v_sc):
    qi = pl.program_id(2)

    @pl.when(qi == 0)
    def _():
        kg = kg_ref[...]
        k_sc[0:ctx, :] = _rms(kc_ref[...].astype(F32), kg).astype(BF16)
        kl = _rope(_rms(kl_ref[...].astype(F32), kg), cos_ref[...], sin_ref[...])
        k_sc[ctx:ctx + seq, :] = kl.astype(BF16)
        v_sc[0:ctx, 0:HEAD_DIM] = vc_ref[...]
        v_sc[ctx:ctx + seq, 0:HEAD_DIM] = vl_ref[...]
        v_sc[:, HEAD_DIM:2 * HEAD_DIM] = jnp.ones((ctx + seq, HEAD_DIM), BF16)

    r0 = pl.multiple_of(qi * tq, tq)
    cos = cos_ref[pl.ds(r0, tq), :]
    sin = sin_ref[pl.ds(r0, tq), :]
    qg = qg_ref[...] * (HEAD_DIM ** -0.5 * LOG2_E)
    keys = k_sc[...]
    vals = v_sc[...]
    parts = [(slice(r * (tq // 2), (r + 1) * (tq // 2)), slice(h * HEAD_DIM, (h + 1) * HEAD_DIM))
             for r in range(2) for h in range(GQA_GROUP)]
    def score(part):
        rs, hs = part
        return _dot_nt(_rope(_rms(q_ref[rs, hs].astype(F32), qg), cos[rs], sin[rs]).astype(BF16), keys)

    def finish(part, s):
        rs, hs = part
        p = jnp.exp2(s - jnp.max(s, axis=-1, keepdims=True))
        ov = _dot(p.astype(BF16), vals)
        o_ref[rs, hs] = (ov[:, 0:HEAD_DIM] / ov[:, HEAD_DIM:HEAD_DIM + 1]).astype(o_ref.dtype)

    s_prev = score(parts[0])
    for i in range(1, len(parts)):
        s_next = score(parts[i])
        finish(parts[i - 1], s_prev)
        s_prev = s_next
    finish(parts[-1], s_prev)


def _attention(p_head, p_kv, q_gain, k_gain, cos, sin_signed, batch, seq, ctx, kv_heads):
    tq = _tile(seq, 512, 16)
    nq = seq // tq
    gw = GQA_GROUP * HEAD_DIM
    ctx_blk0 = batch * seq // ctx
    return pl.pallas_call(
        functools.partial(_attn_kernel, tq, seq, ctx),
        out_shape=jax.ShapeDtypeStruct((batch * seq, kv_heads * gw), BF16),
        grid=(batch, kv_heads, nq),
        in_specs=[pl.BlockSpec((tq, gw), lambda b, h, q: (b * nq + q, h)),
                  pl.BlockSpec((seq, HEAD_DIM), lambda b, h, q: (b, h)),
                  pl.BlockSpec((ctx, HEAD_DIM), lambda b, h, q: (ctx_blk0 + b, h)),
                  pl.BlockSpec((seq, HEAD_DIM), lambda b, h, q: (b, kv_heads + h)),
                  pl.BlockSpec((ctx, HEAD_DIM), lambda b, h, q: (ctx_blk0 + b, kv_heads + h)),
                  pl.BlockSpec((1, HEAD_DIM), lambda b, h, q: (0, 0)),
                  pl.BlockSpec((1, HEAD_DIM), lambda b, h, q: (0, 0)),
                  pl.BlockSpec((seq, HEAD_DIM), lambda b, h, q: (0, 0)),
                  pl.BlockSpec((seq, HEAD_DIM), lambda b, h, q: (0, 0))],
        out_specs=pl.BlockSpec((tq, gw), lambda b, h, q: (b * nq + q, h)),
        scratch_shapes=[pltpu.VMEM((ctx + seq, HEAD_DIM), BF16),
                        pltpu.VMEM((ctx + seq, 2 * HEAD_DIM), BF16)],
        compiler_params=_cp("arbitrary", "arbitrary", "arbitrary"),
        name="gqa_attention",
    )(p_head, p_kv, p_kv, p_kv, p_kv, q_gain.reshape(1, HEAD_DIM), k_gain.reshape(1, HEAD_DIM), cos, sin_signed)


def _hgrn_kernel(hpb, seq, ctx, rows, lb_ref, gain_ref, q_ref, g_ref, ff_ref, fb_ref, v_ref,
                 cff_ref, cfb_ref, cv_ref, o_ref, st_ref, of_ref, ob_ref):
    cpb = rows // CHUNK
    lg_chunk = CHUNK.bit_length() - 1
    row = lax.broadcasted_iota(I32, (rows, rows), 0)
    col = lax.broadcasted_iota(I32, (rows, rows), 1)
    same = (row >> lg_chunk) == (col >> lg_chunk)
    masks = (same & (col <= row), same & (col >= row))
    tris = tuple(jnp.where(m, 1.0, 0.0).astype(BF16) for m in masks)
    row_chunk = lax.broadcasted_iota(I32, (rows, HGRN_EXPAND), 0) >> lg_chunk

    st_ref[...] = jnp.zeros(st_ref.shape, F32)

    def lane_stack(a):
        return jnp.concatenate([jnp.where(row_chunk == c, a, 0.0).astype(BF16) for c in range(cpb)], axis=1)

    def chunk_rows(b, off):
        return jnp.concatenate([jnp.broadcast_to(b[c * CHUNK + off:c * CHUNK + off + 1, :], (CHUNK, HGRN_EXPAND))
                                for c in range(cpb)], axis=0)

    def blocks(items, with_out):
        n = len(items)
        last = [0 if it[5] else CHUNK - 1 for it in items]
        mid = [CHUNK // 2 if it[5] else CHUNK // 2 - 1 for it in items]
        ks, splits = [], []
        for _, raw, _, _, lb, _ in items:
            f = lb + (1.0 - lb) * jax.nn.sigmoid(raw)
            logf = jnp.log(f)
            hi = logf.astype(BF16)
            ks.append(1.0 - f)
            splits.append((hi, (logf - hi.astype(F32)).astype(BF16)))
        bs = [_dot(tris[items[j][5]], splits[j][0]) + _dot(tris[items[j][5]], splits[j][1]) for j in range(n)]
        uts = [_dot_tn(items[j][2], lane_stack(ks[j] * jnp.exp(chunk_rows(bs[j], last[j]) - bs[j]))) for j in range(n)]
        slabs = []
        for j in range(n):
            st = st_ref[items[j][0]]
            starts = [None] * cpb
            for c in (reversed(range(cpb)) if items[j][5] else range(cpb)):
                starts[c] = st.astype(BF16)
                st = (st * jnp.exp(bs[j][c * CHUNK + last[j]:c * CHUNK + last[j] + 1, :])
                      + uts[j][:, c * HGRN_EXPAND:(c + 1) * HGRN_EXPAND])
            st_ref[items[j][0]] = st
            slabs.append(jnp.concatenate(starts, axis=1))
        if not with_out:
            return None
        scores = []
        for j in range(n):
            b_mid = chunk_rows(bs[j], mid[j])
            qt = (items[j][3] * jnp.exp(bs[j] - b_mid)).astype(BF16)
            kt = (ks[j] * jnp.exp(b_mid - bs[j])).astype(BF16)
            scores.append(_dot_nt(qt, kt))
        outs = []
        for j in range(n):
            a = jnp.where(masks[items[j][5]], scores[j], 0.0).astype(BF16)
            inter = _dot_nt(lane_stack(items[j][3] * jnp.exp(bs[j])), slabs[j])
            outs.append(_dot(a, items[j][2]) + inter)
        return outs

    def ctx_body(i, carry):
        r_f = pl.multiple_of(i * rows, rows)
        r_b = pl.multiple_of((ctx // rows - 1 - i) * rows, rows)
        items = []
        for g in range(hpb):
            sl = slice(g * HGRN_EXPAND, (g + 1) * HGRN_EXPAND)
            items.append((2 * g, cff_ref[pl.ds(r_f, rows), sl], cv_ref[pl.ds(r_f, rows), sl], None, lb_ref[0:1, sl], 0))
            items.append((2 * g + 1, cfb_ref[pl.ds(r_b, rows), sl], cv_ref[pl.ds(r_b, rows), sl], None,
                          lb_ref[1:2, sl], 1))
        blocks(items, False)
        return carry

    lax.fori_loop(0, ctx // rows, ctx_body, 0)

    def seq_body(i, carry):
        r_f = pl.multiple_of(i * rows, rows)
        r_b = pl.multiple_of((seq // rows - 1 - i) * rows, rows)
        items, dests = [], []
        for g in range(hpb):
            sl = slice(g * HGRN_EXPAND, (g + 1) * HGRN_EXPAND)
            for rev, r, f_ref, out_ref in ((0, r_f, ff_ref, of_ref), (1, r_b, fb_ref, ob_ref)):
                q = _silu(q_ref[pl.ds(r, rows), sl].astype(F32))
                items.append((2 * g + rev, f_ref[pl.ds(r, rows), sl], v_ref[pl.ds(r, rows), sl], q,
                              lb_ref[rev:rev + 1, sl], rev))
                dests.append((out_ref, r, sl))
        for (out_ref, r, sl), o in zip(dests, blocks(items, True)):
            out_ref[pl.ds(r, rows), sl] = o
        return carry

    lax.fori_loop(0, seq // rows, seq_body, 0)

    rb = _tile(seq, 256, 8)

    def read_body(i, carry):
        r = pl.multiple_of(i * rb, rb)
        for g in range(hpb):
            sl = slice(g * HGRN_EXPAND, (g + 1) * HGRN_EXPAND)
            o = of_ref[pl.ds(r, rb), sl] + ob_ref[pl.ds(r, rb), sl]
            y = _rms(o, gain_ref[...]) * _silu(g_ref[pl.ds(r, rb), sl].astype(F32))
            o_ref[pl.ds(r, rb), sl] = y.astype(o_ref.dtype)
        return carry

    lax.fori_loop(0, seq // rb, read_body, 0)


def _hgrn(p_head, p_f, p_v, lower_bounds, hg_gain, batch, seq, ctx, attn_width, hgrn_width):
    heads = hgrn_width // HGRN_EXPAND
    hpb = 4 if heads % 4 == 0 else (2 if heads % 2 == 0 else 1)
    bw = hpb * HGRN_EXPAND
    nh = heads // hpb
    q0, g0 = attn_width // bw, (attn_width + hgrn_width) // bw
    ctx_blk0 = batch * seq // ctx
    rows = _tile(_gcd(seq, ctx), 256, CHUNK)
    lat = lambda c0: pl.BlockSpec((seq, bw), lambda b, h: (b, c0 + h))
    cx = lambda c0: pl.BlockSpec((ctx, bw), lambda b, h: (ctx_blk0 + b, c0 + h))
    return pl.pallas_call(
        functools.partial(_hgrn_kernel, hpb, seq, ctx, rows),
        out_shape=jax.ShapeDtypeStruct((batch * seq, hgrn_width), BF16),
        grid=(batch, nh),
        in_specs=[pl.BlockSpec((2, bw), lambda b, h: (0, h)),
                  pl.BlockSpec((1, HGRN_EXPAND), lambda b, h: (0, 0)),
                  lat(q0), lat(g0), lat(0), lat(nh), lat(0), cx(0), cx(nh), cx(0)],
        out_specs=pl.BlockSpec((seq, bw), lambda b, h: (b, h)),
        scratch_shapes=[pltpu.VMEM((2 * hpb, HGRN_EXPAND, HGRN_EXPAND), F32),
                        pltpu.VMEM((seq, bw), F32),
                        pltpu.VMEM((seq, bw), F32)],
        compiler_params=_cp("arbitrary", "arbitrary"),
        name="hgrn2_scan",
    )(lower_bounds, hg_gain.reshape(1, HGRN_EXPAND), p_head, p_head, p_f, p_f, p_v, p_f, p_f, p_v)


def _epilogue_kernel(tiles_per_batch, x_ref, y_ref, g1n_ref, g2n_ref, gate_ref, sh_ref, sc_ref, wr_ref, br_ref,
                     x1_ref, hp_ref, lg_ref):
    r = pl.program_id(0) // tiles_per_batch
    x1 = x_ref[...] + gate_ref[pl.ds(r, 1), :] * _rms(y_ref[...], g1n_ref[...])
    x1_ref[...] = x1
    h = _rms(x1, g2n_ref[...]) * (1.0 + sc_ref[pl.ds(r, 1), :]) + sh_ref[pl.ds(r, 1), :]
    hp_ref[...] = h
    w = wr_ref[...]
    w_hi = w.astype(BF16)
    w_lo = (w - w_hi.astype(F32)).astype(BF16)
    h_hi = h.astype(BF16)
    h_lo = (h - h_hi.astype(F32)).astype(BF16)
    lg_ref[...] = _dot_nt(w_hi, h_hi) + _dot_nt(w_hi, h_lo) + _dot_nt(w_lo, h_hi) + br_ref[...]


def _epilogue(x2, y, gain1, gain2, gate_tab, shift_tab, scale_tab, w_router_t, b_router, seq):
    m, d = x2.shape
    ne = w_router_t.shape[0]
    tm = _tile(seq, 256, 128)
    row = lambda: pl.BlockSpec((tm, d), lambda i: (i, 0))
    tab = lambda: pl.BlockSpec((MOD_ROWS, d), lambda i: (0, 0))
    vec = lambda: pl.BlockSpec((1, d), lambda i: (0, 0))
    return pl.pallas_call(
        functools.partial(_epilogue_kernel, seq // tm),
        out_shape=(jax.ShapeDtypeStruct((m, d), F32),
                   jax.ShapeDtypeStruct((m, d), F32),
                   jax.ShapeDtypeStruct((ne, m), F32)),
        grid=(m // tm,),
        in_specs=[row(), row(), vec(), vec(), tab(), tab(), tab(),
                  pl.BlockSpec((ne, d), lambda i: (0, 0)),
                  pl.BlockSpec((ne, 1), lambda i: (0, 0))],
        out_specs=(row(), row(), pl.BlockSpec((ne, tm), lambda i: (0, i))),
        compiler_params=_cp("arbitrary"),
        name="residual_norm_router",
    )(x2, y, gain1.reshape(1, d), gain2.reshape(1, d), gate_tab, shift_tab, scale_tab,
      w_router_t, b_router.reshape(ne, 1))


def _route_kernel(lg_ref, idx_ref, wt_ref, pos_ref, cnt_ref, carry_ref):
    i = pl.program_id(0)
    ne, tn = lg_ref.shape

    @pl.when(i == 0)
    def _():
        carry_ref[...] = jnp.zeros(carry_ref.shape, F32)

    lg = lg_ref[...]
    eidx = lax.broadcasted_iota(I32, (ne, tn), 0)
    vals, hots = [], []
    for k in range(TOP_K):
        m = jnp.max(lg, axis=0, keepdims=True)
        sel = jnp.min(jnp.where(lg == m, eidx, ne), axis=0, keepdims=True)
        hot = eidx == sel
        idx_ref[k:k + 1, :] = sel
        vals.append(m)
        hots.append(hot)
        lg = jnp.where(hot, -jnp.inf, lg)
    exps = [jnp.exp(v - vals[0]) for v in vals]
    denom = exps[0] + exps[1] + exps[2] + exps[3]
    for k in range(TOP_K):
        wt_ref[k:k + 1, :] = exps[k] / denom

    member = hots[0] | hots[1] | hots[2] | hots[3]
    t_row = lax.broadcasted_iota(I32, (tn, tn), 0)
    t_col = lax.broadcasted_iota(I32, (tn, tn), 1)
    before = (t_row < t_col).astype(BF16)
    rank = _dot(member.astype(BF16), before) + carry_ref[:, 0:1]
    for k in range(TOP_K):
        pos_ref[k:k + 1, :] = jnp.sum(jnp.where(hots[k], rank, 0.0), axis=0, keepdims=True).astype(I32)
    carry_ref[...] = carry_ref[...] + jnp.sum(member.astype(F32), axis=1, keepdims=True)
    cnt_ref[...] = carry_ref[...].astype(I32)


def _route(logits_t):
    ne, m = logits_t.shape
    tn = _tile(m, 512, 128)
    out4 = lambda: pl.BlockSpec((TOP_K, tn), lambda i: (0, i))
    return pl.pallas_call(
        _route_kernel,
        out_shape=(jax.ShapeDtypeStruct((TOP_K, m), I32),
                   jax.ShapeDtypeStruct((TOP_K, m), F32),
                   jax.ShapeDtypeStruct((TOP_K, m), I32),
                   jax.ShapeDtypeStruct((ne, 128), I32)),
        grid=(m // tn,),
        in_specs=[pl.BlockSpec((ne, tn), lambda i: (0, i))],
        out_specs=(out4(), out4(), out4(), pl.BlockSpec((ne, 128), lambda i: (0, 0))),
        scratch_shapes=[pltpu.VMEM((ne, 128), F32)],
        compiler_params=_cp("arbitrary"),
        name="top4_route",
    )(logits_t)


def _slots_kernel(row_tile, lg_tile, idx_ref, pos_ref, cnt_ref, slot_ref, te_ref, start_ref, ends_ref):
    ne = cnt_ref.shape[0]
    tn = idx_ref.shape[1]
    tiles = (cnt_ref[...] + (row_tile - 1)) >> lg_tile
    e_row = lax.broadcasted_iota(I32, (ne, ne), 0)
    e_col = lax.broadcasted_iota(I32, (ne, ne), 1)
    upto = jnp.where(e_col <= e_row, 1.0, 0.0).astype(BF16)
    ends = _dot(upto, tiles.astype(F32).astype(BF16)).astype(I32)
    start = (ends - tiles) << lg_tile
    start_ref[...] = start
    ends_ref[...] = ends
    eidx = lax.broadcasted_iota(I32, (ne, tn), 0)
    start_col = start[:, 0:1]
    for k in range(TOP_K):
        hit = eidx == idx_ref[k:k + 1, :]
        slot_ref[k:k + 1, :] = jnp.sum(jnp.where(hit, start_col, 0), axis=0, keepdims=True) + pos_ref[k:k + 1, :]
    tile_i = lax.broadcasted_iota(I32, (ne, te_ref.shape[1]), 1)
    te = jnp.sum(jnp.where(tile_i >= ends[:, 0:1], 1, 0), axis=0, keepdims=True)
    te_ref[...] = jnp.minimum(te, ne - 1)


def _slots(idx, pos, cnt, row_tile, n_tiles):
    ne = cnt.shape[0]
    m = idx.shape[1]
    tn = _tile(m, 2048, 128)
    ntp = -(-n_tiles // 128) * 128
    lg_tile = row_tile.bit_length() - 1
    assert 1 << lg_tile == row_tile and -(-m // row_tile) <= 256
    io4 = lambda: pl.BlockSpec((TOP_K, tn), lambda i: (0, i))
    small = lambda w: pl.BlockSpec((ne, w), lambda i: (0, 0))
    return pl.pallas_call(
        functools.partial(_slots_kernel, row_tile, lg_tile),
        out_shape=(jax.ShapeDtypeStruct((TOP_K, m), I32),
                   jax.ShapeDtypeStruct((1, ntp), I32),
                   jax.ShapeDtypeStruct((ne, 128), I32),
                   jax.ShapeDtypeStruct((ne, 128), I32)),
        grid=(m // tn,),
        in_specs=[io4(), io4(), small(128)],
        out_specs=(io4(), pl.BlockSpec((1, ntp), lambda i: (0, 0)), small(128), small(128)),
        compiler_params=_cp("arbitrary"),
        name="route_slots",
    )(idx, pos, cnt)


DMA_ISSUE_UNROLL = 8


def _inverse_kernel(tm, row_tile, ne, n_slots, start_ref, cnt_ref, nu_ref, slot_ref, tok_ref):
    i = pl.program_id(0)

    @pl.when(i == 0)
    def _():
        def zero(j, c):
            tok_ref[j] = 0
            return c

        for e in range(ne):
            first = start_ref[e] + cnt_ref[e]
            lax.fori_loop(first, first + ((-cnt_ref[e]) & (row_tile - 1)), zero, 0)
        lax.fori_loop(nu_ref[0] * row_tile, n_slots, zero, 0)

    def body(g, c):
        for u in range(DMA_ISSUE_UNROLL):
            r = g * DMA_ISSUE_UNROLL + u
            for k in range(TOP_K):
                tok_ref[slot_ref[0, 0, k * tm + r]] = i * tm + r
        return c

    lax.fori_loop(0, tm // DMA_ISSUE_UNROLL, body, 0)


def _inverse_slots(starts, counts, n_used, slot_tiles, n_slots, tm, row_tile):
    nt = slot_tiles.shape[0]
    return pl.pallas_call(
        functools.partial(_inverse_kernel, tm, row_tile, starts.shape[0], n_slots),
        out_shape=jax.ShapeDtypeStruct((n_slots,), I32),
        grid_spec=pltpu.PrefetchScalarGridSpec(
            num_scalar_prefetch=3,
            grid=(nt,),
            in_specs=[pl.BlockSpec((1, 1, TOP_K * tm), lambda i, st, ct, nu: (i, 0, 0), memory_space=pltpu.SMEM)],
            out_specs=pl.BlockSpec(memory_space=pltpu.SMEM)),
        compiler_params=_cp("arbitrary"),
        name="route_inverse",
    )(starts, counts, n_used, slot_tiles)


def _dispatch_kernel(tm, nu_ref, te_ref, start_ref, cnt_ref, tok_ref, next_ref, src_hbm, o_ref, buf, sem):
    i = pl.program_id(0)
    n_used = nu_ref[0]
    n_tiles = pl.num_programs(0)

    def row_groups(t):
        e = te_ref[jnp.minimum(t, n_tiles - 1)]
        rows = jnp.clip(start_ref[e] + cnt_ref[e] - t * tm, 0, tm)
        return (rows + (DMA_ISSUE_UNROLL - 1)) // DMA_ISSUE_UNROLL

    def issue(t_ref, t, b):
        def body(g, c):
            for u in range(DMA_ISSUE_UNROLL):
                r = g * DMA_ISSUE_UNROLL + u
                pltpu.make_async_copy(src_hbm.at[pl.ds(t_ref[0, 0, r], 1)], buf.at[b, pl.ds(r, 1)],
                                      sem.at[b]).start(priority=u % 2)
            return c

        lax.fori_loop(0, row_groups(t), body, 0)

    @pl.when(i == 0)
    def _():
        buf[...] = jnp.zeros(buf.shape, buf.dtype)
        issue(tok_ref, 0, 0)

    @pl.when(i + 1 < n_used)
    def _():
        issue(next_ref, i + 1, (i + 1) % 2)

    @pl.when(i < n_used)
    def _():
        b = i % 2

        def wait_group(g, c):
            pltpu.make_async_copy(src_hbm.at[pl.ds(0, DMA_ISSUE_UNROLL)], buf.at[b, pl.ds(0, DMA_ISSUE_UNROLL)],
                                  sem.at[b]).wait()
            return c

        lax.fori_loop(0, row_groups(i), wait_group, 0)
        o_ref[...] = buf[b].astype(o_ref.dtype)

    @pl.when(i >= n_used)
    def _():
        o_ref[...] = jnp.zeros(o_ref.shape, o_ref.dtype)


def _dispatch(n_used, tile_expert, starts, counts, tok_of_slot, src, tm):
    w = src.shape[1]
    nt = tok_of_slot.shape[0] // tm
    tok_tiles = tok_of_slot.reshape(nt, 1, tm)
    tok_spec = lambda f: pl.BlockSpec((1, 1, tm), f, memory_space=pltpu.SMEM)
    return pl.pallas_call(
        functools.partial(_dispatch_kernel, tm),
        out_shape=jax.ShapeDtypeStruct((nt * tm, w), BF16),
        grid_spec=pltpu.PrefetchScalarGridSpec(
            num_scalar_prefetch=4,
            grid=(nt,),
            in_specs=[tok_spec(lambda i, *_: (i, 0, 0)),
                      tok_spec(lambda i, *_: (jnp.minimum(i + 1, nt - 1), 0, 0)),
                      pl.BlockSpec(memory_space=pl.ANY)],
            out_specs=pl.BlockSpec((tm, w), lambda i, *_: (i, 0)),
            scratch_shapes=[pltpu.VMEM((2, tm, w), src.dtype), pltpu.SemaphoreType.DMA((2,))]),
        compiler_params=_cp("arbitrary"),
        name="moe_dispatch",
    )(n_used, tile_expert, starts, counts, tok_tiles, tok_tiles, src)


def _stream_expert_weights(n_pass, n_tiles, width, te_ref, nu_ref, grp_ref, sem, weights):
    c, i = pl.program_id(0), pl.program_id(1)
    n_used = nu_ref[0]

    def copies(e, cc, slot):
        col = pl.multiple_of(cc * width, width)
        return [pltpu.make_async_copy(hbm.at[e, :, pl.ds(col, width)], buf.at[slot], sem.at[j, slot])
                for j, (hbm, buf) in enumerate(weights)]

    @pl.when((c == 0) & (i == 0))
    def _():
        grp_ref[0] = 0
        for cp in copies(te_ref[0], 0, 0):
            cp.start()

    e = te_ref[i]
    first = (i < n_used) & ((i == 0) | (e != te_ref[jnp.maximum(i - 1, 0)]))

    @pl.when(first)
    def _():
        grp = grp_ref[0]
        slot = grp % 2
        for cp in copies(e, c, slot):
            cp.wait()
        nxt = lax.while_loop(lambda j: (j < n_used) & (te_ref[jnp.minimum(j, n_tiles - 1)] == e),
                             lambda j: j + 1, i + 1)
        more = nxt < n_used

        @pl.when(more)
        def _():
            for cp in copies(te_ref[jnp.minimum(nxt, n_tiles - 1)], c, 1 - slot):
                cp.start(priority=1)

        @pl.when(jnp.logical_not(more) & (c + 1 < n_pass))
        def _():
            for cp in copies(te_ref[0], c + 1, 1 - slot):
                cp.start(priority=1)

        grp_ref[0] = grp + 1

    return (grp_ref[0] + 1) % 2


def _expert_up_kernel(n_pass, n_tiles, te_ref, nu_ref, x_ref, wg_hbm, wu_hbm, bg_ref, bu_ref, h_ref,
                      wg_buf, wu_buf, sem, grp_ref):
    slot = _stream_expert_weights(n_pass, n_tiles, h_ref.shape[1], te_ref, nu_ref, grp_ref, sem,
                                  [(wg_hbm, wg_buf), (wu_hbm, wu_buf)])

    @pl.when(pl.program_id(1) < nu_ref[0])
    def _():
        x = x_ref[...]

        def proj(w_buf, b_ref):
            return _dot(x, w_buf[slot].astype(BF16)) + b_ref[0]

        g = jnp.minimum(proj(wg_buf, bg_ref), SWIGLU_LIMIT)
        u = jnp.clip(proj(wu_buf, bu_ref), -SWIGLU_LIMIT, SWIGLU_LIMIT)
        h_ref[...] = (g * jax.nn.sigmoid(SWIGLU_ALPHA * g) * (u + 1.0)).astype(h_ref.dtype)

    @pl.when(pl.program_id(1) >= nu_ref[0])
    def _():
        h_ref[...] = jnp.zeros(h_ref.shape, h_ref.dtype)


def _used_tile(i, nu):
    return jnp.minimum(i, nu[0] - 1)


def _expert_up(tile_expert, n_used, xs, w_gate, w_up, b_gate, b_up, tm):
    s_pad = xs.shape[0]
    ne, d, f = w_gate.shape
    fc = _tile(f, 512, 128)
    nt = s_pad // tm
    wspec = lambda: pl.BlockSpec(memory_space=pl.ANY)
    bspec = lambda: pl.BlockSpec((1, 1, fc), lambda c, i, te, nu: (te[i], 0, c))
    return pl.pallas_call(
        functools.partial(_expert_up_kernel, f // fc, nt),
        out_shape=jax.ShapeDtypeStruct((s_pad, f), BF16),
        grid_spec=pltpu.PrefetchScalarGridSpec(
            num_scalar_prefetch=2,
            grid=(f // fc, nt),
            in_specs=[pl.BlockSpec((tm, d), lambda c, i, te, nu: (_used_tile(i, nu), 0)),
                      wspec(), wspec(), bspec(), bspec()],
            out_specs=pl.BlockSpec((tm, fc), lambda c, i, te, nu: (i, c)),
            scratch_shapes=[pltpu.VMEM((2, d, fc), F32), pltpu.VMEM((2, d, fc), F32),
                            pltpu.SemaphoreType.DMA((2, 2)), pltpu.SMEM((1,), I32)]),
        compiler_params=_cp("arbitrary", "arbitrary"),
        name="expert_gate_up",
    )(tile_expert, n_used, xs, w_gate, w_up, b_gate.reshape(ne, 1, f), b_up.reshape(ne, 1, f))


def _expert_down_kernel(n_pass, n_tiles, te_ref, nu_ref, h_ref, wd_hbm, bd_ref, y_ref, wd_buf, sem, grp_ref):
    slot = _stream_expert_weights(n_pass, n_tiles, y_ref.shape[1], te_ref, nu_ref, grp_ref, sem, [(wd_hbm, wd_buf)])

    @pl.when(pl.program_id(1) < nu_ref[0])
    def _():
        y_ref[...] = _dot(h_ref[...], wd_buf[slot].astype(BF16)) + bd_ref[0]

    @pl.when(pl.program_id(1) >= nu_ref[0])
    def _():
        y_ref[...] = jnp.zeros(y_ref.shape, y_ref.dtype)


def _expert_down(tile_expert, n_used, h, w_down, b_down, tm, tn):
    s_pad, f = h.shape
    ne, _, d = w_down.shape
    nt = s_pad // tm
    return pl.pallas_call(
        functools.partial(_expert_down_kernel, d // tn, nt),
        out_shape=jax.ShapeDtypeStruct((s_pad, d), F32),
        grid_spec=pltpu.PrefetchScalarGridSpec(
            num_scalar_prefetch=2,
            grid=(d // tn, nt),
            in_specs=[pl.BlockSpec((tm, f), lambda c, i, te, nu: (_used_tile(i, nu), 0)),
                      pl.BlockSpec(memory_space=pl.ANY),
                      pl.BlockSpec((1, 1, tn), lambda c, i, te, nu: (te[i], 0, c))],
            out_specs=pl.BlockSpec((tm, tn), lambda c, i, te, nu: (i, c)),
            scratch_shapes=[pltpu.VMEM((2, f, tn), F32), pltpu.SemaphoreType.DMA((1, 2)),
                            pltpu.SMEM((1,), I32)]),
        compiler_params=_cp("arbitrary", "arbitrary"),
        name="expert_down",
    )(tile_expert, n_used, h, w_down, b_down.reshape(ne, 1, d))


def _combine_kernel(tm, tiles_per_batch, nt, slot_ref, next_ref, yp_hbm, wt_ref, x1_ref, gn_ref, gate_ref,
                    o_ref, buf, sem):
    i = pl.program_id(0)

    def issue(s_ref, b):
        def body(g, c):
            for u in range(DMA_ISSUE_UNROLL):
                r = g * DMA_ISSUE_UNROLL + u
                for k in range(TOP_K):
                    pltpu.make_async_copy(yp_hbm.at[pl.ds(s_ref[0, 0, k * tm + r], 1)],
                                          buf.at[b, k, pl.ds(r, 1)], sem.at[b]).start(priority=k % 2)
            return c

        lax.fori_loop(0, tm // DMA_ISSUE_UNROLL, body, 0)

    @pl.when(i == 0)
    def _():
        issue(slot_ref, 0)

    @pl.when(i + 1 < nt)
    def _():
        issue(next_ref, (i + 1) % 2)

    b = i % 2
    for k in range(TOP_K):
        pltpu.make_async_copy(yp_hbm.at[pl.ds(0, tm)], buf.at[b, k], sem.at[b]).wait()

    wt = wt_ref[...]
    moe = wt[:, 0:1] * buf[b, 0]
    for k in range(1, TOP_K):
        moe = moe + wt[:, k:k + 1] * buf[b, k]
    r = i // tiles_per_batch
    o_ref[...] = x1_ref[...] + gate_ref[pl.ds(r, 1), :] * _rms(moe, gn_ref[...])


def _combine(slot_tiles, yp, wts, x1, gain, gate_tab, seq, tm):
    m, d = x1.shape
    nt = m // tm
    slot_spec = lambda f: pl.BlockSpec((1, 1, TOP_K * tm), f, memory_space=pltpu.SMEM)
    return pl.pallas_call(
        functools.partial(_combine_kernel, tm, seq // tm, nt),
        out_shape=jax.ShapeDtypeStruct((m, d), F32),
        grid=(nt,),
        in_specs=[slot_spec(lambda i: (i, 0, 0)),
                  slot_spec(lambda i: (jnp.minimum(i + 1, nt - 1), 0, 0)),
                  pl.BlockSpec(memory_space=pl.ANY),
                  pl.BlockSpec((tm, TOP_K), lambda i: (i, 0)),
                  pl.BlockSpec((tm, d), lambda i: (i, 0)),
                  pl.BlockSpec((1, d), lambda i: (0, 0)),
                  pl.BlockSpec((MOD_ROWS, d), lambda i: (0, 0))],
        out_specs=pl.BlockSpec((tm, d), lambda i: (i, 0)),
        scratch_shapes=[pltpu.VMEM((2, TOP_K, tm, d), F32), pltpu.SemaphoreType.DMA((2,))],
        compiler_params=_cp("arbitrary"),
        name="moe_combine",
    )(slot_tiles, slot_tiles, yp, wts, x1, gain.reshape(1, d), gate_tab)


def _rope_tables(seq):
    rows = seq // GRID_W
    axis_dim = HEAD_DIM // 2
    row_ids = jnp.repeat(jnp.arange(rows), GRID_W).astype(F32)
    col_ids = jnp.tile(jnp.arange(GRID_W), rows).astype(F32)
    inv_freq = ROPE_THETA ** (-jnp.arange(0, axis_dim, 2, dtype=F32) / axis_dim)
    ang_r = row_ids[:, None] * inv_freq[None, :]
    ang_c = col_ids[:, None] * inv_freq[None, :]
    cos = jnp.concatenate([jnp.cos(ang_r)] * 2 + [jnp.cos(ang_c)] * 2, axis=1)
    sin = jnp.concatenate([-jnp.sin(ang_r), jnp.sin(ang_r), -jnp.sin(ang_c), jnp.sin(ang_c)], axis=1)
    return cos, sin


def _layer(x, xc, c, c_ctx, lower_bounds, w_ada, b_ada, gains, w_in, q_gain, k_gain, hg_gain, w_out,
           w_router, b_router, w_gate, b_gate, w_up, b_up, w_down, b_down):
    batch, seq, d = x.shape
    ctx = xc.shape[1]
    attn_width = d // 2
    hgrn_width = d // 2
    kv_heads = attn_width // HEAD_DIM // GQA_GROUP
    kv_width = kv_heads * HEAD_DIM
    head_cols = attn_width + 2 * hgrn_width
    n_experts = w_router.shape[1]
    m, mc = batch * seq, batch * ctx
    assert batch < MOD_ROWS and seq % ctx == 0

    cond = jnp.concatenate([c, c_ctx[None, :], jnp.zeros((MOD_ROWS - batch - 1, d), F32)], axis=0)
    mod = _ada_modulation(cond, w_ada, b_ada)
    sh1, sc1, g1, sh2, sc2, g2 = [mod[:, i * d:(i + 1) * d] for i in range(N_MOD)]

    x2 = x.reshape(m, d)
    hh = _norm_modulate(x2, xc.reshape(mc, d), gains[0], sh1, sc1, seq, batch)

    p_head = _project(hh, w_in, 0, head_cols, m, BF16)
    p_kv = _project(hh, w_in, head_cols, 2 * kv_width, m + mc, BF16)
    p_f = _project(hh, w_in, head_cols + 2 * kv_width, 2 * hgrn_width, m + mc, F32)
    p_v = _project(hh, w_in, head_cols + 2 * kv_width + 2 * hgrn_width, hgrn_width, m + mc, BF16)

    cos, sin = _rope_tables(seq)
    attn = _attention(p_head, p_kv, q_gain, k_gain, cos, sin, batch, seq, ctx, kv_heads)
    hgrn = _hgrn(p_head, p_f, p_v, lower_bounds, hg_gain, batch, seq, ctx, attn_width, hgrn_width)
    y = _out_projection(attn, hgrn, w_out)

    x1, h2, logits_t = _epilogue(x2, y, gains[1], gains[2], g1, sh2, sc2, w_router.T, b_router, seq)
    idx, wts, pos, cnt = _route(logits_t)

    row_tile = 256
    n_tiles = -(-(TOP_K * m + n_experts * (row_tile - 1)) // row_tile)
    slots, tile_expert, starts, ends = _slots(idx, pos, cnt, row_tile, n_tiles)
    tile_expert = tile_expert[0, :n_tiles]
    n_used = ends[n_experts - 1, 0:1]
    starts, counts = starts[:, 0], cnt[:, 0]

    tok_tile = _tile(seq, 128, 8)
    nt = m // tok_tile
    slot_tiles = slots.reshape(TOP_K, nt, tok_tile).transpose(1, 0, 2).reshape(nt, 1, TOP_K * tok_tile)
    tok_of_slot = _inverse_slots(starts, counts, n_used, slot_tiles, n_tiles * row_tile, tok_tile, row_tile)
    xs = _dispatch(n_used, tile_expert, starts, counts, tok_of_slot, h2, row_tile)
    hmid = _expert_up(tile_expert, n_used, xs, w_gate, w_up, b_gate, b_up, row_tile)
    ye = _expert_down(tile_expert, n_used, hmid, w_down, b_down, row_tile, _tile(d, 2048, 256))
    out = _combine(slot_tiles, ye, wts.T, x1, gains[3], g2, seq, tok_tile)
    return out.reshape(batch, seq, d)


def kernel(x, c, ctx, c_ctx, w_ada, b_ada, norm_gains, w_in, q_norm_gain, k_norm_gain, hgrn_lb_logits,
           hgrn_norm_gain, w_out, w_router, b_router, w_gate, b_gate, w_up, b_up, w_down, b_down):
    assert w_ada.shape[0] == 1, "single-layer stack only"
    lower_bounds = jnp.cumsum(jax.nn.softmax(hgrn_lb_logits.astype(F32), axis=0), axis=0)
    return _layer(x, ctx, c, c_ctx, lower_bounds[0], w_ada[0], b_ada[0], norm_gains[0], w_in[0],
                  q_norm_gain[0], k_norm_gain[0], hgrn_norm_gain[0], w_out[0], w_router[0], b_router[0],
                  w_gate[0], b_gate[0], w_up[0], b_up[0], w_down[0], b_down[0])
```

```python
import functools

import jax
import jax.numpy as jnp
from jax import lax
from jax.experimental import pallas as pl
from jax.experimental.pallas import tpu as pltpu

F32 = jnp.float32
BF16 = jnp.bfloat16
U32 = jnp.uint32
I32 = jnp.int32

GRID_W = 64
N_MOD = 6
EPS = 1e-6
HEAD_DIM = 128
GQA_GROUP = 4
ROPE_THETA = 10000.0
LOG2_E = 1.4426950408889634
ATTN_PROBLEM_ROWS = 256
HGRN_EXPAND = 128
CHUNK = 64
TOP_K = 4
SWIGLU_LIMIT = 7.0
SWIGLU_ALPHA = 1.702
MOD_ROWS = 8

V7X_VMEM_LIMIT = 56 * 1024 * 1024


def _cp(*sem):
    return pltpu.CompilerParams(dimension_semantics=sem, vmem_limit_bytes=V7X_VMEM_LIMIT)


def _tile(n, pref, mult):
    if n <= pref:
        return n
    t = (pref // mult) * mult
    while t > mult and n % t:
        t -= mult
    assert n % t == 0, (n, pref, mult)
    return t


def _dot(a, b):
    return jnp.dot(a, b, preferred_element_type=F32)


def _dot_nt(a, b):
    return lax.dot_general(a, b, (((1,), (1,)), ((), ())), preferred_element_type=F32)


def _dot_tn(a, b):
    return lax.dot_general(a, b, (((0,), (0,)), ((), ())), preferred_element_type=F32)


def _silu(v):
    return v * jax.nn.sigmoid(v)


def _rms(v, gain):
    return v * lax.rsqrt(jnp.mean(v * v, axis=-1, keepdims=True) + EPS) * gain


def _ada_kernel(c_ref, w_ref, b_ref, o_ref):
    s = _silu(c_ref[...]).astype(BF16)
    o_ref[...] = _dot(s, w_ref[...].astype(BF16)) + b_ref[...]


def _ada_modulation(cond, w_ada, b_ada):
    rows, d = cond.shape
    n = w_ada.shape[1]
    tn = _tile(n, 512, 128)
    return pl.pallas_call(
        _ada_kernel,
        out_shape=jax.ShapeDtypeStruct((rows, n), F32),
        grid=(n // tn,),
        in_specs=[pl.BlockSpec((rows, d), lambda j: (0, 0)),
                  pl.BlockSpec((d, tn), lambda j: (0, j)),
                  pl.BlockSpec((1, tn), lambda j: (0, j))],
        out_specs=pl.BlockSpec((rows, tn), lambda j: (0, j)),
        compiler_params=_cp("arbitrary"),
        name="ada_modulation",
    )(cond, w_ada, b_ada.reshape(1, n))


def _norm_mod_kernel(nx, tiles_per_batch, ctx_row, x_ref, c_ref, g_ref, sh_ref, sc_ref, o_ref):
    i = pl.program_id(0)
    is_x = i < nx
    v = jnp.where(is_x, x_ref[...], c_ref[...])
    r = jnp.where(is_x, i // tiles_per_batch, ctx_row)
    shift = sh_ref[pl.ds(r, 1), :]
    scale = sc_ref[pl.ds(r, 1), :]
    o_ref[...] = (_rms(v, g_ref[...]) * (1.0 + scale) + shift).astype(o_ref.dtype)


def _norm_modulate(x2, c2, gain, shift_tab, scale_tab, seq, ctx_row):
    m, d = x2.shape
    mc = c2.shape[0]
    tm = _tile(_gcd(seq, mc), 256, 16)
    nx, ncx = m // tm, mc // tm
    return pl.pallas_call(
        functools.partial(_norm_mod_kernel, nx, seq // tm, ctx_row),
        out_shape=jax.ShapeDtypeStruct((m + mc, d), BF16),
        grid=(nx + ncx,),
        in_specs=[pl.BlockSpec((tm, d), lambda i: (jnp.minimum(i, nx - 1), 0)),
                  pl.BlockSpec((tm, d), lambda i: (jnp.maximum(i - nx, 0), 0)),
                  pl.BlockSpec((1, d), lambda i: (0, 0)),
                  pl.BlockSpec((MOD_ROWS, d), lambda i: (0, 0)),
                  pl.BlockSpec((MOD_ROWS, d), lambda i: (0, 0))],
        out_specs=pl.BlockSpec((tm, d), lambda i: (i, 0)),
        compiler_params=_cp("arbitrary"),
        name="norm_modulate",
    )(x2, c2, gain.reshape(1, d), shift_tab, scale_tab)


def _gcd(a, b):
    while b:
        a, b = b, a % b
    return a


def _mm_kernel(a_ref, w_ref, o_ref):
    o_ref[...] = _dot(a_ref[...], w_ref[...].astype(BF16)).astype(o_ref.dtype)


def _project(a, w, col0, ncols, m_rows, out_dtype):
    k = a.shape[1]
    tn = _tile(_gcd(ncols, col0) if col0 else ncols, 1024, 128)
    tm = _tile(m_rows, 512, 16)
    jb = col0 // tn
    return pl.pallas_call(
        _mm_kernel,
        out_shape=jax.ShapeDtypeStruct((m_rows, ncols), out_dtype),
        grid=(ncols // tn, m_rows // tm),
        in_specs=[pl.BlockSpec((tm, k), lambda j, i: (i, 0)),
                  pl.BlockSpec((k, tn), lambda j, i: (0, jb + j))],
        out_specs=pl.BlockSpec((tm, tn), lambda j, i: (i, j)),
        compiler_params=_cp("arbitrary", "arbitrary"),
        name="in_projection",
    )(a, w)


def _mm2_kernel(ka, a1_ref, a2_ref, w_ref, o_ref):
    w = w_ref[...].astype(BF16)
    o_ref[...] = _dot(a1_ref[...], w[:ka]) + _dot(a2_ref[...], w[ka:])


def _out_projection(a1, a2, w):
    m, ka = a1.shape
    kb = a2.shape[1]
    n = w.shape[1]
    tn = _tile(n, 1024, 128)
    tm = _tile(m, 512, 16)
    return pl.pallas_call(
        functools.partial(_mm2_kernel, ka),
        out_shape=jax.ShapeDtypeStruct((m, n), F32),
        grid=(n // tn, m // tm),
        in_specs=[pl.BlockSpec((tm, ka), lambda j, i: (i, 0)),
                  pl.BlockSpec((tm, kb), lambda j, i: (i, 0)),
                  pl.BlockSpec((ka + kb, tn), lambda j, i: (0, j))],
        out_specs=pl.BlockSpec((tm, tn), lambda j, i: (i, j)),
        compiler_params=_cp("arbitrary", "arbitrary"),
        name="out_projection",
    )(a1, a2, w)


def _rope(v, cos, sin_signed):
    lane = lax.broadcasted_iota(I32, v.shape, 1)
    partner = jnp.where((lane & 32) == 0, pltpu.roll(v, HEAD_DIM - 32, 1), pltpu.roll(v, 32, 1))
    return v * cos + partner * sin_signed


def _attn_kernel(tq, seq, ctx, q_ref, kl_ref, kc_ref, vl_ref, vc_ref, qg_ref, kg_ref, cos_ref, sin_ref,
                 o_ref, k_sc, v_sc):
    qi = pl.program_id(2)

    @pl.when(qi == 0)
    def _():
        kg = kg_ref[...]
        k_sc[0:ctx, :] = _rms(kc_ref[...].astype(F32), kg).astype(BF16)
        kl = _rope(_rms(kl_ref[...].astype(F32), kg), cos_ref[...], sin_ref[...])
        k_sc[ctx:ctx + seq, :] = kl.astype(BF16)
        v_sc[0:ctx, 0:HEAD_DIM] = vc_ref[...]
        v_sc[ctx:ctx + seq, 0:HEAD_DIM] = vl_ref[...]
        v_sc[:, HEAD_DIM:2 * HEAD_DIM] = jnp.ones((ctx + seq, HEAD_DIM), BF16)

    r0 = pl.multiple_of(qi * tq, tq)
    cos = cos_ref[pl.ds(r0, tq), :]
    sin = sin_ref[pl.ds(r0, tq), :]
    qg = qg_ref[...] * (HEAD_DIM ** -0.5 * LOG2_E)
    keys = k_sc[...]
    vals = v_sc[...]
    pr = min(tq, ATTN_PROBLEM_ROWS)
    parts = [(slice(r * pr, (r + 1) * pr), slice(h * HEAD_DIM, (h + 1) * HEAD_DIM))
             for r in range(tq // pr) for h in range(GQA_GROUP)]
    def score(part):
        rs, hs = part
        return _dot_nt(_rope(_rms(q_ref[rs, hs].astype(F32), qg), cos[rs], sin[rs]).astype(BF16), keys)

    def finish(part, s):
        rs, hs = part
        p = jnp.exp2(s - jnp.max(s, axis=-1, keepdims=True))
        ov = _dot(p.astype(BF16), vals)
        o_ref[rs, hs] = (ov[:, 0:HEAD_DIM] / ov[:, HEAD_DIM:HEAD_DIM + 1]).astype(o_ref.dtype)

    s_prev = score(parts[0])
    for i in range(1, len(parts)):
        s_next = score(parts[i])
        finish(parts[i - 1], s_prev)
        s_prev = s_next
    finish(parts[-1], s_prev)


def _attention(p_head, p_kv, q_gain, k_gain, cos, sin_signed, batch, seq, ctx, kv_heads):
    tq = _tile(seq, 1024, 16)
    nq = seq // tq
    gw = GQA_GROUP * HEAD_DIM
    ctx_blk0 = batch * seq // ctx
    return pl.pallas_call(
        functools.partial(_attn_kernel, tq, seq, ctx),
        out_shape=jax.ShapeDtypeStruct((batch * seq, kv_heads * gw), BF16),
        grid=(batch, kv_heads, nq),
        in_specs=[pl.BlockSpec((tq, gw), lambda b, h, q: (b * nq + q, h)),
                  pl.BlockSpec((seq, HEAD_DIM), lambda b, h, q: (b, h)),
                  pl.BlockSpec((ctx, HEAD_DIM), lambda b, h, q: (ctx_blk0 + b, h)),
                  pl.BlockSpec((seq, HEAD_DIM), lambda b, h, q: (b, kv_heads + h)),
                  pl.BlockSpec((ctx, HEAD_DIM), lambda b, h, q: (ctx_blk0 + b, kv_heads + h)),
                  pl.BlockSpec((1, HEAD_DIM), lambda b, h, q: (0, 0)),
                  pl.BlockSpec((1, HEAD_DIM), lambda b, h, q: (0, 0)),
                  pl.BlockSpec((seq, HEAD_DIM), lambda b, h, q: (0, 0)),
                  pl.BlockSpec((seq, HEAD_DIM), lambda b, h, q: (0, 0))],
        out_specs=pl.BlockSpec((tq, gw), lambda b, h, q: (b * nq + q, h)),
        scratch_shapes=[pltpu.VMEM((ctx + seq, HEAD_DIM), BF16),
                        pltpu.VMEM((ctx + seq, 2 * HEAD_DIM), BF16)],
        compiler_params=_cp("arbitrary", "arbitrary", "arbitrary"),
        name="gqa_attention",
    )(p_head, p_kv, p_kv, p_kv, p_kv, q_gain.reshape(1, HEAD_DIM), k_gain.reshape(1, HEAD_DIM), cos, sin_signed)


def _hgrn_kernel(hpb, seq, ctx, rows, lb_ref, gain_ref, q_ref, g_ref, ff_ref, fb_ref, v_ref,
                 cff_ref, cfb_ref, cv_ref, o_ref, st_ref, of_ref, ob_ref):
    cpb = rows // CHUNK
    lg_chunk = CHUNK.bit_length() - 1
    row = lax.broadcasted_iota(I32, (rows, rows), 0)
    col = lax.broadcasted_iota(I32, (rows, rows), 1)
    same = (row >> lg_chunk) == (col >> lg_chunk)
    masks = (same & (col <= row), same & (col >= row))
    tris = tuple(jnp.where(m, 1.0, 0.0).astype(BF16) for m in masks)
    row_chunk = lax.broadcasted_iota(I32, (rows, HGRN_EXPAND), 0) >> lg_chunk

    st_ref[...] = jnp.zeros(st_ref.shape, F32)

    def lane_stack(a):
        return jnp.concatenate([jnp.where(row_chunk == c, a, 0.0).astype(BF16) for c in range(cpb)], axis=1)

    def chunk_rows(b, off):
        return jnp.concatenate([jnp.broadcast_to(b[c * CHUNK + off:c * CHUNK + off + 1, :], (CHUNK, HGRN_EXPAND))
                                for c in range(cpb)], axis=0)

    def blocks(items, with_out):
        n = len(items)
        last = [0 if it[5] else CHUNK - 1 for it in items]
        mid = [CHUNK // 2 if it[5] else CHUNK // 2 - 1 for it in items]
        ks, splits = [], []
        for _, raw, _, _, lb, _ in items:
            f = lb + (1.0 - lb) * jax.nn.sigmoid(raw)
            logf = jnp.log(f)
            hi = logf.astype(BF16)
            ks.append(1.0 - f)
            splits.append((hi, (logf - hi.astype(F32)).astype(BF16)))
        bs = [_dot(tris[items[j][5]], splits[j][0]) + _dot(tris[items[j][5]], splits[j][1]) for j in range(n)]
        uts = [_dot_tn(items[j][2], lane_stack(ks[j] * jnp.exp(chunk_rows(bs[j], last[j]) - bs[j]))) for j in range(n)]
        slabs = []
        for j in range(n):
            st = st_ref[items[j][0]]
            starts = [None] * cpb
            for c in (reversed(range(cpb)) if items[j][5] else range(cpb)):
                starts[c] = st.astype(BF16)
                st = (st * jnp.exp(bs[j][c * CHUNK + last[j]:c * CHUNK + last[j] + 1, :])
                      + uts[j][:, c * HGRN_EXPAND:(c + 1) * HGRN_EXPAND])
            st_ref[items[j][0]] = st
            slabs.append(jnp.concatenate(starts, axis=1))
        if not with_out:
            return None
        scores = []
        for j in range(n):
            b_mid = chunk_rows(bs[j], mid[j])
            qt = (items[j][3] * jnp.exp(bs[j] - b_mid)).astype(BF16)
            kt = (ks[j] * jnp.exp(b_mid - bs[j])).astype(BF16)
            scores.append(_dot_nt(qt, kt))
        outs = []
        for j in range(n):
            a = jnp.where(masks[items[j][5]], scores[j], 0.0).astype(BF16)
            inter = _dot_nt(lane_stack(items[j][3] * jnp.exp(bs[j])), slabs[j])
            outs.append(_dot(a, items[j][2]) + inter)
        return outs

    def ctx_body(i, carry):
        r_f = pl.multiple_of(i * rows, rows)
        r_b = pl.multiple_of((ctx // rows - 1 - i) * rows, rows)
        items = []
        for g in range(hpb):
            sl = slice(g * HGRN_EXPAND, (g + 1) * HGRN_EXPAND)
            items.append((2 * g, cff_ref[pl.ds(r_f, rows), sl], cv_ref[pl.ds(r_f, rows), sl], None, lb_ref[0:1, sl], 0))
            items.append((2 * g + 1, cfb_ref[pl.ds(r_b, rows), sl], cv_ref[pl.ds(r_b, rows), sl], None,
                          lb_ref[1:2, sl], 1))
        blocks(items, False)
        return carry

    lax.fori_loop(0, ctx // rows, ctx_body, 0)

    def seq_body(i, carry):
        r_f = pl.multiple_of(i * rows, rows)
        r_b = pl.multiple_of((seq // rows - 1 - i) * rows, rows)
        items, dests = [], []
        for g in range(hpb):
            sl = slice(g * HGRN_EXPAND, (g + 1) * HGRN_EXPAND)
            for rev, r, f_ref, out_ref in ((0, r_f, ff_ref, of_ref), (1, r_b, fb_ref, ob_ref)):
                q = _silu(q_ref[pl.ds(r, rows), sl].astype(F32))
                items.append((2 * g + rev, f_ref[pl.ds(r, rows), sl], v_ref[pl.ds(r, rows), sl], q,
                              lb_ref[rev:rev + 1, sl], rev))
                dests.append((out_ref, r, sl))
        for (out_ref, r, sl), o in zip(dests, blocks(items, True)):
            out_ref[pl.ds(r, rows), sl] = o
        return carry

    lax.fori_loop(0, seq // rows, seq_body, 0)

    rb = _tile(seq, 256, 8)

    def read_body(i, carry):
        r = pl.multiple_of(i * rb, rb)
        for g in range(hpb):
            sl = slice(g * HGRN_EXPAND, (g + 1) * HGRN_EXPAND)
            o = of_ref[pl.ds(r, rb), sl] + ob_ref[pl.ds(r, rb), sl]
            y = _rms(o, gain_ref[...]) * _silu(g_ref[pl.ds(r, rb), sl].astype(F32))
            o_ref[pl.ds(r, rb), sl] = y.astype(o_ref.dtype)
        return carry

    lax.fori_loop(0, seq // rb, read_body, 0)


def _hgrn(p_head, p_f, p_v, lower_bounds, hg_gain, batch, seq, ctx, attn_width, hgrn_width):
    heads = hgrn_width // HGRN_EXPAND
    hpb = 4 if heads % 4 == 0 else (2 if heads % 2 == 0 else 1)
    bw = hpb * HGRN_EXPAND
    nh = heads // hpb
    q0, g0 = attn_width // bw, (attn_width + hgrn_width) // bw
    ctx_blk0 = batch * seq // ctx
    rows = _tile(_gcd(seq, ctx), 256, CHUNK)
    lat = lambda c0: pl.BlockSpec((seq, bw), lambda b, h: (b, c0 + h))
    cx = lambda c0: pl.BlockSpec((ctx, bw), lambda b, h: (ctx_blk0 + b, c0 + h))
    return pl.pallas_call(
        functools.partial(_hgrn_kernel, hpb, seq, ctx, rows),
        out_shape=jax.ShapeDtypeStruct((batch * seq, hgrn_width), BF16),
        grid=(batch, nh),
        in_specs=[pl.BlockSpec((2, bw), lambda b, h: (0, h)),
                  pl.BlockSpec((1, HGRN_EXPAND), lambda b, h: (0, 0)),
                  lat(q0), lat(g0), lat(0), lat(nh), lat(0), cx(0), cx(nh), cx(0)],
        out_specs=pl.BlockSpec((seq, bw), lambda b, h: (b, h)),
        scratch_shapes=[pltpu.VMEM((2 * hpb, HGRN_EXPAND, HGRN_EXPAND), F32),
                        pltpu.VMEM((seq, bw), F32),
                        pltpu.VMEM((seq, bw), F32)],
        compiler_params=_cp("arbitrary", "arbitrary"),
        name="hgrn2_scan",
    )(lower_bounds, hg_gain.reshape(1, HGRN_EXPAND), p_head, p_head, p_f, p_f, p_v, p_f, p_f, p_v)


def _epilogue_kernel(tiles_per_batch, x_ref, y_ref, g1n_ref, g2n_ref, gate_ref, sh_ref, sc_ref, wr_ref, br_ref,
                     x1_ref, hp_ref, lg_ref):
    r = pl.program_id(0) // tiles_per_batch
    x1 = x_ref[...] + gate_ref[pl.ds(r, 1), :] * _rms(y_ref[...], g1n_ref[...])
    x1_ref[...] = x1
    h = _rms(x1, g2n_ref[...]) * (1.0 + sc_ref[pl.ds(r, 1), :]) + sh_ref[pl.ds(r, 1), :]
    hp_ref[...] = h
    w = wr_ref[...]
    w_hi = w.astype(BF16)
    w_lo = (w - w_hi.astype(F32)).astype(BF16)
    h_hi = h.astype(BF16)
    h_lo = (h - h_hi.astype(F32)).astype(BF16)
    lg_ref[...] = _dot_nt(w_hi, h_hi) + _dot_nt(w_hi, h_lo) + _dot_nt(w_lo, h_hi) + br_ref[...]


def _epilogue(x2, y, gain1, gain2, gate_tab, shift_tab, scale_tab, w_router_t, b_router, seq):
    m, d = x2.shape
    ne = w_router_t.shape[0]
    tm = _tile(seq, 256, 128)
    row = lambda: pl.BlockSpec((tm, d), lambda i: (i, 0))
    tab = lambda: pl.BlockSpec((MOD_ROWS, d), lambda i: (0, 0))
    vec = lambda: pl.BlockSpec((1, d), lambda i: (0, 0))
    return pl.pallas_call(
        functools.partial(_epilogue_kernel, seq // tm),
        out_shape=(jax.ShapeDtypeStruct((m, d), F32),
                   jax.ShapeDtypeStruct((m, d), F32),
                   jax.ShapeDtypeStruct((ne, m), F32)),
        grid=(m // tm,),
        in_specs=[row(), row(), vec(), vec(), tab(), tab(), tab(),
                  pl.BlockSpec((ne, d), lambda i: (0, 0)),
                  pl.BlockSpec((ne, 1), lambda i: (0, 0))],
        out_specs=(row(), row(), pl.BlockSpec((ne, tm), lambda i: (0, i))),
        compiler_params=_cp("arbitrary"),
        name="residual_norm_router",
    )(x2, y, gain1.reshape(1, d), gain2.reshape(1, d), gate_tab, shift_tab, scale_tab,
      w_router_t, b_router.reshape(ne, 1))


def _route_kernel(lg_ref, idx_ref, wt_ref, pos_ref, cnt_ref, carry_ref):
    i = pl.program_id(0)
    ne, tn = lg_ref.shape

    @pl.when(i == 0)
    def _():
        carry_ref[...] = jnp.zeros(carry_ref.shape, F32)

    lg = lg_ref[...]
    eidx = lax.broadcasted_iota(I32, (ne, tn), 0)
    vals, hots = [], []
    for k in range(TOP_K):
        m = jnp.max(lg, axis=0, keepdims=True)
        sel = jnp.min(jnp.where(lg == m, eidx, ne), axis=0, keepdims=True)
        hot = eidx == sel
        idx_ref[k:k + 1, :] = sel
        vals.append(m)
        hots.append(hot)
        lg = jnp.where(hot, -jnp.inf, lg)
    exps = [jnp.exp(v - vals[0]) for v in vals]
    denom = exps[0] + exps[1] + exps[2] + exps[3]
    for k in range(TOP_K):
        wt_ref[k:k + 1, :] = exps[k] / denom

    member = hots[0] | hots[1] | hots[2] | hots[3]
    t_row = lax.broadcasted_iota(I32, (tn, tn), 0)
    t_col = lax.broadcasted_iota(I32, (tn, tn), 1)
    before = (t_row < t_col).astype(BF16)
    rank = _dot(member.astype(BF16), before) + carry_ref[:, 0:1]
    for k in range(TOP_K):
        pos_ref[k:k + 1, :] = jnp.sum(jnp.where(hots[k], rank, 0.0), axis=0, keepdims=True).astype(I32)
    carry_ref[...] = carry_ref[...] + jnp.sum(member.astype(F32), axis=1, keepdims=True)
    cnt_ref[...] = carry_ref[...].astype(I32)


def _route(logits_t):
    ne, m = logits_t.shape
    tn = _tile(m, 512, 128)
    out4 = lambda: pl.BlockSpec((TOP_K, tn), lambda i: (0, i))
    return pl.pallas_call(
        _route_kernel,
        out_shape=(jax.ShapeDtypeStruct((TOP_K, m), I32),
                   jax.ShapeDtypeStruct((TOP_K, m), F32),
                   jax.ShapeDtypeStruct((TOP_K, m), I32),
                   jax.ShapeDtypeStruct((ne, 128), I32)),
        grid=(m // tn,),
        in_specs=[pl.BlockSpec((ne, tn), lambda i: (0, i))],
        out_specs=(out4(), out4(), out4(), pl.BlockSpec((ne, 128), lambda i: (0, 0))),
        scratch_shapes=[pltpu.VMEM((ne, 128), F32)],
        compiler_params=_cp("arbitrary"),
        name="top4_route",
    )(logits_t)


def _slots_kernel(row_tile, lg_tile, idx_ref, pos_ref, cnt_ref, slot_ref, te_ref, start_ref, ends_ref):
    ne = cnt_ref.shape[0]
    tn = idx_ref.shape[1]
    tiles = (cnt_ref[...] + (row_tile - 1)) >> lg_tile
    e_row = lax.broadcasted_iota(I32, (ne, ne), 0)
    e_col = lax.broadcasted_iota(I32, (ne, ne), 1)
    upto = jnp.where(e_col <= e_row, 1.0, 0.0).astype(BF16)
    ends = _dot(upto, tiles.astype(F32).astype(BF16)).astype(I32)
    start = (ends - tiles) << lg_tile
    start_ref[...] = start
    ends_ref[...] = ends
    eidx = lax.broadcasted_iota(I32, (ne, tn), 0)
    start_col = start[:, 0:1]
    for k in range(TOP_K):
        hit = eidx == idx_ref[k:k + 1, :]
        slot_ref[k:k + 1, :] = jnp.sum(jnp.where(hit, start_col, 0), axis=0, keepdims=True) + pos_ref[k:k + 1, :]
    tile_i = lax.broadcasted_iota(I32, (ne, te_ref.shape[1]), 1)
    te = jnp.sum(jnp.where(tile_i >= ends[:, 0:1], 1, 0), axis=0, keepdims=True)
    te_ref[...] = jnp.minimum(te, ne - 1)


def _slots(idx, pos, cnt, row_tile, n_tiles):
    ne = cnt.shape[0]
    m = idx.shape[1]
    tn = _tile(m, 2048, 128)
    ntp = -(-n_tiles // 128) * 128
    lg_tile = row_tile.bit_length() - 1
    assert 1 << lg_tile == row_tile and -(-m // row_tile) <= 256
    io4 = lambda: pl.BlockSpec((TOP_K, tn), lambda i: (0, i))
    small = lambda w: pl.BlockSpec((ne, w), lambda i: (0, 0))
    return pl.pallas_call(
        functools.partial(_slots_kernel, row_tile, lg_tile),
        out_shape=(jax.ShapeDtypeStruct((TOP_K, m), I32),
                   jax.ShapeDtypeStruct((1, ntp), I32),
                   jax.ShapeDtypeStruct((ne, 128), I32),
                   jax.ShapeDtypeStruct((ne, 128), I32)),
        grid=(m // tn,),
        in_specs=[io4(), io4(), small(128)],
        out_specs=(io4(), pl.BlockSpec((1, ntp), lambda i: (0, 0)), small(128), small(128)),
        compiler_params=_cp("arbitrary"),
        name="route_slots",
    )(idx, pos, cnt)


DMA_ISSUE_UNROLL = 8


def _inverse_kernel(tm, row_tile, ne, n_slots, start_ref, cnt_ref, nu_ref, slot_ref, tok_ref):
    i = pl.program_id(0)

    @pl.when(i == 0)
    def _():
        def zero(j, c):
            tok_ref[j] = 0
            return c

        for e in range(ne):
            first = start_ref[e] + cnt_ref[e]
            lax.fori_loop(first, first + ((-cnt_ref[e]) & (row_tile - 1)), zero, 0)
        lax.fori_loop(nu_ref[0] * row_tile, n_slots, zero, 0)

    def body(g, c):
        for u in range(DMA_ISSUE_UNROLL):
            r = g * DMA_ISSUE_UNROLL + u
            for k in range(TOP_K):
                tok_ref[slot_ref[0, 0, k * tm + r]] = i * tm + r
        return c

    lax.fori_loop(0, tm // DMA_ISSUE_UNROLL, body, 0)


def _inverse_slots(starts, counts, n_used, slot_tiles, n_slots, tm, row_tile):
    nt = slot_tiles.shape[0]
    return pl.pallas_call(
        functools.partial(_inverse_kernel, tm, row_tile, starts.shape[0], n_slots),
        out_shape=jax.ShapeDtypeStruct((n_slots,), I32),
        grid_spec=pltpu.PrefetchScalarGridSpec(
            num_scalar_prefetch=3,
            grid=(nt,),
            in_specs=[pl.BlockSpec((1, 1, TOP_K * tm), lambda i, st, ct, nu: (i, 0, 0), memory_space=pltpu.SMEM)],
            out_specs=pl.BlockSpec(memory_space=pltpu.SMEM)),
        compiler_params=_cp("arbitrary"),
        name="route_inverse",
    )(starts, counts, n_used, slot_tiles)


def _dispatch_kernel(tm, nu_ref, te_ref, start_ref, cnt_ref, tok_ref, next_ref, src_hbm, o_ref, buf, sem):
    i = pl.program_id(0)
    n_used = nu_ref[0]
    n_tiles = pl.num_programs(0)

    def row_groups(t):
        e = te_ref[jnp.minimum(t, n_tiles - 1)]
        rows = jnp.clip(start_ref[e] + cnt_ref[e] - t * tm, 0, tm)
        return (rows + (DMA_ISSUE_UNROLL - 1)) // DMA_ISSUE_UNROLL

    def issue(t_ref, t, b):
        def body(g, c):
            for u in range(DMA_ISSUE_UNROLL):
                r = g * DMA_ISSUE_UNROLL + u
                pltpu.make_async_copy(src_hbm.at[pl.ds(t_ref[0, 0, r], 1)], buf.at[b, pl.ds(r, 1)],
                                      sem.at[b]).start(priority=u % 2)
            return c

        lax.fori_loop(0, row_groups(t), body, 0)

    @pl.when(i == 0)
    def _():
        buf[...] = jnp.zeros(buf.shape, buf.dtype)
        issue(tok_ref, 0, 0)

    @pl.when(i + 1 < n_used)
    def _():
        issue(next_ref, i + 1, (i + 1) % 2)

    @pl.when(i < n_used)
    def _():
        b = i % 2

        def wait_group(g, c):
            pltpu.make_async_copy(src_hbm.at[pl.ds(0, DMA_ISSUE_UNROLL)], buf.at[b, pl.ds(0, DMA_ISSUE_UNROLL)],
                                  sem.at[b]).wait()
            return c

        lax.fori_loop(0, row_groups(i), wait_group, 0)
        o_ref[...] = buf[b].astype(o_ref.dtype)

    @pl.when(i >= n_used)
    def _():
        o_ref[...] = jnp.zeros(o_ref.shape, o_ref.dtype)


def _dispatch(n_used, tile_expert, starts, counts, tok_of_slot, src, tm):
    w = src.shape[1]
    nt = tok_of_slot.shape[0] // tm
    tok_tiles = tok_of_slot.reshape(nt, 1, tm)
    tok_spec = lambda f: pl.BlockSpec((1, 1, tm), f, memory_space=pltpu.SMEM)
    return pl.pallas_call(
        functools.partial(_dispatch_kernel, tm),
        out_shape=jax.ShapeDtypeStruct((nt * tm, w), BF16),
        grid_spec=pltpu.PrefetchScalarGridSpec(
            num_scalar_prefetch=4,
            grid=(nt,),
            in_specs=[tok_spec(lambda i, *_: (i, 0, 0)),
                      tok_spec(lambda i, *_: (jnp.minimum(i + 1, nt - 1), 0, 0)),
                      pl.BlockSpec(memory_space=pl.ANY)],
            out_specs=pl.BlockSpec((tm, w), lambda i, *_: (i, 0)),
            scratch_shapes=[pltpu.VMEM((2, tm, w), src.dtype), pltpu.SemaphoreType.DMA((2,))]),
        compiler_params=_cp("arbitrary"),
        name="moe_dispatch",
    )(n_used, tile_expert, starts, counts, tok_tiles, tok_tiles, src)


def _stream_expert_weights(n_pass, n_tiles, width, te_ref, nu_ref, grp_ref, sem, weights):
    c, i = pl.program_id(0), pl.program_id(1)
    n_used = nu_ref[0]

    def copies(e, cc, slot):
        col = pl.multiple_of(cc * width, width)
        return [pltpu.make_async_copy(hbm.at[e, :, pl.ds(col, width)], buf.at[slot], sem.at[j, slot])
                for j, (hbm, buf) in enumerate(weights)]

    @pl.when((c == 0) & (i == 0))
    def _():
        grp_ref[0] = 0
        for cp in copies(te_ref[0], 0, 0):
            cp.start()

    e = te_ref[i]
    first = (i < n_used) & ((i == 0) | (e != te_ref[jnp.maximum(i - 1, 0)]))

    @pl.when(first)
    def _():
        grp = grp_ref[0]
        slot = grp % 2
        for cp in copies(e, c, slot):
            cp.wait()
        nxt = lax.while_loop(lambda j: (j < n_used) & (te_ref[jnp.minimum(j, n_tiles - 1)] == e),
                             lambda j: j + 1, i + 1)
        more = nxt < n_used

        @pl.when(more)
        def _():
            for cp in copies(te_ref[jnp.minimum(nxt, n_tiles - 1)], c, 1 - slot):
                cp.start(priority=1)

        @pl.when(jnp.logical_not(more) & (c + 1 < n_pass))
        def _():
            for cp in copies(te_ref[0], c + 1, 1 - slot):
                cp.start(priority=1)

        grp_ref[0] = grp + 1

    return (grp_ref[0] + 1) % 2


def _expert_up_kernel(n_pass, n_tiles, te_ref, nu_ref, x_ref, wg_hbm, wu_hbm, bg_ref, bu_ref, h_ref,
                      wg_buf, wu_buf, sem, grp_ref):
    slot = _stream_expert_weights(n_pass, n_tiles, h_ref.shape[1], te_ref, nu_ref, grp_ref, sem,
                                  [(wg_hbm, wg_buf), (wu_hbm, wu_buf)])

    @pl.when(pl.program_id(1) < nu_ref[0])
    def _():
        x = x_ref[...]

        def proj(w_buf, b_ref):
            return _dot(x, w_buf[slot].astype(BF16)) + b_ref[0]

        g = jnp.minimum(proj(wg_buf, bg_ref), SWIGLU_LIMIT)
        u = jnp.clip(proj(wu_buf, bu_ref), -SWIGLU_LIMIT, SWIGLU_LIMIT)
        h_ref[...] = (g * jax.nn.sigmoid(SWIGLU_ALPHA * g) * (u + 1.0)).astype(h_ref.dtype)

    @pl.when(pl.program_id(1) >= nu_ref[0])
    def _():
        h_ref[...] = jnp.zeros(h_ref.shape, h_ref.dtype)


def _used_tile(i, nu):
    return jnp.minimum(i, nu[0] - 1)


def _expert_up(tile_expert, n_used, xs, w_gate, w_up, b_gate, b_up, tm):
    s_pad = xs.shape[0]
    ne, d, f = w_gate.shape
    fc = _tile(f, 512, 128)
    nt = s_pad // tm
    wspec = lambda: pl.BlockSpec(memory_space=pl.ANY)
    bspec = lambda: pl.BlockSpec((1, 1, fc), lambda c, i, te, nu: (te[i], 0, c))
    return pl.pallas_call(
        functools.partial(_expert_up_kernel, f // fc, nt),
        out_shape=jax.ShapeDtypeStruct((s_pad, f), BF16),
        grid_spec=pltpu.PrefetchScalarGridSpec(
            num_scalar_prefetch=2,
            grid=(f // fc, nt),
            in_specs=[pl.BlockSpec((tm, d), lambda c, i, te, nu: (_used_tile(i, nu), 0)),
                      wspec(), wspec(), bspec(), bspec()],
            out_specs=pl.BlockSpec((tm, fc), lambda c, i, te, nu: (i, c)),
            scratch_shapes=[pltpu.VMEM((2, d, fc), F32), pltpu.VMEM((2, d, fc), F32),
                            pltpu.SemaphoreType.DMA((2, 2)), pltpu.SMEM((1,), I32)]),
        compiler_params=_cp("arbitrary", "arbitrary"),
        name="expert_gate_up",
    )(tile_expert, n_used, xs, w_gate, w_up, b_gate.reshape(ne, 1, f), b_up.reshape(ne, 1, f))


def _expert_down_kernel(n_pass, n_tiles, te_ref, nu_ref, h_ref, wd_hbm, bd_ref, y_ref, wd_buf, sem, grp_ref):
    slot = _stream_expert_weights(n_pass, n_tiles, y_ref.shape[1], te_ref, nu_ref, grp_ref, sem, [(wd_hbm, wd_buf)])

    @pl.when(pl.program_id(1) < nu_ref[0])
    def _():
        y_ref[...] = _dot(h_ref[...], wd_buf[slot].astype(BF16)) + bd_ref[0]

    @pl.when(pl.program_id(1) >= nu_ref[0])
    def _():
        y_ref[...] = jnp.zeros(y_ref.shape, y_ref.dtype)


def _expert_down(tile_expert, n_used, h, w_down, b_down, tm, tn):
    s_pad, f = h.shape
    ne, _, d = w_down.shape
    nt = s_pad // tm
    return pl.pallas_call(
        functools.partial(_expert_down_kernel, d // tn, nt),
        out_shape=jax.ShapeDtypeStruct((s_pad, d), F32),
        grid_spec=pltpu.PrefetchScalarGridSpec(
            num_scalar_prefetch=2,
            grid=(d // tn, nt),
            in_specs=[pl.BlockSpec((tm, f), lambda c, i, te, nu: (_used_tile(i, nu), 0)),
                      pl.BlockSpec(memory_space=pl.ANY),
                      pl.BlockSpec((1, 1, tn), lambda c, i, te, nu: (te[i], 0, c))],
            out_specs=pl.BlockSpec((tm, tn), lambda c, i, te, nu: (i, c)),
            scratch_shapes=[pltpu.VMEM((2, f, tn), F32), pltpu.SemaphoreType.DMA((1, 2)),
                            pltpu.SMEM((1,), I32)]),
        compiler_params=_cp("arbitrary", "arbitrary"),
        name="expert_down",
    )(tile_expert, n_used, h, w_down, b_down.reshape(ne, 1, d))


def _combine_kernel(tm, tiles_per_batch, nt, slot_ref, next_ref, yp_hbm, wt_ref, x1_ref, gn_ref, gate_ref,
                    o_ref, buf, sem):
    i = pl.program_id(0)

    def issue(s_ref, b):
        def body(g, c):
            for u in range(DMA_ISSUE_UNROLL):
                r = g * DMA_ISSUE_UNROLL + u
                for k in range(TOP_K):
                    pltpu.make_async_copy(yp_hbm.at[pl.ds(s_ref[0, 0, k * tm + r], 1)],
                                          buf.at[b, k, pl.ds(r, 1)], sem.at[b]).start(priority=k % 2)
            return c

        lax.fori_loop(0, tm // DMA_ISSUE_UNROLL, body, 0)

    @pl.when(i == 0)
    def _():
        issue(slot_ref, 0)

    @pl.when(i + 1 < nt)
    def _():
        issue(next_ref, (i + 1) % 2)

    b = i % 2
    for k in range(TOP_K):
        pltpu.make_async_copy(yp_hbm.at[pl.ds(0, tm)], buf.at[b, k], sem.at[b]).wait()

    wt = wt_ref[...]
    moe = wt[:, 0:1] * buf[b, 0]
    for k in range(1, TOP_K):
        moe = moe + wt[:, k:k + 1] * buf[b, k]
    r = i // tiles_per_batch
    o_ref[...] = x1_ref[...] + gate_ref[pl.ds(r, 1), :] * _rms(moe, gn_ref[...])


def _combine(slot_tiles, yp, wts, x1, gain, gate_tab, seq, tm):
    m, d = x1.shape
    nt = m // tm
    slot_spec = lambda f: pl.BlockSpec((1, 1, TOP_K * tm), f, memory_space=pltpu.SMEM)
    return pl.pallas_call(
        functools.partial(_combine_kernel, tm, seq // tm, nt),
        out_shape=jax.ShapeDtypeStruct((m, d), F32),
        grid=(nt,),
        in_specs=[slot_spec(lambda i: (i, 0, 0)),
                  slot_spec(lambda i: (jnp.minimum(i + 1, nt - 1), 0, 0)),
                  pl.BlockSpec(memory_space=pl.ANY),
                  pl.BlockSpec((tm, TOP_K), lambda i: (i, 0)),
                  pl.BlockSpec((tm, d), lambda i: (i, 0)),
                  pl.BlockSpec((1, d), lambda i: (0, 0)),
                  pl.BlockSpec((MOD_ROWS, d), lambda i: (0, 0))],
        out_specs=pl.BlockSpec((tm, d), lambda i: (i, 0)),
        scratch_shapes=[pltpu.VMEM((2, TOP_K, tm, d), F32), pltpu.SemaphoreType.DMA((2,))],
        compiler_params=_cp("arbitrary"),
        name="moe_combine",
    )(slot_tiles, slot_tiles, yp, wts, x1, gain.reshape(1, d), gate_tab)


def _rope_tables(seq):
    rows = seq // GRID_W
    axis_dim = HEAD_DIM // 2
    row_ids = jnp.repeat(jnp.arange(rows), GRID_W).astype(F32)
    col_ids = jnp.tile(jnp.arange(GRID_W), rows).astype(F32)
    inv_freq = ROPE_THETA ** (-jnp.arange(0, axis_dim, 2, dtype=F32) / axis_dim)
    ang_r = row_ids[:, None] * inv_freq[None, :]
    ang_c = col_ids[:, None] * inv_freq[None, :]
    cos = jnp.concatenate([jnp.cos(ang_r)] * 2 + [jnp.cos(ang_c)] * 2, axis=1)
    sin = jnp.concatenate([-jnp.sin(ang_r), jnp.sin(ang_r), -jnp.sin(ang_c), jnp.sin(ang_c)], axis=1)
    return cos, sin


def _layer(x, xc, c, c_ctx, lower_bounds, w_ada, b_ada, gains, w_in, q_gain, k_gain, hg_gain, w_out,
           w_router, b_router, w_gate, b_gate, w_up, b_up, w_down, b_down):
    batch, seq, d = x.shape
    ctx = xc.shape[1]
    attn_width = d // 2
    hgrn_width = d // 2
    kv_heads = attn_width // HEAD_DIM // GQA_GROUP
    kv_width = kv_heads * HEAD_DIM
    head_cols = attn_width + 2 * hgrn_width
    n_experts = w_router.shape[1]
    m, mc = batch * seq, batch * ctx
    assert batch < MOD_ROWS and seq % ctx == 0

    cond = jnp.concatenate([c, c_ctx[None, :], jnp.zeros((MOD_ROWS - batch - 1, d), F32)], axis=0)
    mod = _ada_modulation(cond, w_ada, b_ada)
    sh1, sc1, g1, sh2, sc2, g2 = [mod[:, i * d:(i + 1) * d] for i in range(N_MOD)]

    x2 = x.reshape(m, d)
    hh = _norm_modulate(x2, xc.reshape(mc, d), gains[0], sh1, sc1, seq, batch)

    p_head = _project(hh, w_in, 0, head_cols, m, BF16)
    p_kv = _project(hh, w_in, head_cols, 2 * kv_width, m + mc, BF16)
    p_f = _project(hh, w_in, head_cols + 2 * kv_width, 2 * hgrn_width, m + mc, F32)
    p_v = _project(hh, w_in, head_cols + 2 * kv_width + 2 * hgrn_width, hgrn_width, m + mc, BF16)

    cos, sin = _rope_tables(seq)
    attn = _attention(p_head, p_kv, q_gain, k_gain, cos, sin, batch, seq, ctx, kv_heads)
    hgrn = _hgrn(p_head, p_f, p_v, lower_bounds, hg_gain, batch, seq, ctx, attn_width, hgrn_width)
    y = _out_projection(attn, hgrn, w_out)

    x1, h2, logits_t = _epilogue(x2, y, gains[1], gains[2], g1, sh2, sc2, w_router.T, b_router, seq)
    idx, wts, pos, cnt = _route(logits_t)

    row_tile = 256
    n_tiles = -(-(TOP_K * m + n_experts * (row_tile - 1)) // row_tile)
    slots, tile_expert, starts, ends = _slots(idx, pos, cnt, row_tile, n_tiles)
    tile_expert = tile_expert[0, :n_tiles]
    n_used = ends[n_experts - 1, 0:1]
    starts, counts = starts[:, 0], cnt[:, 0]

    tok_tile = _tile(seq, 128, 8)
    nt = m // tok_tile
    slot_tiles = slots.reshape(TOP_K, nt, tok_tile).transpose(1, 0, 2).reshape(nt, 1, TOP_K * tok_tile)
    tok_of_slot = _inverse_slots(starts, counts, n_used, slot_tiles, n_tiles * row_tile, tok_tile, row_tile)
    xs = _dispatch(n_used, tile_expert, starts, counts, tok_of_slot, h2, row_tile)
    hmid = _expert_up(tile_expert, n_used, xs, w_gate, w_up, b_gate, b_up, row_tile)
    ye = _expert_down(tile_expert, n_used, hmid, w_down, b_down, row_tile, _tile(d, 2048, 256))
    out = _combine(slot_tiles, ye, wts.T, x1, gains[3], g2, seq, tok_tile)
    return out.reshape(batch, seq, d)


def kernel(x, c, ctx, c_ctx, w_ada, b_ada, norm_gains, w_in, q_norm_gain, k_norm_gain, hgrn_lb_logits,
           hgrn_norm_gain, w_out, w_router, b_router, w_gate, b_gate, w_up, b_up, w_down, b_down):
    assert w_ada.shape[0] == 1, "single-layer stack only"
    lower_bounds = jnp.cumsum(jax.nn.softmax(hgrn_lb_logits.astype(F32), axis=0), axis=0)
    return _layer(x, ctx, c, c_ctx, lower_bounds[0], w_ada[0], b_ada[0], norm_gains[0], w_in[0],
                  q_norm_gain[0], k_norm_gain[0], hgrn_norm_gain[0], w_out[0], w_router[0], b_router[0],
                  w_gate[0], b_gate[0], w_up[0], b_up[0], w_down[0], b_down[0])
```

```python
import functools

import jax
import jax.numpy as jnp
from jax import lax
from jax.experimental import pallas as pl
from jax.experimental.pallas import tpu as pltpu

F32 = jnp.float32
BF16 = jnp.bfloat16
U32 = jnp.uint32
I32 = jnp.int32

GRID_W = 64
N_MOD = 6
EPS = 1e-6
HEAD_DIM = 128
GQA_GROUP = 4
ROPE_THETA = 10000.0
LOG2_E = 1.4426950408889634
ATTN_PROBLEM_ROWS = 256
HGRN_EXPAND = 128
CHUNK = 64
TOP_K = 4
SWIGLU_LIMIT = 7.0
SWIGLU_ALPHA = 1.702
MOD_ROWS = 8

V7X_VMEM_LIMIT = 56 * 1024 * 1024


def _cp(*sem):
    return pltpu.CompilerParams(dimension_semantics=sem, vmem_limit_bytes=V7X_VMEM_LIMIT)


def _tile(n, pref, mult):
    if n <= pref:
        return n
    t = (pref // mult) * mult
    while t > mult and n % t:
        t -= mult
    assert n % t == 0, (n, pref, mult)
    return t


def _dot(a, b):
    return jnp.dot(a, b, preferred_element_type=F32)


def _dot_nt(a, b):
    return lax.dot_general(a, b, (((1,), (1,)), ((), ())), preferred_element_type=F32)


def _dot_tn(a, b):
    return lax.dot_general(a, b, (((0,), (0,)), ((), ())), preferred_element_type=F32)


def _silu(v):
    return v * jax.nn.sigmoid(v)


def _rms(v, gain):
    return v * lax.rsqrt(jnp.mean(v * v, axis=-1, keepdims=True) + EPS) * gain


def _ada_kernel(c_ref, w_ref, b_ref, o_ref):
    s = _silu(c_ref[...]).astype(BF16)
    o_ref[...] = _dot(s, w_ref[...].astype(BF16)) + b_ref[...]


def _ada_modulation(cond, w_ada, b_ada):
    rows, d = cond.shape
    n = w_ada.shape[1]
    tn = _tile(n, 512, 128)
    return pl.pallas_call(
        _ada_kernel,
        out_shape=jax.ShapeDtypeStruct((rows, n), F32),
        grid=(n // tn,),
        in_specs=[pl.BlockSpec((rows, d), lambda j: (0, 0)),
                  pl.BlockSpec((d, tn), lambda j: (0, j)),
                  pl.BlockSpec((1, tn), lambda j: (0, j))],
        out_specs=pl.BlockSpec((rows, tn), lambda j: (0, j)),
        compiler_params=_cp("arbitrary"),
        name="ada_modulation",
    )(cond, w_ada, b_ada.reshape(1, n))


def _norm_mod_kernel(nx, tiles_per_batch, ctx_row, x_ref, c_ref, g_ref, sh_ref, sc_ref, o_ref):
    i = pl.program_id(0)
    is_x = i < nx
    v = jnp.where(is_x, x_ref[...], c_ref[...])
    r = jnp.where(is_x, i // tiles_per_batch, ctx_row)
    shift = sh_ref[pl.ds(r, 1), :]
    scale = sc_ref[pl.ds(r, 1), :]
    o_ref[...] = (_rms(v, g_ref[...]) * (1.0 + scale) + shift).astype(o_ref.dtype)


def _norm_modulate(x2, c2, gain, shift_tab, scale_tab, seq, ctx_row):
    m, d = x2.shape
    mc = c2.shape[0]
    tm = _tile(_gcd(seq, mc), 256, 16)
    nx, ncx = m // tm, mc // tm
    return pl.pallas_call(
        functools.partial(_norm_mod_kernel, nx, seq // tm, ctx_row),
        out_shape=jax.ShapeDtypeStruct((m + mc, d), BF16),
        grid=(nx + ncx,),
        in_specs=[pl.BlockSpec((tm, d), lambda i: (jnp.minimum(i, nx - 1), 0)),
                  pl.BlockSpec((tm, d), lambda i: (jnp.maximum(i - nx, 0), 0)),
                  pl.BlockSpec((1, d), lambda i: (0, 0)),
                  pl.BlockSpec((MOD_ROWS, d), lambda i: (0, 0)),
                  pl.BlockSpec((MOD_ROWS, d), lambda i: (0, 0))],
        out_specs=pl.BlockSpec((tm, d), lambda i: (i, 0)),
        compiler_params=_cp("arbitrary"),
        name="norm_modulate",
    )(x2, c2, gain.reshape(1, d), shift_tab, scale_tab)


def _gcd(a, b):
    while b:
        a, b = b, a % b
    return a


def _mm_kernel(a_ref, w_ref, o_ref):
    o_ref[...] = _dot(a_ref[...], w_ref[...].astype(BF16)).astype(o_ref.dtype)


def _project(a, w, col0, ncols, m_rows, out_dtype):
    k = a.shape[1]
    tn = _tile(_gcd(ncols, col0) if col0 else ncols, 1024, 128)
    tm = _tile(m_rows, 512, 16)
    jb = col0 // tn
    return pl.pallas_call(
        _mm_kernel,
        out_shape=jax.ShapeDtypeStruct((m_rows, ncols), out_dtype),
        grid=(ncols // tn, m_rows // tm),
        in_specs=[pl.BlockSpec((tm, k), lambda j, i: (i, 0)),
                  pl.BlockSpec((k, tn), lambda j, i: (0, jb + j))],
        out_specs=pl.BlockSpec((tm, tn), lambda j, i: (i, j)),
        compiler_params=_cp("arbitrary", "arbitrary"),
        name="in_projection",
    )(a, w)


def _mm2_kernel(ka, a1_ref, a2_ref, w_ref, o_ref):
    w = w_ref[...].astype(BF16)
    o_ref[...] = (_dot(a1_ref[...], w[:ka]) + _dot(a2_ref[...], w[ka:])).astype(o_ref.dtype)


def _out_projection(a1, a2, w):
    m, ka = a1.shape
    kb = a2.shape[1]
    n = w.shape[1]
    tn = _tile(n, 1024, 128)
    tm = _tile(m, 512, 16)
    return pl.pallas_call(
        functools.partial(_mm2_kernel, ka),
        out_shape=jax.ShapeDtypeStruct((m, n), BF16),
        grid=(n // tn, m // tm),
        in_specs=[pl.BlockSpec((tm, ka), lambda j, i: (i, 0)),
                  pl.BlockSpec((tm, kb), lambda j, i: (i, 0)),
                  pl.BlockSpec((ka + kb, tn), lambda j, i: (0, j))],
        out_specs=pl.BlockSpec((tm, tn), lambda j, i: (i, j)),
        compiler_params=_cp("arbitrary", "arbitrary"),
        name="out_projection",
    )(a1, a2, w)


def _rope(v, cos, sin_signed):
    lane = lax.broadcasted_iota(I32, v.shape, 1)
    partner = jnp.where((lane & 32) == 0, pltpu.roll(v, HEAD_DIM - 32, 1), pltpu.roll(v, 32, 1))
    return v * cos + partner * sin_signed


def _attn_kernel(tq, seq, ctx, q_ref, kl_ref, kc_ref, vl_ref, vc_ref, qg_ref, kg_ref, cos_ref, sin_ref,
                 o_ref, k_sc, v_sc):
    qi = pl.program_id(2)

    @pl.when(qi == 0)
    def _():
        kg = kg_ref[...]
        k_sc[0:ctx, :] = _rms(kc_ref[...].astype(F32), kg).astype(BF16)
        kl = _rope(_rms(kl_ref[...].astype(F32), kg), cos_ref[...], sin_ref[...])
        k_sc[ctx:ctx + seq, :] = kl.astype(BF16)
        v_sc[0:ctx, 0:HEAD_DIM] = vc_ref[...]
        v_sc[ctx:ctx + seq, 0:HEAD_DIM] = vl_ref[...]
        v_sc[:, HEAD_DIM:2 * HEAD_DIM] = jnp.ones((ctx + seq, HEAD_DIM), BF16)

    r0 = pl.multiple_of(qi * tq, tq)
    cos = cos_ref[pl.ds(r0, tq), :]
    sin = sin_ref[pl.ds(r0, tq), :]
    qg = qg_ref[...] * (HEAD_DIM ** -0.5 * LOG2_E)
    keys = k_sc[...]
    vals = v_sc[...]
    pr = min(tq, ATTN_PROBLEM_ROWS)
    parts = [(slice(r * pr, (r + 1) * pr), slice(h * HEAD_DIM, (h + 1) * HEAD_DIM))
             for r in range(tq // pr) for h in range(GQA_GROUP)]
    def score(part):
        rs, hs = part
        return _dot_nt(_rope(_rms(q_ref[rs, hs].astype(F32), qg), cos[rs], sin[rs]).astype(BF16), keys)

    def finish(part, s):
        rs, hs = part
        p = jnp.exp2(s - jnp.max(s, axis=-1, keepdims=True))
        ov = _dot(p.astype(BF16), vals)
        o_ref[rs, hs] = (ov[:, 0:HEAD_DIM] / ov[:, HEAD_DIM:HEAD_DIM + 1]).astype(o_ref.dtype)

    s_prev = score(parts[0])
    for i in range(1, len(parts)):
        s_next = score(parts[i])
        finish(parts[i - 1], s_prev)
        s_prev = s_next
    finish(parts[-1], s_prev)


def _attention(p_head, p_kv, q_gain, k_gain, cos, sin_signed, batch, seq, ctx, kv_heads):
    tq = _tile(seq, 1024, 16)
    nq = seq // tq
    gw = GQA_GROUP * HEAD_DIM
    ctx_blk0 = batch * seq // ctx
    return pl.pallas_call(
        functools.partial(_attn_kernel, tq, seq, ctx),
        out_shape=jax.ShapeDtypeStruct((batch * seq, kv_heads * gw), BF16),
        grid=(batch, kv_heads, nq),
        in_specs=[pl.BlockSpec((tq, gw), lambda b, h, q: (b * nq + q, h)),
                  pl.BlockSpec((seq, HEAD_DIM), lambda b, h, q: (b, h)),
                  pl.BlockSpec((ctx, HEAD_DIM), lambda b, h, q: (ctx_blk0 + b, h)),
                  pl.BlockSpec((seq, HEAD_DIM), lambda b, h, q: (b, kv_heads + h)),
                  pl.BlockSpec((ctx, HEAD_DIM), lambda b, h, q: (ctx_blk0 + b, kv_heads + h)),
                  pl.BlockSpec((1, HEAD_DIM), lambda b, h, q: (0, 0)),
                  pl.BlockSpec((1, HEAD_DIM), lambda b, h, q: (0, 0)),
                  pl.BlockSpec((seq, HEAD_DIM), lambda b, h, q: (0, 0)),
                  pl.BlockSpec((seq, HEAD_DIM), lambda b, h, q: (0, 0))],
        out_specs=pl.BlockSpec((tq, gw), lambda b, h, q: (b * nq + q, h)),
        scratch_shapes=[pltpu.VMEM((ctx + seq, HEAD_DIM), BF16),
                        pltpu.VMEM((ctx + seq, 2 * HEAD_DIM), BF16)],
        compiler_params=_cp("arbitrary", "arbitrary", "arbitrary"),
        name="gqa_attention",
    )(p_head, p_kv, p_kv, p_kv, p_kv, q_gain.reshape(1, HEAD_DIM), k_gain.reshape(1, HEAD_DIM), cos, sin_signed)


def _hgrn_kernel(hpb, seq, ctx, rows, lb_ref, gain_ref, q_ref, g_ref, ff_ref, fb_ref, v_ref,
                 cff_ref, cfb_ref, cv_ref, o_ref, st_ref, of_ref, ob_ref):
    cpb = rows // CHUNK
    lg_chunk = CHUNK.bit_length() - 1
    row = lax.broadcasted_iota(I32, (rows, rows), 0)
    col = lax.broadcasted_iota(I32, (rows, rows), 1)
    same = (row >> lg_chunk) == (col >> lg_chunk)
    masks = (same & (col <= row), same & (col >= row))
    tris = tuple(jnp.where(m, 1.0, 0.0).astype(BF16) for m in masks)
    row_chunk = lax.broadcasted_iota(I32, (rows, HGRN_EXPAND), 0) >> lg_chunk

    st_ref[...] = jnp.zeros(st_ref.shape, F32)

    def lane_stack(a):
        return jnp.concatenate([jnp.where(row_chunk == c, a, 0.0).astype(BF16) for c in range(cpb)], axis=1)

    def chunk_rows(b, off):
        return jnp.concatenate([jnp.broadcast_to(b[c * CHUNK + off:c * CHUNK + off + 1, :], (CHUNK, HGRN_EXPAND))
                                for c in range(cpb)], axis=0)

    def blocks(items, with_out):
        n = len(items)
        last = [0 if it[5] else CHUNK - 1 for it in items]
        mid = [CHUNK // 2 if it[5] else CHUNK // 2 - 1 for it in items]
        ks, splits = [], []
        for _, raw, _, _, lb, _ in items:
            f = lb + (1.0 - lb) * jax.nn.sigmoid(raw)
            logf = jnp.log(f)
            hi = logf.astype(BF16)
            ks.append(1.0 - f)
            splits.append((hi, (logf - hi.astype(F32)).astype(BF16)))
        bs = [_dot(tris[items[j][5]], splits[j][0]) + _dot(tris[items[j][5]], splits[j][1]) for j in range(n)]
        uts = [_dot_tn(items[j][2], lane_stack(ks[j] * jnp.exp(chunk_rows(bs[j], last[j]) - bs[j]))) for j in range(n)]
        slabs = []
        for j in range(n):
            st = st_ref[items[j][0]]
            starts = [None] * cpb
            for c in (reversed(range(cpb)) if items[j][5] else range(cpb)):
                starts[c] = st.astype(BF16)
                st = (st * jnp.exp(bs[j][c * CHUNK + last[j]:c * CHUNK + last[j] + 1, :])
                      + uts[j][:, c * HGRN_EXPAND:(c + 1) * HGRN_EXPAND])
            st_ref[items[j][0]] = st
            slabs.append(jnp.concatenate(starts, axis=1))
        if not with_out:
            return None
        scores = []
        for j in range(n):
            b_mid = chunk_rows(bs[j], mid[j])
            qt = (items[j][3] * jnp.exp(bs[j] - b_mid)).astype(BF16)
            kt = (ks[j] * jnp.exp(b_mid - bs[j])).astype(BF16)
            scores.append(_dot_nt(qt, kt))
        outs = []
        for j in range(n):
            a = jnp.where(masks[items[j][5]], scores[j], 0.0).astype(BF16)
            inter = _dot_nt(lane_stack(items[j][3] * jnp.exp(bs[j])), slabs[j])
            outs.append(_dot(a, items[j][2]) + inter)
        return outs

    def ctx_body(i, carry):
        r_f = pl.multiple_of(i * rows, rows)
        r_b = pl.multiple_of((ctx // rows - 1 - i) * rows, rows)
        items = []
        for g in range(hpb):
            sl = slice(g * HGRN_EXPAND, (g + 1) * HGRN_EXPAND)
            items.append((2 * g, cff_ref[pl.ds(r_f, rows), sl], cv_ref[pl.ds(r_f, rows), sl], None, lb_ref[0:1, sl], 0))
            items.append((2 * g + 1, cfb_ref[pl.ds(r_b, rows), sl], cv_ref[pl.ds(r_b, rows), sl], None,
                          lb_ref[1:2, sl], 1))
        blocks(items, False)
        return carry

    lax.fori_loop(0, ctx // rows, ctx_body, 0)

    def seq_body(i, carry):
        r_f = pl.multiple_of(i * rows, rows)
        r_b = pl.multiple_of((seq // rows - 1 - i) * rows, rows)
        items, dests = [], []
        for g in range(hpb):
            sl = slice(g * HGRN_EXPAND, (g + 1) * HGRN_EXPAND)
            for rev, r, f_ref, out_ref in ((0, r_f, ff_ref, of_ref), (1, r_b, fb_ref, ob_ref)):
                q = _silu(q_ref[pl.ds(r, rows), sl].astype(F32))
                items.append((2 * g + rev, f_ref[pl.ds(r, rows), sl], v_ref[pl.ds(r, rows), sl], q,
                              lb_ref[rev:rev + 1, sl], rev))
                dests.append((out_ref, r, sl))
        for (out_ref, r, sl), o in zip(dests, blocks(items, True)):
            out_ref[pl.ds(r, rows), sl] = o
        return carry

    lax.fori_loop(0, seq // rows, seq_body, 0)

    rb = _tile(seq, 256, 8)

    def read_body(i, carry):
        r = pl.multiple_of(i * rb, rb)
        for g in range(hpb):
            sl = slice(g * HGRN_EXPAND, (g + 1) * HGRN_EXPAND)
            o = of_ref[pl.ds(r, rb), sl] + ob_ref[pl.ds(r, rb), sl]
            y = _rms(o, gain_ref[...]) * _silu(g_ref[pl.ds(r, rb), sl].astype(F32))
            o_ref[pl.ds(r, rb), sl] = y.astype(o_ref.dtype)
        return carry

    lax.fori_loop(0, seq // rb, read_body, 0)


def _hgrn(p_head, p_f, p_v, lower_bounds, hg_gain, batch, seq, ctx, attn_width, hgrn_width):
    heads = hgrn_width // HGRN_EXPAND
    hpb = 4 if heads % 4 == 0 else (2 if heads % 2 == 0 else 1)
    bw = hpb * HGRN_EXPAND
    nh = heads // hpb
    q0, g0 = attn_width // bw, (attn_width + hgrn_width) // bw
    ctx_blk0 = batch * seq // ctx
    rows = _tile(_gcd(seq, ctx), 256, CHUNK)
    lat = lambda c0: pl.BlockSpec((seq, bw), lambda b, h: (b, c0 + h))
    cx = lambda c0: pl.BlockSpec((ctx, bw), lambda b, h: (ctx_blk0 + b, c0 + h))
    return pl.pallas_call(
        functools.partial(_hgrn_kernel, hpb, seq, ctx, rows),
        out_shape=jax.ShapeDtypeStruct((batch * seq, hgrn_width), BF16),
        grid=(batch, nh),
        in_specs=[pl.BlockSpec((2, bw), lambda b, h: (0, h)),
                  pl.BlockSpec((1, HGRN_EXPAND), lambda b, h: (0, 0)),
                  lat(q0), lat(g0), lat(0), lat(nh), lat(0), cx(0), cx(nh), cx(0)],
        out_specs=pl.BlockSpec((seq, bw), lambda b, h: (b, h)),
        scratch_shapes=[pltpu.VMEM((2 * hpb, HGRN_EXPAND, HGRN_EXPAND), F32),
                        pltpu.VMEM((seq, bw), F32),
                        pltpu.VMEM((seq, bw), F32)],
        compiler_params=_cp("arbitrary", "arbitrary"),
        name="hgrn2_scan",
    )(lower_bounds, hg_gain.reshape(1, HGRN_EXPAND), p_head, p_head, p_f, p_f, p_v, p_f, p_f, p_v)


def _epilogue_kernel(tiles_per_batch, x_ref, y_ref, g1n_ref, g2n_ref, gate_ref, sh_ref, sc_ref, wr_ref, br_ref,
                     x1_ref, hp_ref, lg_ref):
    r = pl.program_id(0) // tiles_per_batch
    x1 = x_ref[...] + gate_ref[pl.ds(r, 1), :] * _rms(y_ref[...].astype(F32), g1n_ref[...])
    x1_ref[...] = x1
    h = _rms(x1, g2n_ref[...]) * (1.0 + sc_ref[pl.ds(r, 1), :]) + sh_ref[pl.ds(r, 1), :]
    hp_ref[...] = h
    w = wr_ref[...]
    w_hi = w.astype(BF16)
    w_lo = (w - w_hi.astype(F32)).astype(BF16)
    h_hi = h.astype(BF16)
    h_lo = (h - h_hi.astype(F32)).astype(BF16)
    lg_ref[...] = _dot_nt(w_hi, h_hi) + _dot_nt(w_hi, h_lo) + _dot_nt(w_lo, h_hi) + br_ref[...]


def _epilogue(x2, y, gain1, gain2, gate_tab, shift_tab, scale_tab, w_router_t, b_router, seq):
    m, d = x2.shape
    ne = w_router_t.shape[0]
    tm = _tile(seq, 256, 128)
    row = lambda: pl.BlockSpec((tm, d), lambda i: (i, 0))
    tab = lambda: pl.BlockSpec((MOD_ROWS, d), lambda i: (0, 0))
    vec = lambda: pl.BlockSpec((1, d), lambda i: (0, 0))
    return pl.pallas_call(
        functools.partial(_epilogue_kernel, seq // tm),
        out_shape=(jax.ShapeDtypeStruct((m, d), F32),
                   jax.ShapeDtypeStruct((m, d), F32),
                   jax.ShapeDtypeStruct((ne, m), F32)),
        grid=(m // tm,),
        in_specs=[row(), row(), vec(), vec(), tab(), tab(), tab(),
                  pl.BlockSpec((ne, d), lambda i: (0, 0)),
                  pl.BlockSpec((ne, 1), lambda i: (0, 0))],
        out_specs=(row(), row(), pl.BlockSpec((ne, tm), lambda i: (0, i))),
        compiler_params=_cp("arbitrary"),
        name="residual_norm_router",
    )(x2, y, gain1.reshape(1, d), gain2.reshape(1, d), gate_tab, shift_tab, scale_tab,
      w_router_t, b_router.reshape(ne, 1))


def _route_kernel(lg_ref, idx_ref, wt_ref, pos_ref, cnt_ref, carry_ref):
    i = pl.program_id(0)
    ne, tn = lg_ref.shape

    @pl.when(i == 0)
    def _():
        carry_ref[...] = jnp.zeros(carry_ref.shape, F32)

    lg = lg_ref[...]
    eidx = lax.broadcasted_iota(I32, (ne, tn), 0)
    vals, hots = [], []
    for k in range(TOP_K):
        m = jnp.max(lg, axis=0, keepdims=True)
        sel = jnp.min(jnp.where(lg == m, eidx, ne), axis=0, keepdims=True)
        hot = eidx == sel
        idx_ref[k:k + 1, :] = sel
        vals.append(m)
        hots.append(hot)
        lg = jnp.where(hot, -jnp.inf, lg)
    exps = [jnp.exp(v - vals[0]) for v in vals]
    denom = exps[0] + exps[1] + exps[2] + exps[3]
    for k in range(TOP_K):
        wt_ref[k:k + 1, :] = exps[k] / denom

    member = hots[0] | hots[1] | hots[2] | hots[3]
    t_row = lax.broadcasted_iota(I32, (tn, tn), 0)
    t_col = lax.broadcasted_iota(I32, (tn, tn), 1)
    before = (t_row < t_col).astype(BF16)
    rank = _dot(member.astype(BF16), before) + carry_ref[:, 0:1]
    for k in range(TOP_K):
        pos_ref[k:k + 1, :] = jnp.sum(jnp.where(hots[k], rank, 0.0), axis=0, keepdims=True).astype(I32)
    carry_ref[...] = carry_ref[...] + jnp.sum(member.astype(F32), axis=1, keepdims=True)
    cnt_ref[...] = carry_ref[...].astype(I32)


def _route(logits_t):
    ne, m = logits_t.shape
    tn = _tile(m, 512, 128)
    out4 = lambda: pl.BlockSpec((TOP_K, tn), lambda i: (0, i))
    return pl.pallas_call(
        _route_kernel,
        out_shape=(jax.ShapeDtypeStruct((TOP_K, m), I32),
                   jax.ShapeDtypeStruct((TOP_K, m), F32),
                   jax.ShapeDtypeStruct((TOP_K, m), I32),
                   jax.ShapeDtypeStruct((ne, 128), I32)),
        grid=(m // tn,),
        in_specs=[pl.BlockSpec((ne, tn), lambda i: (0, i))],
        out_specs=(out4(), out4(), out4(), pl.BlockSpec((ne, 128), lambda i: (0, 0))),
        scratch_shapes=[pltpu.VMEM((ne, 128), F32)],
        compiler_params=_cp("arbitrary"),
        name="top4_route",
    )(logits_t)


def _slots_kernel(row_tile, lg_tile, idx_ref, pos_ref, cnt_ref, slot_ref, te_ref, start_ref, ends_ref):
    ne = cnt_ref.shape[0]
    tn = idx_ref.shape[1]
    tiles = (cnt_ref[...] + (row_tile - 1)) >> lg_tile
    e_row = lax.broadcasted_iota(I32, (ne, ne), 0)
    e_col = lax.broadcasted_iota(I32, (ne, ne), 1)
    upto = jnp.where(e_col <= e_row, 1.0, 0.0).astype(BF16)
    ends = _dot(upto, tiles.astype(F32).astype(BF16)).astype(I32)
    start = (ends - tiles) << lg_tile
    start_ref[...] = start
    ends_ref[...] = ends
    eidx = lax.broadcasted_iota(I32, (ne, tn), 0)
    start_col = start[:, 0:1]
    for k in range(TOP_K):
        hit = eidx == idx_ref[k:k + 1, :]
        slot_ref[k:k + 1, :] = jnp.sum(jnp.where(hit, start_col, 0), axis=0, keepdims=True) + pos_ref[k:k + 1, :]
    tile_i = lax.broadcasted_iota(I32, (ne, te_ref.shape[1]), 1)
    te = jnp.sum(jnp.where(tile_i >= ends[:, 0:1], 1, 0), axis=0, keepdims=True)
    te_ref[...] = jnp.minimum(te, ne - 1)


def _slots(idx, pos, cnt, row_tile, n_tiles):
    ne = cnt.shape[0]
    m = idx.shape[1]
    tn = _tile(m, 2048, 128)
    ntp = -(-n_tiles // 128) * 128
    lg_tile = row_tile.bit_length() - 1
    assert 1 << lg_tile == row_tile and -(-m // row_tile) <= 256
    io4 = lambda: pl.BlockSpec((TOP_K, tn), lambda i: (0, i))
    small = lambda w: pl.BlockSpec((ne, w), lambda i: (0, 0))
    return pl.pallas_call(
        functools.partial(_slots_kernel, row_tile, lg_tile),
        out_shape=(jax.ShapeDtypeStruct((TOP_K, m), I32),
                   jax.ShapeDtypeStruct((1, ntp), I32),
                   jax.ShapeDtypeStruct((ne, 128), I32),
                   jax.ShapeDtypeStruct((ne, 128), I32)),
        grid=(m // tn,),
        in_specs=[io4(), io4(), small(128)],
        out_specs=(io4(), pl.BlockSpec((1, ntp), lambda i: (0, 0)), small(128), small(128)),
        compiler_params=_cp("arbitrary"),
        name="route_slots",
    )(idx, pos, cnt)


DMA_ISSUE_UNROLL = 8


def _inverse_kernel(tm, row_tile, ne, n_slots, start_ref, cnt_ref, nu_ref, slot_ref, tok_ref):
    i = pl.program_id(0)

    @pl.when(i == 0)
    def _():
        def zero(j, c):
            tok_ref[j] = 0
            return c

        for e in range(ne):
            first = start_ref[e] + cnt_ref[e]
            lax.fori_loop(first, first + ((-cnt_ref[e]) & (row_tile - 1)), zero, 0)
        lax.fori_loop(nu_ref[0] * row_tile, n_slots, zero, 0)

    def body(g, c):
        for u in range(DMA_ISSUE_UNROLL):
            r = g * DMA_ISSUE_UNROLL + u
            for k in range(TOP_K):
                tok_ref[slot_ref[0, 0, k * tm + r]] = i * tm + r
        return c

    lax.fori_loop(0, tm // DMA_ISSUE_UNROLL, body, 0)


def _inverse_slots(starts, counts, n_used, slot_tiles, n_slots, tm, row_tile):
    nt = slot_tiles.shape[0]
    return pl.pallas_call(
        functools.partial(_inverse_kernel, tm, row_tile, starts.shape[0], n_slots),
        out_shape=jax.ShapeDtypeStruct((n_slots,), I32),
        grid_spec=pltpu.PrefetchScalarGridSpec(
            num_scalar_prefetch=3,
            grid=(nt,),
            in_specs=[pl.BlockSpec((1, 1, TOP_K * tm), lambda i, st, ct, nu: (i, 0, 0), memory_space=pltpu.SMEM)],
            out_specs=pl.BlockSpec(memory_space=pltpu.SMEM)),
        compiler_params=_cp("arbitrary"),
        name="route_inverse",
    )(starts, counts, n_used, slot_tiles)


def _dispatch_kernel(tm, nu_ref, te_ref, start_ref, cnt_ref, tok_ref, next_ref, src_hbm, o_ref, buf, sem):
    i = pl.program_id(0)
    n_used = nu_ref[0]
    n_tiles = pl.num_programs(0)

    def row_groups(t):
        e = te_ref[jnp.minimum(t, n_tiles - 1)]
        rows = jnp.clip(start_ref[e] + cnt_ref[e] - t * tm, 0, tm)
        return (rows + (DMA_ISSUE_UNROLL - 1)) // DMA_ISSUE_UNROLL

    def issue(t_ref, t, b):
        def body(g, c):
            for u in range(DMA_ISSUE_UNROLL):
                r = g * DMA_ISSUE_UNROLL + u
                pltpu.make_async_copy(src_hbm.at[pl.ds(t_ref[0, 0, r], 1)], buf.at[b, pl.ds(r, 1)],
                                      sem.at[b]).start(priority=u % 2)
            return c

        lax.fori_loop(0, row_groups(t), body, 0)

    @pl.when(i == 0)
    def _():
        buf[...] = jnp.zeros(buf.shape, buf.dtype)
        issue(tok_ref, 0, 0)

    @pl.when(i + 1 < n_used)
    def _():
        issue(next_ref, i + 1, (i + 1) % 2)

    @pl.when(i < n_used)
    def _():
        b = i % 2

        def wait_group(g, c):
            pltpu.make_async_copy(src_hbm.at[pl.ds(0, DMA_ISSUE_UNROLL)], buf.at[b, pl.ds(0, DMA_ISSUE_UNROLL)],
                                  sem.at[b]).wait()
            return c

        lax.fori_loop(0, row_groups(i), wait_group, 0)
        o_ref[...] = buf[b].astype(o_ref.dtype)

    @pl.when(i >= n_used)
    def _():
        o_ref[...] = jnp.zeros(o_ref.shape, o_ref.dtype)


def _dispatch(n_used, tile_expert, starts, counts, tok_of_slot, src, tm):
    w = src.shape[1]
    nt = tok_of_slot.shape[0] // tm
    tok_tiles = tok_of_slot.reshape(nt, 1, tm)
    tok_spec = lambda f: pl.BlockSpec((1, 1, tm), f, memory_space=pltpu.SMEM)
    return pl.pallas_call(
        functools.partial(_dispatch_kernel, tm),
        out_shape=jax.ShapeDtypeStruct((nt * tm, w), BF16),
        grid_spec=pltpu.PrefetchScalarGridSpec(
            num_scalar_prefetch=4,
            grid=(nt,),
            in_specs=[tok_spec(lambda i, *_: (i, 0, 0)),
                      tok_spec(lambda i, *_: (jnp.minimum(i + 1, nt - 1), 0, 0)),
                      pl.BlockSpec(memory_space=pl.ANY)],
            out_specs=pl.BlockSpec((tm, w), lambda i, *_: (i, 0)),
            scratch_shapes=[pltpu.VMEM((2, tm, w), src.dtype), pltpu.SemaphoreType.DMA((2,))]),
        compiler_params=_cp("arbitrary"),
        name="moe_dispatch",
    )(n_used, tile_expert, starts, counts, tok_tiles, tok_tiles, src)


def _stream_expert_weights(n_pass, n_tiles, width, te_ref, nu_ref, grp_ref, sem, weights):
    c, i = pl.program_id(0), pl.program_id(1)
    n_used = nu_ref[0]

    def copies(e, cc, slot):
        col = pl.multiple_of(cc * width, width)
        return [pltpu.make_async_copy(hbm.at[e, :, pl.ds(col, width)], buf.at[slot], sem.at[j, slot])
                for j, (hbm, buf) in enumerate(weights)]

    @pl.when((c == 0) & (i == 0))
    def _():
        grp_ref[0] = 0
        for cp in copies(te_ref[0], 0, 0):
            cp.start()

    e = te_ref[i]
    first = (i < n_used) & ((i == 0) | (e != te_ref[jnp.maximum(i - 1, 0)]))

    @pl.when(first)
    def _():
        grp = grp_ref[0]
        slot = grp % 2
        for cp in copies(e, c, slot):
            cp.wait()
        nxt = lax.while_loop(lambda j: (j < n_used) & (te_ref[jnp.minimum(j, n_tiles - 1)] == e),
                             lambda j: j + 1, i + 1)
        more = nxt < n_used

        @pl.when(more)
        def _():
            for cp in copies(te_ref[jnp.minimum(nxt, n_tiles - 1)], c, 1 - slot):
                cp.start(priority=1)

        @pl.when(jnp.logical_not(more) & (c + 1 < n_pass))
        def _():
            for cp in copies(te_ref[0], c + 1, 1 - slot):
                cp.start(priority=1)

        grp_ref[0] = grp + 1

    return (grp_ref[0] + 1) % 2


def _expert_up_kernel(n_pass, n_tiles, te_ref, nu_ref, x_ref, wg_hbm, wu_hbm, bg_ref, bu_ref, h_ref,
                      wg_buf, wu_buf, sem, grp_ref):
    slot = _stream_expert_weights(n_pass, n_tiles, h_ref.shape[1], te_ref, nu_ref, grp_ref, sem,
                                  [(wg_hbm, wg_buf), (wu_hbm, wu_buf)])

    @pl.when(pl.program_id(1) < nu_ref[0])
    def _():
        x = x_ref[...]

        def proj(w_buf, b_ref):
            return _dot(x, w_buf[slot].astype(BF16)) + b_ref[0]

        g = jnp.minimum(proj(wg_buf, bg_ref), SWIGLU_LIMIT)
        u = jnp.clip(proj(wu_buf, bu_ref), -SWIGLU_LIMIT, SWIGLU_LIMIT)
        h_ref[...] = (g * jax.nn.sigmoid(SWIGLU_ALPHA * g) * (u + 1.0)).astype(h_ref.dtype)

    @pl.when(pl.program_id(1) >= nu_ref[0])
    def _():
        h_ref[...] = jnp.zeros(h_ref.shape, h_ref.dtype)


def _used_tile(i, nu):
    return jnp.minimum(i, nu[0] - 1)


def _expert_up(tile_expert, n_used, xs, w_gate, w_up, b_gate, b_up, tm):
    s_pad = xs.shape[0]
    ne, d, f = w_gate.shape
    fc = _tile(f, 512, 128)
    nt = s_pad // tm
    wspec = lambda: pl.BlockSpec(memory_space=pl.ANY)
    bspec = lambda: pl.BlockSpec((1, 1, fc), lambda c, i, te, nu: (te[i], 0, c))
    return pl.pallas_call(
        functools.partial(_expert_up_kernel, f // fc, nt),
        out_shape=jax.ShapeDtypeStruct((s_pad, f), BF16),
        grid_spec=pltpu.PrefetchScalarGridSpec(
            num_scalar_prefetch=2,
            grid=(f // fc, nt),
            in_specs=[pl.BlockSpec((tm, d), lambda c, i, te, nu: (_used_tile(i, nu), 0)),
                      wspec(), wspec(), bspec(), bspec()],
            out_specs=pl.BlockSpec((tm, fc), lambda c, i, te, nu: (i, c)),
            scratch_shapes=[pltpu.VMEM((2, d, fc), F32), pltpu.VMEM((2, d, fc), F32),
                            pltpu.SemaphoreType.DMA((2, 2)), pltpu.SMEM((1,), I32)]),
        compiler_params=_cp("arbitrary", "arbitrary"),
        name="expert_gate_up",
    )(tile_expert, n_used, xs, w_gate, w_up, b_gate.reshape(ne, 1, f), b_up.reshape(ne, 1, f))


def _expert_down_kernel(n_pass, n_tiles, te_ref, nu_ref, h_ref, wd_hbm, bd_ref, y_ref, wd_buf, sem, grp_ref):
    slot = _stream_expert_weights(n_pass, n_tiles, y_ref.shape[1], te_ref, nu_ref, grp_ref, sem, [(wd_hbm, wd_buf)])

    @pl.when(pl.program_id(1) < nu_ref[0])
    def _():
        y_ref[...] = _dot(h_ref[...], wd_buf[slot].astype(BF16)) + bd_ref[0]

    @pl.when(pl.program_id(1) >= nu_ref[0])
    def _():
        y_ref[...] = jnp.zeros(y_ref.shape, y_ref.dtype)


def _expert_down(tile_expert, n_used, h, w_down, b_down, tm, tn):
    s_pad, f = h.shape
    ne, _, d = w_down.shape
    nt = s_pad // tm
    return pl.pallas_call(
        functools.partial(_expert_down_kernel, d // tn, nt),
        out_shape=jax.ShapeDtypeStruct((s_pad, d), F32),
        grid_spec=pltpu.PrefetchScalarGridSpec(
            num_scalar_prefetch=2,
            grid=(d // tn, nt),
            in_specs=[pl.BlockSpec((tm, f), lambda c, i, te, nu: (_used_tile(i, nu), 0)),
                      pl.BlockSpec(memory_space=pl.ANY),
                      pl.BlockSpec((1, 1, tn), lambda c, i, te, nu: (te[i], 0, c))],
            out_specs=pl.BlockSpec((tm, tn), lambda c, i, te, nu: (i, c)),
            scratch_shapes=[pltpu.VMEM((2, f, tn), F32), pltpu.SemaphoreType.DMA((1, 2)),
                            pltpu.SMEM((1,), I32)]),
        compiler_params=_cp("arbitrary", "arbitrary"),
        name="expert_down",
    )(tile_expert, n_used, h, w_down, b_down.reshape(ne, 1, d))


def _combine_kernel(tm, tiles_per_batch, nt, slot_ref, next_ref, yp_hbm, wt_ref, x1_ref, gn_ref, gate_ref,
                    o_ref, buf, sem):
    i = pl.program_id(0)

    def issue(s_ref, b):
        def body(g, c):
            for u in range(DMA_ISSUE_UNROLL):
                r = g * DMA_ISSUE_UNROLL + u
                for k in range(TOP_K):
                    pltpu.make_async_copy(yp_hbm.at[pl.ds(s_ref[0, 0, k * tm + r], 1)],
                                          buf.at[b, k, pl.ds(r, 1)], sem.at[b]).start(priority=k % 2)
            return c

        lax.fori_loop(0, tm // DMA_ISSUE_UNROLL, body, 0)

    @pl.when(i == 0)
    def _():
        issue(slot_ref, 0)

    @pl.when(i + 1 < nt)
    def _():
        issue(next_ref, (i + 1) % 2)

    b = i % 2
    for k in range(TOP_K):
        pltpu.make_async_copy(yp_hbm.at[pl.ds(0, tm)], buf.at[b, k], sem.at[b]).wait()

    wt = wt_ref[...]
    moe = wt[:, 0:1] * buf[b, 0]
    for k in range(1, TOP_K):
        moe = moe + wt[:, k:k + 1] * buf[b, k]
    r = i // tiles_per_batch
    o_ref[...] = x1_ref[...] + gate_ref[pl.ds(r, 1), :] * _rms(moe, gn_ref[...])


def _combine(slot_tiles, yp, wts, x1, gain, gate_tab, seq, tm):
    m, d = x1.shape
    nt = m // tm
    slot_spec = lambda f: pl.BlockSpec((1, 1, TOP_K * tm), f, memory_space=pltpu.SMEM)
    return pl.pallas_call(
        functools.partial(_combine_kernel, tm, seq // tm, nt),
        out_shape=jax.ShapeDtypeStruct((m, d), F32),
        grid=(nt,),
        in_specs=[slot_spec(lambda i: (i, 0, 0)),
                  slot_spec(lambda i: (jnp.minimum(i + 1, nt - 1), 0, 0)),
                  pl.BlockSpec(memory_space=pl.ANY),
                  pl.BlockSpec((tm, TOP_K), lambda i: (i, 0)),
                  pl.BlockSpec((tm, d), lambda i: (i, 0)),
                  pl.BlockSpec((1, d), lambda i: (0, 0)),
                  pl.BlockSpec((MOD_ROWS, d), lambda i: (0, 0))],
        out_specs=pl.BlockSpec((tm, d), lambda i: (i, 0)),
        scratch_shapes=[pltpu.VMEM((2, TOP_K, tm, d), F32), pltpu.SemaphoreType.DMA((2,))],
        compiler_params=_cp("arbitrary"),
        name="moe_combine",
    )(slot_tiles, slot_tiles, yp, wts, x1, gain.reshape(1, d), gate_tab)


def _rope_tables(seq):
    rows = seq // GRID_W
    axis_dim = HEAD_DIM // 2
    row_ids = jnp.repeat(jnp.arange(rows), GRID_W).astype(F32)
    col_ids = jnp.tile(jnp.arange(GRID_W), rows).astype(F32)
    inv_freq = ROPE_THETA ** (-jnp.arange(0, axis_dim, 2, dtype=F32) / axis_dim)
    ang_r = row_ids[:, None] * inv_freq[None, :]
    ang_c = col_ids[:, None] * inv_freq[None, :]
    cos = jnp.concatenate([jnp.cos(ang_r)] * 2 + [jnp.cos(ang_c)] * 2, axis=1)
    sin = jnp.concatenate([-jnp.sin(ang_r), jnp.sin(ang_r), -jnp.sin(ang_c), jnp.sin(ang_c)], axis=1)
    return cos, sin


def _layer(x, xc, c, c_ctx, lower_bounds, w_ada, b_ada, gains, w_in, q_gain, k_gain, hg_gain, w_out,
           w_router, b_router, w_gate, b_gate, w_up, b_up, w_down, b_down):
    batch, seq, d = x.shape
    ctx = xc.shape[1]
    attn_width = d // 2
    hgrn_width = d // 2
    kv_heads = attn_width // HEAD_DIM // GQA_GROUP
    kv_width = kv_heads * HEAD_DIM
    head_cols = attn_width + 2 * hgrn_width
    n_experts = w_router.shape[1]
    m, mc = batch * seq, batch * ctx
    assert batch < MOD_ROWS and seq % ctx == 0

    cond = jnp.concatenate([c, c_ctx[None, :], jnp.zeros((MOD_ROWS - batch - 1, d), F32)], axis=0)
    mod = _ada_modulation(cond, w_ada, b_ada)
    sh1, sc1, g1, sh2, sc2, g2 = [mod[:, i * d:(i + 1) * d] for i in range(N_MOD)]

    x2 = x.reshape(m, d)
    hh = _norm_modulate(x2, xc.reshape(mc, d), gains[0], sh1, sc1, seq, batch)

    p_head = _project(hh, w_in, 0, head_cols, m, BF16)
    p_kv = _project(hh, w_in, head_cols, 2 * kv_width, m + mc, BF16)
    p_f = _project(hh, w_in, head_cols + 2 * kv_width, 2 * hgrn_width, m + mc, F32)
    p_v = _project(hh, w_in, head_cols + 2 * kv_width + 2 * hgrn_width, hgrn_width, m + mc, BF16)

    cos, sin = _rope_tables(seq)
    attn = _attention(p_head, p_kv, q_gain, k_gain, cos, sin, batch, seq, ctx, kv_heads)
    hgrn = _hgrn(p_head, p_f, p_v, lower_bounds, hg_gain, batch, seq, ctx, attn_width, hgrn_width)
    y = _out_projection(attn, hgrn, w_out)

    x1, h2, logits_t = _epilogue(x2, y, gains[1], gains[2], g1, sh2, sc2, w_router.T, b_router, seq)
    idx, wts, pos, cnt = _route(logits_t)

    row_tile = 256
    n_tiles = -(-(TOP_K * m + n_experts * (row_tile - 1)) // row_tile)
    slots, tile_expert, starts, ends = _slots(idx, pos, cnt, row_tile, n_tiles)
    tile_expert = tile_expert[0, :n_tiles]
    n_used = ends[n_experts - 1, 0:1]
    starts, counts = starts[:, 0], cnt[:, 0]

    tok_tile = _tile(seq, 128, 8)
    nt = m // tok_tile
    slot_tiles = slots.reshape(TOP_K, nt, tok_tile).transpose(1, 0, 2).reshape(nt, 1, TOP_K * tok_tile)
    tok_of_slot = _inverse_slots(starts, counts, n_used, slot_tiles, n_tiles * row_tile, tok_tile, row_tile)
    xs = _dispatch(n_used, tile_expert, starts, counts, tok_of_slot, h2, row_tile)
    hmid = _expert_up(tile_expert, n_used, xs, w_gate, w_up, b_gate, b_up, row_tile)
    ye = _expert_down(tile_expert, n_used, hmid, w_down, b_down, row_tile, _tile(d, 2048, 256))
    out = _combine(slot_tiles, ye, wts.T, x1, gains[3], g2, seq, tok_tile)
    return out.reshape(batch, seq, d)


def kernel(x, c, ctx, c_ctx, w_ada, b_ada, norm_gains, w_in, q_norm_gain, k_norm_gain, hgrn_lb_logits,
           hgrn_norm_gain, w_out, w_router, b_router, w_gate, b_gate, w_up, b_up, w_down, b_down):
    assert w_ada.shape[0] == 1, "single-layer stack only"
    lower_bounds = jnp.cumsum(jax.nn.softmax(hgrn_lb_logits.astype(F32), axis=0), axis=0)
    return _layer(x, ctx, c, c_ctx, lower_bounds[0], w_ada[0], b_ada[0], norm_gains[0], w_in[0],
                  q_norm_gain[0], k_norm_gain[0], hgrn_norm_gain[0], w_out[0], w_router[0], b_router[0],
                  w_gate[0], b_gate[0], w_up[0], b_up[0], w_down[0], b_down[0])
```
